```python
import jax, jax.numpy as jnp
from jax import lax
import numpy as np

D_MODEL = 1024
BATCH = 8
SEQ = 4096
DEPTH = 4

CTX_LEN = 256
GRID_W = 64
HEAD_DIM = 64
RET_W = D_MODEL // 4
RET_HEADS = RET_W // HEAD_DIM
RET_CHUNK = 128
ATT_W = D_MODEL // 2
ATT_Q_HEADS = ATT_W // HEAD_DIM
ATT_KV_HEADS = ATT_Q_HEADS // 4
ATT_KV_W = ATT_KV_HEADS * HEAD_DIM
Q_BLOCK = 128
ROPE_THETA = 10000.0
LRU_W = D_MODEL // 4
LRU_BLOCKS = 4
LRU_BLOCK_W = LRU_W // LRU_BLOCKS
CONV_W = 4
LRU_C = 8.0
D_MIX = RET_W + ATT_W + LRU_W
IN_SIZES = (RET_W, RET_W, RET_W, RET_W, ATT_W, ATT_KV_W, ATT_KV_W, LRU_W, LRU_W)
D_IN = sum(IN_SIZES)
D_FF = 4 * D_MODEL
ALPHA = (2.0 * DEPTH) ** 0.25
BETA = (8.0 * DEPTH) ** -0.25
EPS = 1e-6

kernel_name = 'hybrid_retention_gqa_rglru_dit_trunk'


def layer_norm(x, g, b):
    xf = x.astype(jnp.float32)
    mu = xf.mean(-1, keepdims=True)
    var = jnp.square(xf - mu).mean(-1, keepdims=True)
    return ((xf - mu) * lax.rsqrt(var + EPS) * g + b).astype(x.dtype)


def rms_norm(x, g):
    xf = x.astype(jnp.float32)
    return (xf * lax.rsqrt(jnp.mean(xf * xf, -1, keepdims=True) + EPS) * g).astype(x.dtype)


def group_rms(o):
    of = o.astype(jnp.float32)
    return of * lax.rsqrt(jnp.mean(of * of, -1, keepdims=True) + EPS)


def split_cols(p):
    offs = np.cumsum(IN_SIZES)[:-1].tolist()
    return jnp.split(p, offs, axis=-1)


def to_heads(t, n_heads):
    b, n, _ = t.shape
    return t.reshape(b, n, n_heads, HEAD_DIM).transpose(0, 2, 1, 3)


def from_heads(t):
    b, h, n, d = t.shape
    return t.transpose(0, 2, 1, 3).reshape(b, n, h * d)


def axial_rope(rows):
    row = jnp.repeat(jnp.arange(rows, dtype=jnp.float32), GRID_W)
    col = jnp.tile(jnp.arange(GRID_W, dtype=jnp.float32), rows)
    n_freq = HEAD_DIM // 4
    inv = ROPE_THETA ** (-jnp.arange(n_freq, dtype=jnp.float32) / n_freq)
    ang = jnp.concatenate([row[:, None] * inv, col[:, None] * inv], axis=-1)
    return jnp.cos(ang), jnp.sin(ang)


def apply_rope(x, cos, sin):
    half = HEAD_DIM // 2
    x1, x2 = x[..., :half], x[..., half:]
    cb, sb = cos[:, None, :], sin[:, None, :]
    return jnp.concatenate([x1 * cb - x2 * sb, x1 * sb + x2 * cb], axis=-1)


def retention_dir(q, k, v, log_g, s0, include_diag):
    b, h, n, d = q.shape
    nc, cl = n // RET_CHUNK, RET_CHUNK
    qc = q.reshape(b, h, nc, cl, d)
    kc = k.reshape(b, h, nc, cl, d)
    vc = v.reshape(b, h, nc, cl, d)
    idx = jnp.arange(cl, dtype=jnp.float32)
    diff = idx[:, None] - idx[None, :]
    mask = diff >= 0 if include_diag else diff > 0
    lg = log_g[:, None, None]
    decay_mat = jnp.where(mask, jnp.exp(lg * jnp.where(mask, diff, 0.0)), 0.0)
    scores = jnp.einsum('bhcid,bhcjd->bhcij', qc, kc) * decay_mat[:, None]
    intra = jnp.einsum('bhcij,bhcjv->bhciv', scores, vc)
    k_w = jnp.exp(log_g[:, None] * (cl - 1.0 - idx)[None, :])[:, None, :, None]
    contrib = jnp.einsum('bhcjd,bhcjv->cbhdv', kc * k_w, vc)
    chunk_decay = jnp.exp(log_g * cl)[:, None, None]

    def step(s, u):
        return chunk_decay * s + u, s

    _, s_prev = lax.scan(step, s0, contrib)
    q_w = jnp.exp(log_g[:, None] * (idx + 1.0)[None, :])[:, None, :, None]
    inter = jnp.einsum('bhcid,cbhdv->bhciv', qc * q_w, s_prev)
    return (intra + inter).reshape(b, h, n, d)


def retention_final_state(k, v, log_g, reverse):
    n = k.shape[2]
    pos = jnp.arange(n, dtype=jnp.float32)
    expo = pos if reverse else (n - 1.0 - pos)
    w = jnp.exp(log_g[:, None] * expo[None, :])[:, :, None]
    return jnp.einsum('bhnd,bhnv->bhdv', k * w, v)


def retention_bidir(q, k, v, lg_f, lg_b, s_f, s_b):
    fwd = retention_dir(q, k, v, lg_f, s_f, True)
    bwd = retention_dir(jnp.flip(q, 2), jnp.flip(k, 2), jnp.flip(v, 2), lg_b, s_b, False)
    return fwd + jnp.flip(bwd, 2)


def retention_group(q, k, v, g, qc, kc, vc, gc, decay_logit, need_ctx):
    lg_f = jax.nn.log_sigmoid(decay_logit[0].astype(jnp.float32))
    lg_b = jax.nn.log_sigmoid(decay_logit[1].astype(jnp.float32))
    scale = HEAD_DIM ** -0.5
    q, k, v = to_heads(q, RET_HEADS), to_heads(k, RET_HEADS) * scale, to_heads(v, RET_HEADS)
    qc, kc, vc = to_heads(qc, RET_HEADS), to_heads(kc, RET_HEADS) * scale, to_heads(vc, RET_HEADS)
    s_f = retention_final_state(kc, vc, lg_f, False)
    s_b = retention_final_state(kc, vc, lg_b, True)
    o = retention_bidir(q, k, v, lg_f, lg_b, s_f, s_b)
    out = from_heads(group_rms(o)) * jax.nn.silu(g)
    out_c = None
    if need_ctx:
        z = jnp.zeros_like(s_f)
        oc = retention_bidir(qc, kc, vc, lg_f, lg_b, z, z)
        out_c = from_heads(group_rms(oc)) * jax.nn.silu(gc)
    return out, out_c


def block_attention(q, k, v):
    b, n, hq, d = q.shape
    hkv = k.shape[2]
    grp = hq // hkv
    nb = n // Q_BLOCK
    qb = q.reshape(b, nb, Q_BLOCK, hkv, grp, d).transpose(1, 0, 2, 3, 4, 5)
    scale = HEAD_DIM ** -0.5

    def one(qblk):
        s = jnp.einsum('bqhgd,bmhd->bhgqm', qblk, k).astype(jnp.float32) * scale
        p = jax.nn.softmax(s, axis=-1).astype(v.dtype)
        return jnp.einsum('bhgqm,bmhd->bqhgd', p, v)

    o = lax.map(one, qb)
    return o.transpose(1, 0, 2, 3, 4, 5).reshape(b, n, hq * d)


def attention_group(q, k, v, qc, kc, vc, q_gain, k_gain, cos, sin, need_ctx):
    b, n, _ = q.shape
    lc = qc.shape[1]
    q = apply_rope(rms_norm(q.reshape(b, n, ATT_Q_HEADS, HEAD_DIM), q_gain), cos, sin)
    k = apply_rope(rms_norm(k.reshape(b, n, ATT_KV_HEADS, HEAD_DIM), k_gain), cos, sin)
    v = v.reshape(b, n, ATT_KV_HEADS, HEAD_DIM)
    kc = rms_norm(kc.reshape(b, lc, ATT_KV_HEADS, HEAD_DIM), k_gain)
    vc = vc.reshape(b, lc, ATT_KV_HEADS, HEAD_DIM)
    k_all = jnp.concatenate([k, kc.astype(k.dtype)], axis=1)
    v_all = jnp.concatenate([v, vc], axis=1)
    out = block_attention(q, k_all, v_all)
    out_c = None
    if need_ctx:
        qc = rms_norm(qc.reshape(b, lc, ATT_Q_HEADS, HEAD_DIM), q_gain)
        out_c = block_attention(qc, kc, vc)
    return out, out_c


def centred_dwconv(x, w, b):
    n = x.shape[1]
    xp = jnp.pad(x, ((0, 0), (CONV_W // 2, CONV_W - 1 - CONV_W // 2), (0, 0)))
    acc = xp[:, 0:n] * w[0]
    for j in range(1, CONV_W):
        acc = acc + xp[:, j:j + n] * w[j]
    return acc + b


def rglru_coeffs(xr, w_a, b_a, w_x, b_x, lam):
    blocks = xr.reshape(*xr.shape[:-1], LRU_BLOCKS, LRU_BLOCK_W)
    r = jax.nn.sigmoid(jnp.einsum('bnkc,kcd->bnkd', blocks, w_a).reshape(xr.shape) + b_a)
    i = jax.nn.sigmoid(jnp.einsum('bnkc,kcd->bnkd', blocks, w_x).reshape(xr.shape) + b_x)
    log_a = LRU_C * r.astype(jnp.float32) * jax.nn.log_sigmoid(lam.astype(jnp.float32))
    a = jnp.exp(log_a)
    drive = jnp.sqrt(-jnp.expm1(2.0 * log_a)) * (i * xr)
    return a, drive


def linear_scan(a, bdrive, h0):
    bdrive = bdrive.at[:, 0].add(a[:, 0] * h0)

    def comb(lhs, rhs):
        return lhs[0] * rhs[0], rhs[0] * lhs[1] + rhs[1]

    _, h = lax.associative_scan(comb, (a, bdrive), axis=1)
    return h


def rglru_group(xr, gb, xrc, gbc, conv_w, conv_b, w_a, b_a, w_x, b_x, lam, need_ctx):
    xr = centred_dwconv(xr, conv_w, conv_b)
    xrc = centred_dwconv(xrc, conv_w, conv_b)
    a_f, d_f = rglru_coeffs(xr, w_a[0], b_a[0], w_x[0], b_x[0], lam[0])
    a_b, d_b = rglru_coeffs(xr, w_a[1], b_a[1], w_x[1], b_x[1], lam[1])
    ac_f, dc_f = rglru_coeffs(xrc, w_a[0], b_a[0], w_x[0], b_x[0], lam[0])
    ac_b, dc_b = rglru_coeffs(xrc, w_a[1], b_a[1], w_x[1], b_x[1], lam[1])
    h0 = jnp.zeros((xr.shape[0], LRU_W), a_f.dtype)
    hc_f = linear_scan(ac_f, dc_f, h0)
    hc_b = jnp.flip(linear_scan(jnp.flip(ac_b, 1), jnp.flip(dc_b, 1), h0), 1)
    h_f = linear_scan(a_f, d_f, hc_f[:, -1])
    h_b = jnp.flip(linear_scan(jnp.flip(a_b, 1), jnp.flip(d_b, 1), hc_b[:, 0]), 1)
    out = (h_f + h_b) * jax.nn.gelu(gb)
    out_c = (hc_f + hc_b) * jax.nn.gelu(gbc) if need_ctx else None
    return out, out_c


def token_mixers(u, uc, w_in, decay_logit, q_gain, k_gain, conv_w, conv_b,
                 w_a, b_a, w_x, b_x, lam, cos, sin, need_ctx):
    rq, rk, rv, rg, aq, ak, av, lx, lg = split_cols(u @ w_in)
    rqc, rkc, rvc, rgc, aqc, akc, avc, lxc, lgc = split_cols(uc @ w_in)
    ret, ret_c = retention_group(rq, rk, rv, rg, rqc, rkc, rvc, rgc, decay_logit, need_ctx)
    att, att_c = attention_group(aq, ak, av, aqc, akc, avc, q_gain, k_gain, cos, sin, need_ctx)
    lru, lru_c = rglru_group(lx, lg, lxc, lgc, conv_w, conv_b, w_a, b_a, w_x, b_x, lam, need_ctx)
    mix = jnp.concatenate([ret.astype(u.dtype), att.astype(u.dtype), lru.astype(u.dtype)], axis=-1)
    mix_c = None
    if need_ctx:
        mix_c = jnp.concatenate([ret_c.astype(u.dtype), att_c.astype(u.dtype), lru_c.astype(u.dtype)], axis=-1)
    return mix, mix_c


def squared_relu_mlp(u, w1, w2):
    return jnp.square(jax.nn.relu(u @ w1)) @ w2


def setup_inputs(seed: int = 0) -> dict:
    key = jax.random.key(seed)
    ks = jax.random.split(key, 26)

    def nrm(k, shape, s):
        return jax.random.normal(k, shape, jnp.float32) * s

    gam = 1.0 - 2.0 ** (-5.0 - np.arange(RET_HEADS))
    base_logit = jnp.asarray(np.log(gam / (1.0 - gam)).astype(np.float32))
    a0 = jax.random.uniform(ks[10], (DEPTH, 2, LRU_W), jnp.float32, minval=0.9, maxval=0.999)
    sig = a0 ** (1.0 / LRU_C)
    return {
        'x': nrm(ks[0], (BATCH, SEQ, D_MODEL), 1.0),
        'c': nrm(ks[1], (BATCH, D_MODEL), 1.0),
        'ctx': nrm(ks[2], (BATCH, CTX_LEN, D_MODEL), 1.0),
        'c_ctx': nrm(ks[3], (D_MODEL,), 1.0),
        'w_ada': nrm(ks[4], (DEPTH, D_MODEL, 6 * D_MODEL), D_MODEL ** -0.5),
        'b_ada': nrm(ks[5], (DEPTH, 6 * D_MODEL), 0.02),
        'w_in': nrm(ks[6], (DEPTH, D_MODEL, D_IN), D_MODEL ** -0.5),
        'ret_decay_logit': base_logit + nrm(ks[7], (DEPTH, 2, RET_HEADS), 0.01),
        'attn_q_gain': 1.0 + nrm(ks[8], (DEPTH, HEAD_DIM), 0.02),
        'attn_k_gain': 1.0 + nrm(ks[9], (DEPTH, HEAD_DIM), 0.02),
        'lru_conv_w': nrm(ks[11], (DEPTH, CONV_W, LRU_W), CONV_W ** -0.5),
        'lru_conv_b': nrm(ks[12], (DEPTH, LRU_W), 0.02),
        'lru_w_a': nrm(ks[13], (DEPTH, 2, LRU_BLOCKS, LRU_BLOCK_W, LRU_BLOCK_W), LRU_BLOCK_W ** -0.5),
        'lru_b_a': nrm(ks[14], (DEPTH, 2, LRU_W), 0.02),
        'lru_w_x': nrm(ks[15], (DEPTH, 2, LRU_BLOCKS, LRU_BLOCK_W, LRU_BLOCK_W), LRU_BLOCK_W ** -0.5),
        'lru_b_x': nrm(ks[16], (DEPTH, 2, LRU_W), 0.02),
        'lru_lambda': jnp.log(sig) - jnp.log1p(-sig),
        'w_out': nrm(ks[17], (DEPTH, D_MIX, D_MODEL), BETA * D_MIX ** -0.5),
        'ln1_g': 1.0 + nrm(ks[18], (DEPTH, D_MODEL), 0.02),
        'ln1_b': nrm(ks[19], (DEPTH, D_MODEL), 0.02),
        'w_ff1': nrm(ks[20], (DEPTH, D_MODEL, D_FF), D_MODEL ** -0.5),
        'w_ff2': nrm(ks[21], (DEPTH, D_FF, D_MODEL), BETA * D_FF ** -0.5),
        'ln2_g': 1.0 + nrm(ks[22], (DEPTH, D_MODEL), 0.02),
        'ln2_b': nrm(ks[23], (DEPTH, D_MODEL), 0.02),
    }


def reference(x, c, ctx, c_ctx, w_ada, b_ada, w_in, ret_decay_logit, attn_q_gain, attn_k_gain,
              lru_conv_w, lru_conv_b, lru_w_a, lru_b_a, lru_w_x, lru_b_x, lru_lambda,
              w_out, ln1_g, ln1_b, w_ff1, w_ff2, ln2_g, ln2_b):
    n_lat = x.shape[1]
    rows = n_lat // GRID_W
    cos, sin = axial_rope(rows)
    s_c = jax.nn.silu(c)
    s_cc = jax.nn.silu(c_ctx)
    xc = ctx
    for l in range(DEPTH):
        need_ctx = l < DEPTH - 1
        mod = (s_c @ w_ada[l] + b_ada[l])[:, None, :]
        sh1, sc1, g1, sh2, sc2, g2 = jnp.split(mod, 6, axis=-1)
        modc = s_cc @ w_ada[l] + b_ada[l]
        csh1, csc1, cg1, csh2, csc2, cg2 = jnp.split(modc, 6, axis=-1)
        mix, mix_c = token_mixers(x * (1.0 + sc1) + sh1, xc * (1.0 + csc1) + csh1, w_in[l],
                                  ret_decay_logit[l], attn_q_gain[l], attn_k_gain[l],
                                  lru_conv_w[l], lru_conv_b[l], lru_w_a[l], lru_b_a[l],
                                  lru_w_x[l], lru_b_x[l], lru_lambda[l], cos, sin, need_ctx)
        x = layer_norm(ALPHA * x + g1 * (mix @ w_out[l]), ln1_g[l], ln1_b[l])
        ff = squared_relu_mlp(x * (1.0 + sc2) + sh2, w_ff1[l], w_ff2[l])
        x = layer_norm(ALPHA * x + g2 * ff, ln2_g[l], ln2_b[l])
        if need_ctx:
            xc = layer_norm(ALPHA * xc + cg1 * (mix_c @ w_out[l]), ln1_g[l], ln1_b[l])
            ffc = squared_relu_mlp(xc * (1.0 + csc2) + csh2, w_ff1[l], w_ff2[l])
            xc = layer_norm(ALPHA * xc + cg2 * ffc, ln2_g[l], ln2_b[l])
    return x
```

```python
import functools
import math

import jax
import jax.numpy as jnp
from jax import lax
from jax.experimental import pallas as pl
from jax.experimental.pallas import tpu as pltpu

F32 = jnp.float32
BF16 = jnp.bfloat16

HEAD_DIM = 64
GRID_W = 64
ROPE_THETA = 10000.0
LRU_C = 8.0
LRU_BLOCKS = 4
EPS = 1e-6
LOG2E = 1.4426950408889634

CHUNK = 256
SCAN_TILE = 128
VMEM_LIMIT = 56 * 1024 * 1024


def _cparams(sem):
    return pltpu.CompilerParams(dimension_semantics=sem, vmem_limit_bytes=VMEM_LIMIT)


def _log_sigmoid(x):
    return jnp.minimum(x, 0.0) - jnp.log1p(jnp.exp(-jnp.abs(x)))


def _layer_norm(z, g, b):
    mu = jnp.mean(z, axis=-1, keepdims=True)
    zc = z - mu
    var = jnp.mean(zc * zc, axis=-1, keepdims=True)
    return zc * lax.rsqrt(var + EPS) * g + b


def _row_mod(mod_ref, idx, is_ctx):
    return jnp.where(is_ctx, mod_ref[1, idx:idx + 1, :], mod_ref[0, idx:idx + 1, :])


def _ctx_rows(tile_idx, tm, n_lat):
    rows = tile_idx * tm + lax.broadcasted_iota(jnp.int32, (tm, 1), 0)
    return rows >= n_lat


def _ada_kernel(s_ref, w_ref, b_ref, o_ref):
    s = s_ref[...]
    s = s * jax.nn.sigmoid(s)
    o_ref[...] = jnp.dot(s.astype(BF16), w_ref[...].astype(BF16),
                         preferred_element_type=F32) + b_ref[...]


def _ada_call(s_in, w_ada, b_ada):
    depth, d, d6 = w_ada.shape
    rows = s_in.shape[0]
    tn = d6 // 4
    return pl.pallas_call(
        _ada_kernel,
        grid=(depth, d6 // tn),
        in_specs=[
            pl.BlockSpec((rows, d), lambda l, j: (0, 0)),
            pl.BlockSpec((None, d, tn), lambda l, j: (l, 0, j)),
            pl.BlockSpec((None, 1, tn), lambda l, j: (l, 0, j)),
        ],
        out_specs=pl.BlockSpec((None, rows, tn), lambda l, j: (l, 0, j)),
        out_shape=jax.ShapeDtypeStruct((depth, rows, d6), F32),
        compiler_params=_cparams(("parallel", "parallel")),
    )(s_in, w_ada, b_ada.reshape(depth, 1, d6))


def _inproj_kernel(x_ref, mod_ref, wn_ref, wt_ref, cos_ref, sin_ref, qg_ref, kg_ref,
                   ret_ref, g_ref, lru_ref, qt_ref, k_ref, vt_ref, *, n_lat):
    tm = x_ref.shape[0]
    is_ctx = _ctx_rows(pl.program_id(1), tm, n_lat)
    x = x_ref[...]
    u = (x * (1.0 + _row_mod(mod_ref, 1, is_ctx)) + _row_mod(mod_ref, 0, is_ctx)).astype(BF16)

    pn = jnp.dot(u, wn_ref[...], preferred_element_type=F32)
    rw = g_ref.shape[1]
    ret_ref[:, 0:rw] = pn[:, 0:rw].astype(BF16)
    ret_ref[:, rw:2 * rw] = (pn[:, rw:2 * rw] * HEAD_DIM ** -0.5).astype(BF16)
    ret_ref[:, 2 * rw:3 * rw] = pn[:, 2 * rw:3 * rw].astype(BF16)
    g_ref[...] = pn[:, 3 * rw:4 * rw]
    lru_ref[...] = pn[:, 4 * rw:]

    pt = lax.dot_general(wt_ref[...], u, (((1,), (1,)), ((), ())), preferred_element_type=F32)
    cos = cos_ref[...]
    sin = sin_ref[...]
    half = HEAD_DIM // 2

    def norm_rope(t, gain):
        ms = jnp.mean(t * t, axis=0, keepdims=True)
        t = t * lax.rsqrt(ms + EPS) * gain
        x1, x2 = t[:half], t[half:]
        return jnp.concatenate([x1 * cos - x2 * sin, x1 * sin + x2 * cos], axis=0)

    qw = qt_ref.shape[0]
    kw = k_ref.shape[1]
    qscale = HEAD_DIM ** -0.5 * LOG2E
    for h in range(qw // HEAD_DIM):
        r = h * HEAD_DIM
        qt_ref[r:r + HEAD_DIM, :] = (norm_rope(pt[r:r + HEAD_DIM], qg_ref[...]) * qscale).astype(BF16)
    kt = jnp.concatenate(
        [norm_rope(pt[qw + h * HEAD_DIM:qw + (h + 1) * HEAD_DIM], kg_ref[...])
         for h in range(kw // HEAD_DIM)], axis=0)
    k_ref[...] = kt.T.astype(BF16)
    vt_ref[...] = pt[qw + kw:].astype(BF16)


def _inproj_call(x_all, mod, wn, wt, cos_t, sin_t, qg, kg, n_lat, tm=256):
    b, s, d = x_all.shape
    rw, qw, kw = d // 4, d // 2, d // 8
    grid = (b, s // tm)
    tok = lambda shape_last: pl.BlockSpec((None, tm, shape_last), lambda i, j: (i, j, 0))
    tr = lambda rows: pl.BlockSpec((None, rows, tm), lambda i, j: (i, 0, j))
    const = lambda shape: pl.BlockSpec(shape, lambda i, j: tuple(0 for _ in shape))
    return pl.pallas_call(
        functools.partial(_inproj_kernel, n_lat=n_lat),
        grid=grid,
        in_specs=[
            tok(d),
            pl.BlockSpec((None, 2, 6, d), lambda i, j: (i, 0, 0, 0)),
            const(wn.shape), const(wt.shape),
            pl.BlockSpec((HEAD_DIM // 2, tm), lambda i, j: (0, j)),
            pl.BlockSpec((HEAD_DIM // 2, tm), lambda i, j: (0, j)),
            const(qg.shape), const(kg.shape),
        ],
        out_specs=[tok(3 * rw), tok(rw), tok(2 * rw), tr(qw), tok(kw), tr(kw)],
        out_shape=[
            jax.ShapeDtypeStruct((b, s, 3 * rw), BF16),
            jax.ShapeDtypeStruct((b, s, rw), F32),
            jax.ShapeDtypeStruct((b, s, 2 * rw), F32),
            jax.ShapeDtypeStruct((b, qw, s), BF16),
            jax.ShapeDtypeStruct((b, s, kw), BF16),
            jax.ShapeDtypeStruct((b, kw, s), BF16),
        ],
        compiler_params=_cparams(("parallel", "parallel")),
    )(x_all, mod, wn, wt, cos_t, sin_t, qg, kg)


def _ret_kernel(ret_ref, g_ref, dl_ref, o_ref, sf_ref, sb_ref, dm_ref, *, n_lat_chunks):
    c_len = CHUNK
    s_len, rw = g_ref.shape
    nc = s_len // c_len
    ncl = n_lat_chunks
    n_heads = rw // HEAD_DIM

    lg = _log_sigmoid(dl_ref[...])
    lgf, lgb = lg[0:1], lg[1:2]
    pos = lax.broadcasted_iota(jnp.int32, (c_len, 1), 0).astype(F32)
    kwf = jnp.exp(lgf * (c_len - 1.0 - pos))
    kwb = jnp.exp(lgb * pos)
    qwf = jnp.exp(lgf * (pos + 1.0))
    qwb = jnp.exp(lgb * (c_len - pos))
    cdf = jnp.exp(lgf * c_len)
    cdb = jnp.exp(lgb * c_len)
    row_head = lax.broadcasted_iota(jnp.int32, (rw, rw), 0) // HEAD_DIM
    col_head = lax.broadcasted_iota(jnp.int32, (rw, rw), 1) // HEAD_DIM
    same_head = row_head == col_head
    lane_head = lax.broadcasted_iota(jnp.int32, (1, rw), 1) // HEAD_DIM

    ii = lax.broadcasted_iota(jnp.int32, (c_len, c_len), 0)
    jj = lax.broadcasted_iota(jnp.int32, (c_len, c_len), 1)
    dij = (ii - jj).astype(F32)
    for h in range(n_heads):
        lf = lgf[:, h * HEAD_DIM:h * HEAD_DIM + 1]
        lb = lgb[:, h * HEAD_DIM:h * HEAD_DIM + 1]
        dm_ref[h] = jnp.exp(jnp.where(dij >= 0.0, lf * dij, -lb * dij))

    def contrib(c, carry):
        r0 = pl.multiple_of(c * c_len, c_len)
        k = ret_ref[pl.ds(r0, c_len), rw:2 * rw].astype(F32)
        v = ret_ref[pl.ds(r0, c_len), 2 * rw:3 * rw]
        tn = (((0,), (0,)), ((), ()))
        cf = lax.dot_general((k * kwf).astype(BF16), v, tn, preferred_element_type=F32)
        cb = lax.dot_general((k * kwb).astype(BF16), v, tn, preferred_element_type=F32)
        sf_ref[c] = jnp.where(same_head, cf, 0.0)
        sb_ref[c] = jnp.where(same_head, cb, 0.0)
        return carry

    lax.fori_loop(0, nc, contrib, 0)

    def chain(ref, decay, order):
        state = jnp.zeros((rw, rw), F32)
        for c in order:
            u = ref[c]
            ref[c] = state
            state = decay * state + u

    chain(sf_ref, cdf, list(range(ncl, nc)) + list(range(ncl)))
    chain(sb_ref, cdb, list(range(nc - 1, ncl - 1, -1)) + list(range(ncl - 1, -1, -1)))

    ones_blk = jnp.where(same_head, 1.0, 0.0).astype(BF16)

    def outputs(c, carry):
        r0 = pl.multiple_of(c * c_len, c_len)
        q = ret_ref[pl.ds(r0, c_len), 0:rw]
        k = ret_ref[pl.ds(r0, c_len), rw:2 * rw]
        v = ret_ref[pl.ds(r0, c_len), 2 * rw:3 * rw]
        qf = q.astype(F32)
        qi = jnp.concatenate([(qf * qwf).astype(BF16), (qf * qwb).astype(BF16)], axis=1)
        st = jnp.concatenate([sf_ref[c], sb_ref[c]], axis=0).astype(BF16)
        o = jnp.dot(qi, st, preferred_element_type=F32)
        for h in range(n_heads):
            hm = lane_head == h
            qh = jnp.where(hm, q, jnp.zeros_like(q))
            sc = lax.dot_general(qh, k, (((1,), (1,)), ((), ())), preferred_element_type=F32)
            p = (sc * dm_ref[h]).astype(BF16)
            o = o + jnp.where(hm, jnp.dot(p, v, preferred_element_type=F32), 0.0)
        o2 = o * o
        hi = o2.astype(BF16)
        lo = (o2 - hi.astype(F32)).astype(BF16)
        ms = (jnp.dot(hi, ones_blk, preferred_element_type=F32)
              + jnp.dot(lo, ones_blk, preferred_element_type=F32)) * (1.0 / HEAD_DIM)
        g = g_ref[pl.ds(r0, c_len), :]
        o_ref[pl.ds(r0, c_len), :] = (o * lax.rsqrt(ms + EPS) * (g * jax.nn.sigmoid(g))).astype(BF16)
        return carry

    lax.fori_loop(0, nc, outputs, 0)


def _ret_call(ret_in, g_in, dl_lane, n_lat):
    b, s, rw = g_in.shape
    nc = s // CHUNK
    return pl.pallas_call(
        functools.partial(_ret_kernel, n_lat_chunks=n_lat // CHUNK),
        grid=(b,),
        in_specs=[
            pl.BlockSpec((None, s, 3 * rw), lambda i: (i, 0, 0)),
            pl.BlockSpec((None, s, rw), lambda i: (i, 0, 0)),
            pl.BlockSpec((2, rw), lambda i: (0, 0)),
        ],
        out_specs=pl.BlockSpec((None, s, rw), lambda i: (i, 0, 0)),
        out_shape=jax.ShapeDtypeStruct((b, s, rw), BF16),
        scratch_shapes=[
            pltpu.VMEM((nc, rw, rw), F32),
            pltpu.VMEM((nc, rw, rw), F32),
            pltpu.VMEM((rw // HEAD_DIM, CHUNK, CHUNK), F32),
        ],
        compiler_params=_cparams(("parallel",)),
    )(ret_in, g_in, dl_lane)


def _attn_kernel(qt_ref, k_ref, vt_ref, o_ref, s_ref, acc_ref, *, n_lat, n_lat_tiles, group):
    tq = qt_ref.shape[1]
    n_q_heads = qt_ref.shape[0] // HEAD_DIM
    is_ctx_tile = pl.program_id(1) >= n_lat_tiles

    def head(hq, carry):
        r = pl.multiple_of(hq * HEAD_DIM, HEAD_DIM)
        qt = qt_ref[pl.ds(r, HEAD_DIM), :]
        kv = hq // group
        zero = jnp.zeros_like(qt)
        rhs = jnp.concatenate([jnp.where(kv == 0, qt, zero), jnp.where(kv == 1, qt, zero)], axis=0)
        s_ref[...] = jnp.dot(k_ref[...], rhs, preferred_element_type=F32)

        @pl.when(is_ctx_tile)
        def _():
            s_ref[0:n_lat, :] = jnp.full((n_lat, tq), -1e30, F32)

        s = s_ref[...]
        m = jnp.max(s, axis=0, keepdims=True)
        p = jnp.exp2(s - m)
        l = jnp.sum(p, axis=0, keepdims=True)
        rv = pl.multiple_of(kv * HEAD_DIM, HEAD_DIM)
        ot = jnp.dot(vt_ref[pl.ds(rv, HEAD_DIM), :], p.astype(BF16), preferred_element_type=F32)
        acc_ref[pl.ds(r, HEAD_DIM), :] = ot / l
        return carry

    lax.fori_loop(0, n_q_heads, head, 0)
    o_ref[...] = acc_ref[...].T.astype(BF16)


def _attn_call(qt, k, vt, n_lat, with_ctx, tq=256):
    b, qw, s = qt.shape
    kw = k.shape[2]
    n_lat_tiles = n_lat // tq
    n_tiles = s // tq if with_ctx else n_lat_tiles
    group = (qw // HEAD_DIM) // (kw // HEAD_DIM)
    assert kw == 2 * HEAD_DIM
    return pl.pallas_call(
        functools.partial(_attn_kernel, n_lat=n_lat, n_lat_tiles=n_lat_tiles, group=group),
        grid=(b, n_tiles),
        in_specs=[
            pl.BlockSpec((None, qw, tq), lambda i, j: (i, 0, j)),
            pl.BlockSpec((None, s, kw), lambda i, j: (i, 0, 0)),
            pl.BlockSpec((None, kw, s), lambda i, j: (i, 0, 0)),
        ],
        out_specs=pl.BlockSpec((None, tq, qw), lambda i, j: (i, j, 0)),
        out_shape=jax.ShapeDtypeStruct((b, s, qw), BF16),
        scratch_shapes=[pltpu.VMEM((s, tq), F32), pltpu.VMEM((qw, tq), F32)],
        compiler_params=_cparams(("parallel", "arbitrary")),
    )(qt, k, vt)


def _neg_expm1(y, a):
    series = -y * (1.0 + y * (1.0 / 2) * (1.0 + y * (1.0 / 3) * (1.0 + y * (1.0 / 4) * (
        1.0 + y * (1.0 / 5) * (1.0 + y * (1.0 / 6) * (1.0 + y * (1.0 / 7)))))))
    return jnp.where(y > -0.125, series, 1.0 - a * a)


def _lru_pre_kernel(cur_ref, prev_ref, next_ref, cw_ref, cb_ref, wg_ref, bg_ref, lam_ref,
                    af_ref, df_ref, ab_ref, db_ref, *, n_lat_chunks, n_chunks):
    c = pl.program_id(1)
    c_len, w = cur_ref.shape
    first = jnp.logical_or(c == 0, c == n_lat_chunks)
    last = jnp.logical_or(c == n_lat_chunks - 1, c == n_chunks - 1)
    cur = cur_ref[...]
    prev = jnp.where(first, 0.0, prev_ref[...])
    nxt = jnp.where(last, 0.0, next_ref[...])
    win = jnp.concatenate([prev, cur, nxt], axis=0)
    n_win = c_len + 16

    def shifted(off):
        return pltpu.roll(win, (-off) % n_win, axis=0)[8:8 + c_len]

    xr = (shifted(-2) * cw_ref[0:1, :] + shifted(-1) * cw_ref[1:2, :] + cur * cw_ref[2:3, :]
          + shifted(1) * cw_ref[3:4, :] + cb_ref[...])
    gates = jnp.dot(xr.astype(BF16), wg_ref[...], preferred_element_type=F32) + bg_ref[...]
    log_lam = _log_sigmoid(lam_ref[...])
    for d, (a_ref, d_ref) in enumerate(((af_ref, df_ref), (ab_ref, db_ref))):
        r = jax.nn.sigmoid(gates[:, 2 * d * w:(2 * d + 1) * w])
        i = jax.nn.sigmoid(gates[:, (2 * d + 1) * w:(2 * d + 2) * w])
        log_a = LRU_C * r * log_lam[d:d + 1, :]
        a = jnp.exp(log_a)
        a_ref[...] = a
        d_ref[...] = jnp.sqrt(_neg_expm1(2.0 * log_a, a)) * (i * xr)


def _lru_pre_call(lru_in, conv_w, conv_b, wg, bg, lam, n_lat):
    b, s, w2 = lru_in.shape
    w = w2 // 2
    nc = s // CHUNK
    hb = CHUNK // 8
    const = lambda shape: pl.BlockSpec(shape, lambda i, j: tuple(0 for _ in shape))
    out_spec = pl.BlockSpec((CHUNK, w), lambda i, j: (j, i))
    out_sds = jax.ShapeDtypeStruct((s, b * w), F32)
    return pl.pallas_call(
        functools.partial(_lru_pre_kernel, n_lat_chunks=n_lat // CHUNK, n_chunks=nc),
        grid=(b, nc),
        in_specs=[
            pl.BlockSpec((None, CHUNK, w), lambda i, j: (i, j, 0)),
            pl.BlockSpec((None, 8, w), lambda i, j: (i, jnp.maximum(j * hb - 1, 0), 0)),
            pl.BlockSpec((None, 8, w), lambda i, j: (i, jnp.minimum((j + 1) * hb, s // 8 - 1), 0)),
            const(conv_w.shape), const(conv_b.shape), const(wg.shape), const(bg.shape),
            const(lam.shape),
        ],
        out_specs=[out_spec] * 4,
        out_shape=[out_sds] * 4,
        compiler_params=_cparams(("parallel", "parallel")),
    )(lru_in, lru_in, lru_in, conv_w, conv_b, wg, bg, lam)


def _scan_kernel(af_ref, df_ref, ab_ref, db_ref, hf_ref, hb_ref, sf_ref, sb_ref):
    tt = af_ref.shape[0]

    @pl.when(pl.program_id(0) == 0)
    def _():
        sf_ref[...] = jnp.zeros_like(sf_ref)
        sb_ref[...] = jnp.zeros_like(sb_ref)

    def step(t, carry):
        hf, hb = carry
        hf = af_ref[t] * hf + df_ref[t]
        hf_ref[t] = hf
        tb = tt - 1 - t
        hb = ab_ref[tb] * hb + db_ref[tb]
        hb_ref[tb] = hb
        return hf, hb

    hf, hb = lax.fori_loop(0, tt, step, (sf_ref[...], sb_ref[...]), unroll=8)
    sf_ref[...] = hf
    sb_ref[...] = hb


def _scan_call(af, df, ab, db, n_lat):
    s, b, w = af.shape
    nt = s // SCAN_TILE
    nlt = n_lat // SCAN_TILE
    fwd = pl.BlockSpec((SCAN_TILE, b, w), lambda i: ((i + nlt) % nt, 0, 0))
    bwd = pl.BlockSpec((SCAN_TILE, b, w), lambda i: (nt - 1 - i, 0, 0))
    sds = jax.ShapeDtypeStruct((s, b, w), F32)
    return pl.pallas_call(
        _scan_kernel,
        grid=(nt,),
        in_specs=[fwd, fwd, bwd, bwd],
        out_specs=[fwd, bwd],
        out_shape=[sds, sds],
        scratch_shapes=[pltpu.VMEM((b, w), F32), pltpu.VMEM((b, w), F32)],
        compiler_params=_cparams(("arbitrary",)),
    )(af, df, ab, db)


def _outproj_kernel(x_ref, ret_ref, att_ref, hf_ref, hb_ref, lg_ref, mod_ref, w_ref, g_ref, b_ref,
                    o_ref, *, n_lat, alpha):
    tm = x_ref.shape[0]
    is_ctx = _ctx_rows(pl.program_id(1), tm, n_lat)
    rw = ret_ref.shape[1]
    aw = att_ref.shape[1]
    lru = ((hf_ref[...] + hb_ref[...]) * jax.nn.gelu(lg_ref[...])).astype(BF16)
    y = jnp.dot(ret_ref[...], w_ref[0:rw, :], preferred_element_type=F32)
    y = y + jnp.dot(att_ref[...], w_ref[rw:rw + aw, :], preferred_element_type=F32)
    y = y + jnp.dot(lru, w_ref[rw + aw:, :], preferred_element_type=F32)
    z = alpha * x_ref[...] + _row_mod(mod_ref, 2, is_ctx) * y
    o_ref[...] = _layer_norm(z, g_ref[...], b_ref[...])


def _outproj_call(x_all, ret, att, hf, hb, lru_in, mod, w_out, ln_g, ln_b, n_lat, n_rows, alpha, tm=256):
    b, _, d = x_all.shape
    rw, aw = ret.shape[2], att.shape[2]
    tok = lambda last: pl.BlockSpec((None, tm, last), lambda i, j: (i, j, 0))
    col = pl.BlockSpec((tm, rw), lambda i, j: (j, i))
    const = lambda shape: pl.BlockSpec(shape, lambda i, j: tuple(0 for _ in shape))
    return pl.pallas_call(
        functools.partial(_outproj_kernel, n_lat=n_lat, alpha=alpha),
        grid=(b, n_rows // tm),
        in_specs=[
            tok(d), tok(rw), tok(aw), col, col,
            pl.BlockSpec((None, tm, rw), lambda i, j: (i, j, 1)),
            pl.BlockSpec((None, 2, 6, d), lambda i, j: (i, 0, 0, 0)),
            const(w_out.shape), const(ln_g.shape), const(ln_b.shape),
        ],
        out_specs=tok(d),
        out_shape=jax.ShapeDtypeStruct((b, n_rows, d), F32),
        compiler_params=_cparams(("parallel", "parallel")),
    )(x_all, ret, att, hf, hb, lru_in, mod, w_out, ln_g, ln_b)


def _ffn_kernel(x_ref, mod_ref, w1_ref, w2_ref, g_ref, b_ref, o_ref, u_ref, acc_ref, *, n_lat, alpha):
    tm = x_ref.shape[0]
    kf = pl.program_id(2)
    is_ctx = _ctx_rows(pl.program_id(1), tm, n_lat)

    @pl.when(kf == 0)
    def _():
        u_ref[...] = (x_ref[...] * (1.0 + _row_mod(mod_ref, 4, is_ctx))
                      + _row_mod(mod_ref, 3, is_ctx)).astype(BF16)
        acc_ref[...] = jnp.zeros_like(acc_ref)

    h = jnp.maximum(jnp.dot(u_ref[...], w1_ref[...], preferred_element_type=F32), 0.0)
    acc_ref[...] += jnp.dot((h * h).astype(BF16), w2_ref[...], preferred_element_type=F32)

    @pl.when(kf == pl.num_programs(2) - 1)
    def _():
        z = alpha * x_ref[...] + _row_mod(mod_ref, 5, is_ctx) * acc_ref[...]
        o_ref[...] = _layer_norm(z, g_ref[...], b_ref[...])


def _ffn_call(x1, mod, w1, w2, ln_g, ln_b, n_lat, alpha, tm, tf=1024):
    b, s, d = x1.shape
    f = w1.shape[1]
    tok = pl.BlockSpec((None, tm, d), lambda i, j, k: (i, j, 0))
    const = lambda shape: pl.BlockSpec(shape, lambda i, j, k: tuple(0 for _ in shape))
    return pl.pallas_call(
        functools.partial(_ffn_kernel, n_lat=n_lat, alpha=alpha),
        grid=(b, s // tm, f // tf),
        in_specs=[
            tok,
            pl.BlockSpec((None, 2, 6, d), lambda i, j, k: (i, 0, 0, 0)),
            pl.BlockSpec((d, tf), lambda i, j, k: (0, k)),
            pl.BlockSpec((tf, d), lambda i, j, k: (k, 0)),
            const(ln_g.shape), const(ln_b.shape),
        ],
        out_specs=tok,
        out_shape=jax.ShapeDtypeStruct((b, s, d), F32),
        scratch_shapes=[pltpu.VMEM((tm, d), BF16), pltpu.VMEM((tm, d), F32)],
        compiler_params=_cparams(("parallel", "parallel", "arbitrary")),
    )(x1, mod, w1, w2, ln_g, ln_b)


def _rope_tables(n_lat, n_ctx):
    rows = n_lat // GRID_W
    row = jnp.repeat(jnp.arange(rows, dtype=F32), GRID_W)
    col = jnp.tile(jnp.arange(GRID_W, dtype=F32), rows)
    n_freq = HEAD_DIM // 4
    inv = ROPE_THETA ** (-jnp.arange(n_freq, dtype=F32) / n_freq)
    ang = jnp.concatenate([row[:, None] * inv, col[:, None] * inv], axis=-1)
    cos = jnp.concatenate([jnp.cos(ang), jnp.ones((n_ctx, HEAD_DIM // 2), F32)], axis=0)
    sin = jnp.concatenate([jnp.sin(ang), jnp.zeros((n_ctx, HEAD_DIM // 2), F32)], axis=0)
    return cos.T, sin.T


def _block_diag(w):
    k, c = w.shape[-3], w.shape[-2]
    eye = jnp.eye(k, dtype=w.dtype)
    bd = jnp.einsum('...kce,kj->...kcje', w, eye)
    return bd.reshape(*w.shape[:-3], k * c, k * c)


def kernel(x, c, ctx, c_ctx, w_ada, b_ada, w_in, ret_decay_logit, attn_q_gain, attn_k_gain,
           lru_conv_w, lru_conv_b, lru_w_a, lru_b_a, lru_w_x, lru_b_x, lru_lambda,
           w_out, ln1_g, ln1_b, w_ff1, w_ff2, ln2_g, ln2_b):
    b, n_lat, d = x.shape
    n_ctx = ctx.shape[1]
    depth = w_in.shape[0]
    s = n_lat + n_ctx
    rw, aw, kw = d // 4, d // 2, d // 8
    alpha = (2.0 * depth) ** 0.25
    assert n_lat % CHUNK == 0 and n_ctx % CHUNK == 0 and d == 16 * HEAD_DIM

    pad = (-(b + 1)) % 8
    s_in = jnp.concatenate([c, c_ctx[None, :], jnp.zeros((pad, d), F32)], axis=0)
    mods = _ada_call(s_in, w_ada, b_ada)
    mod_lat = mods[:, :b].reshape(depth, b, 1, 6, d)
    mod_ctx = jnp.broadcast_to(mods[:, b].reshape(depth, 1, 1, 6, d), (depth, b, 1, 6, d))
    mod_all = jnp.concatenate([mod_lat, mod_ctx], axis=2)

    o_aq = 4 * rw
    o_lx = o_aq + aw + 2 * kw
    wn = jnp.concatenate([w_in[:, :, :o_aq], w_in[:, :, o_lx:]], axis=2).astype(BF16)
    wt = jnp.swapaxes(w_in[:, :, o_aq:o_lx], 1, 2).astype(BF16)
    wg = jnp.concatenate([_block_diag(lru_w_a[:, 0]), _block_diag(lru_w_x[:, 0]),
                          _block_diag(lru_w_a[:, 1]), _block_diag(lru_w_x[:, 1])], axis=-1).astype(BF16)
    bg = jnp.concatenate([lru_b_a[:, 0], lru_b_x[:, 0], lru_b_a[:, 1], lru_b_x[:, 1]],
                         axis=-1)[:, None, :]
    w_out_b = w_out.astype(BF16)
    w1_b = w_ff1.astype(BF16)
    w2_b = w_ff2.astype(BF16)
    dl_lane = jnp.repeat(ret_decay_logit, HEAD_DIM, axis=-1)
    cos_t, sin_t = _rope_tables(n_lat, n_ctx)

    def ffn_tile(rows):
        return rows // 4 if rows % 32 == 0 else CHUNK

    xa = jnp.concatenate([x, ctx], axis=1)
    for l in range(depth):
        need_ctx = l < depth - 1
        mod = mod_all[l]
        ret_in, g_in, lru_in, qt, k, vt = _inproj_call(
            xa, mod, wn[l], wt[l], cos_t, sin_t,
            attn_q_gain[l][:, None], attn_k_gain[l][:, None], n_lat)
        ret = _ret_call(ret_in, g_in, dl_lane[l], n_lat)
        att = _attn_call(qt, k, vt, n_lat, need_ctx)
        af, df, ab, db = _lru_pre_call(lru_in, lru_conv_w[l], lru_conv_b[l][None, :], wg[l], bg[l],
                                       lru_lambda[l], n_lat)
        shp = (s, b, rw)
        hf, hb = _scan_call(af.reshape(shp), df.reshape(shp), ab.reshape(shp), db.reshape(shp), n_lat)
        n_rows = s if need_ctx else n_lat
        x1 = _outproj_call(xa, ret, att, hf.reshape(s, b * rw), hb.reshape(s, b * rw), lru_in, mod,
                           w_out_b[l], ln1_g[l][None, :], ln1_b[l][None, :], n_lat, n_rows, alpha)
        xa = _ffn_call(x1, mod, w1_b[l], w2_b[l], ln2_g[l][None, :], ln2_b[l][None, :],
                       n_lat, alpha, ffn_tile(n_rows))
    return xa
```

```python
import functools
import math

import jax
import jax.numpy as jnp
from jax import lax
from jax.experimental import pallas as pl
from jax.experimental.pallas import tpu as pltpu

F32 = jnp.float32
BF16 = jnp.bfloat16

HEAD_DIM = 64
GRID_W = 64
ROPE_THETA = 10000.0
LRU_C = 8.0
LRU_BLOCKS = 4
EPS = 1e-6
LOG2E = 1.4426950408889634

CHUNK = 256
SCAN_TILE = 128
VMEM_LIMIT = 56 * 1024 * 1024


def _cparams(sem):
    return pltpu.CompilerParams(dimension_semantics=sem, vmem_limit_bytes=VMEM_LIMIT)


def _log_sigmoid(x):
    return jnp.minimum(x, 0.0) - jnp.log1p(jnp.exp(-jnp.abs(x)))


def _layer_norm(z, g, b):
    mu = jnp.mean(z, axis=-1, keepdims=True)
    zc = z - mu
    var = jnp.mean(zc * zc, axis=-1, keepdims=True)
    return zc * lax.rsqrt(var + EPS) * g + b


def _row_mod(mod_ref, idx, is_ctx):
    return jnp.where(is_ctx, mod_ref[1, idx:idx + 1, :], mod_ref[0, idx:idx + 1, :])


def _ctx_rows(tile_idx, tm, n_lat):
    rows = tile_idx * tm + lax.broadcasted_iota(jnp.int32, (tm, 1), 0)
    return rows >= n_lat


def _ada_kernel(s_ref, w_ref, b_ref, o_ref):
    s = s_ref[...]
    s = s * jax.nn.sigmoid(s)
    o_ref[...] = jnp.dot(s.astype(BF16), w_ref[...].astype(BF16),
                         preferred_element_type=F32) + b_ref[...]


def _ada_call(s_in, w_ada, b_ada):
    depth, d, d6 = w_ada.shape
    rows = s_in.shape[0]
    tn = d6 // 4
    return pl.pallas_call(
        _ada_kernel,
        grid=(depth, d6 // tn),
        in_specs=[
            pl.BlockSpec((rows, d), lambda l, j: (0, 0)),
            pl.BlockSpec((None, d, tn), lambda l, j: (l, 0, j)),
            pl.BlockSpec((None, 1, tn), lambda l, j: (l, 0, j)),
        ],
        out_specs=pl.BlockSpec((None, rows, tn), lambda l, j: (l, 0, j)),
        out_shape=jax.ShapeDtypeStruct((depth, rows, d6), F32),
        compiler_params=_cparams(("parallel", "parallel")),
    )(s_in, w_ada, b_ada.reshape(depth, 1, d6))


def _inproj_kernel(x_ref, mod_ref, wn_ref, wt_ref, cos_ref, sin_ref, qg_ref, kg_ref,
                   ret_ref, g_ref, lru_ref, qt_ref, k_ref, vt_ref, *, n_lat):
    tm = x_ref.shape[0]
    is_ctx = _ctx_rows(pl.program_id(1), tm, n_lat)
    x = x_ref[...]
    u = (x * (1.0 + _row_mod(mod_ref, 1, is_ctx)) + _row_mod(mod_ref, 0, is_ctx)).astype(BF16)

    pn = jnp.dot(u, wn_ref[...], preferred_element_type=F32)
    rw = g_ref.shape[1]
    ret_ref[:, 0:rw] = pn[:, 0:rw].astype(BF16)
    ret_ref[:, rw:2 * rw] = (pn[:, rw:2 * rw] * HEAD_DIM ** -0.5).astype(BF16)
    ret_ref[:, 2 * rw:3 * rw] = pn[:, 2 * rw:3 * rw].astype(BF16)
    g_ref[...] = pn[:, 3 * rw:4 * rw]
    lru_ref[...] = pn[:, 4 * rw:]

    pt = lax.dot_general(wt_ref[...], u, (((1,), (1,)), ((), ())), preferred_element_type=F32)
    cos = cos_ref[...]
    sin = sin_ref[...]
    half = HEAD_DIM // 2

    def norm_rope(t, gain):
        ms = jnp.mean(t * t, axis=0, keepdims=True)
        t = t * lax.rsqrt(ms + EPS) * gain
        x1, x2 = t[:half], t[half:]
        return jnp.concatenate([x1 * cos - x2 * sin, x1 * sin + x2 * cos], axis=0)

    qw = qt_ref.shape[0]
    kw = k_ref.shape[1]
    qscale = HEAD_DIM ** -0.5 * LOG2E
    for h in range(qw // HEAD_DIM):
        r = h * HEAD_DIM
        qt_ref[r:r + HEAD_DIM, :] = (norm_rope(pt[r:r + HEAD_DIM], qg_ref[...]) * qscale).astype(BF16)
    kt = jnp.concatenate(
        [norm_rope(pt[qw + h * HEAD_DIM:qw + (h + 1) * HEAD_DIM], kg_ref[...])
         for h in range(kw // HEAD_DIM)], axis=0)
    k_ref[...] = kt.T.astype(BF16)
    vt_ref[...] = pt[qw + kw:].astype(BF16)


def _inproj_call(x_all, mod, wn, wt, cos_t, sin_t, qg, kg, n_lat, tm=256):
    b, s, d = x_all.shape
    rw, qw, kw = d // 4, d // 2, d // 8
    grid = (b, s // tm)
    tok = lambda shape_last: pl.BlockSpec((None, tm, shape_last), lambda i, j: (i, j, 0))
    tr = lambda rows: pl.BlockSpec((None, rows, tm), lambda i, j: (i, 0, j))
    const = lambda shape: pl.BlockSpec(shape, lambda i, j: tuple(0 for _ in shape))
    return pl.pallas_call(
        functools.partial(_inproj_kernel, n_lat=n_lat),
        grid=grid,
        in_specs=[
            tok(d),
            pl.BlockSpec((None, 2, 6, d), lambda i, j: (i, 0, 0, 0)),
            const(wn.shape), const(wt.shape),
            pl.BlockSpec((HEAD_DIM // 2, tm), lambda i, j: (0, j)),
            pl.BlockSpec((HEAD_DIM // 2, tm), lambda i, j: (0, j)),
            const(qg.shape), const(kg.shape),
        ],
        out_specs=[tok(3 * rw), tok(rw), tok(2 * rw), tr(qw), tok(kw),
                   pl.BlockSpec((None, None, kw, tm), lambda i, j: (i, j, 0, 0))],
        out_shape=[
            jax.ShapeDtypeStruct((b, s, 3 * rw), BF16),
            jax.ShapeDtypeStruct((b, s, rw), F32),
            jax.ShapeDtypeStruct((b, s, 2 * rw), F32),
            jax.ShapeDtypeStruct((b, qw, s), BF16),
            jax.ShapeDtypeStruct((b, s, kw), BF16),
            jax.ShapeDtypeStruct((b, s // tm, kw, tm), BF16),
        ],
        compiler_params=_cparams(("parallel", "parallel")),
    )(x_all, mod, wn, wt, cos_t, sin_t, qg, kg)


def _ret_kernel(ret_ref, g_ref, dl_ref, o_ref, sf_ref, sb_ref, dm_ref, *, n_lat_chunks):
    c_len = CHUNK
    s_len, rw = g_ref.shape
    nc = s_len // c_len
    ncl = n_lat_chunks
    n_heads = rw // HEAD_DIM

    lg = _log_sigmoid(dl_ref[...])
    lgf, lgb = lg[0:1], lg[1:2]
    pos = lax.broadcasted_iota(jnp.int32, (c_len, 1), 0).astype(F32)
    kwf = jnp.exp(lgf * (c_len - 1.0 - pos))
    kwb = jnp.exp(lgb * pos)
    qwf = jnp.exp(lgf * (pos + 1.0))
    qwb = jnp.exp(lgb * (c_len - pos))
    cdf = jnp.exp(lgf * c_len)
    cdb = jnp.exp(lgb * c_len)
    row_head = lax.broadcasted_iota(jnp.int32, (rw, rw), 0) // HEAD_DIM
    col_head = lax.broadcasted_iota(jnp.int32, (rw, rw), 1) // HEAD_DIM
    same_head = row_head == col_head
    lane_head = lax.broadcasted_iota(jnp.int32, (1, rw), 1) // HEAD_DIM

    ii = lax.broadcasted_iota(jnp.int32, (c_len, c_len), 0)
    jj = lax.broadcasted_iota(jnp.int32, (c_len, c_len), 1)
    dij = (ii - jj).astype(F32)
    for h in range(n_heads):
        lf = lgf[:, h * HEAD_DIM:h * HEAD_DIM + 1]
        lb = lgb[:, h * HEAD_DIM:h * HEAD_DIM + 1]
        dm_ref[h] = jnp.exp(jnp.where(dij >= 0.0, lf * dij, -lb * dij))

    def contrib(c, carry):
        r0 = pl.multiple_of(c * c_len, c_len)
        k = ret_ref[pl.ds(r0, c_len), rw:2 * rw].astype(F32)
        v = ret_ref[pl.ds(r0, c_len), 2 * rw:3 * rw]
        tn = (((0,), (0,)), ((), ()))
        cf = lax.dot_general((k * kwf).astype(BF16), v, tn, preferred_element_type=F32)
        cb = lax.dot_general((k * kwb).astype(BF16), v, tn, preferred_element_type=F32)
        sf_ref[c] = jnp.where(same_head, cf, 0.0)
        sb_ref[c] = jnp.where(same_head, cb, 0.0)
        return carry

    lax.fori_loop(0, nc, contrib, 0)

    def chain(ref, decay, order):
        state = jnp.zeros((rw, rw), F32)
        for c in order:
            u = ref[c]
            ref[c] = state
            state = decay * state + u

    chain(sf_ref, cdf, list(range(ncl, nc)) + list(range(ncl)))
    chain(sb_ref, cdb, list(range(nc - 1, ncl - 1, -1)) + list(range(ncl - 1, -1, -1)))

    ones_blk = jnp.where(same_head, 1.0, 0.0).astype(BF16)

    def outputs(c, carry):
        r0 = pl.multiple_of(c * c_len, c_len)
        q = ret_ref[pl.ds(r0, c_len), 0:rw]
        k = ret_ref[pl.ds(r0, c_len), rw:2 * rw]
        v = ret_ref[pl.ds(r0, c_len), 2 * rw:3 * rw]
        qf = q.astype(F32)
        qi = jnp.concatenate([(qf * qwf).astype(BF16), (qf * qwb).astype(BF16)], axis=1)
        st = jnp.concatenate([sf_ref[c], sb_ref[c]], axis=0).astype(BF16)
        o = jnp.dot(qi, st, preferred_element_type=F32)
        for h in range(n_heads):
            hm = lane_head == h
            qh = jnp.where(hm, q, jnp.zeros_like(q))
            sc = lax.dot_general(qh, k, (((1,), (1,)), ((), ())), preferred_element_type=F32)
            p = (sc * dm_ref[h]).astype(BF16)
            o = o + jnp.where(hm, jnp.dot(p, v, preferred_element_type=F32), 0.0)
        o2 = o * o
        hi = o2.astype(BF16)
        lo = (o2 - hi.astype(F32)).astype(BF16)
        ms = (jnp.dot(hi, ones_blk, preferred_element_type=F32)
              + jnp.dot(lo, ones_blk, preferred_element_type=F32)) * (1.0 / HEAD_DIM)
        g = g_ref[pl.ds(r0, c_len), :]
        o_ref[pl.ds(r0, c_len), :] = (o * lax.rsqrt(ms + EPS) * (g * jax.nn.sigmoid(g))).astype(BF16)
        return carry

    lax.fori_loop(0, nc, outputs, 0)


def _ret_call(ret_in, g_in, dl_lane, n_lat):
    b, s, rw = g_in.shape
    nc = s // CHUNK
    return pl.pallas_call(
        functools.partial(_ret_kernel, n_lat_chunks=n_lat // CHUNK),
        grid=(b,),
        in_specs=[
            pl.BlockSpec((None, s, 3 * rw), lambda i: (i, 0, 0)),
            pl.BlockSpec((None, s, rw), lambda i: (i, 0, 0)),
            pl.BlockSpec((2, rw), lambda i: (0, 0)),
        ],
        out_specs=pl.BlockSpec((None, s, rw), lambda i: (i, 0, 0)),
        out_shape=jax.ShapeDtypeStruct((b, s, rw), BF16),
        scratch_shapes=[
            pltpu.VMEM((nc, rw, rw), F32),
            pltpu.VMEM((nc, rw, rw), F32),
            pltpu.VMEM((rw // HEAD_DIM, CHUNK, CHUNK), F32),
        ],
        compiler_params=_cparams(("parallel",)),
    )(ret_in, g_in, dl_lane)


def _attn_kernel(qt_ref, k_ref, vt_ref, *rest, group):
    o_ref, rhs_ref, m_ref, l_ref, acc_ref = rest[-5:]
    tq = qt_ref.shape[1]
    n_q_heads = qt_ref.shape[0] // HEAD_DIM
    n_chunks = vt_ref.shape[0]
    kc_len = vt_ref.shape[2]

    for h in range(n_q_heads):
        qt = qt_ref[h * HEAD_DIM:(h + 1) * HEAD_DIM, :]
        zero = jnp.zeros_like(qt)
        rhs_ref[h] = jnp.concatenate([qt, zero] if h // group == 0 else [zero, qt], axis=0)
    m_ref[...] = jnp.full(m_ref.shape, -jnp.inf, F32)
    l_ref[...] = jnp.zeros(l_ref.shape, F32)
    acc_ref[...] = jnp.zeros(acc_ref.shape, F32)

    def chunk(c, carry):
        c0 = pl.multiple_of(c * kc_len, kc_len)
        kc = k_ref[pl.ds(c0, kc_len), :]
        vtc = vt_ref[c]
        scores = [jnp.dot(kc, rhs_ref[h], preferred_element_type=F32) for h in range(n_q_heads)]
        for h in range(n_q_heads):
            s = scores[h]
            r = h * HEAD_DIM
            kv = h // group
            m_old = m_ref[h:h + 1, :]
            m_new = jnp.maximum(m_old, jnp.max(s, axis=0, keepdims=True))
            alpha = jnp.exp2(m_old - m_new)
            p = jnp.exp2(s - m_new)
            l_ref[h:h + 1, :] = alpha * l_ref[h:h + 1, :] + jnp.sum(p, axis=0, keepdims=True)
            m_ref[h:h + 1, :] = m_new
            pv = jnp.dot(vtc[kv * HEAD_DIM:(kv + 1) * HEAD_DIM, :], p.astype(BF16),
                         preferred_element_type=F32)
            acc_ref[r:r + HEAD_DIM, :] = alpha * acc_ref[r:r + HEAD_DIM, :] + pv
        return carry

    lax.fori_loop(0, n_chunks, chunk, 0)
    for h in range(n_q_heads):
        r = h * HEAD_DIM
        acc_ref[r:r + HEAD_DIM, :] = acc_ref[r:r + HEAD_DIM, :] / l_ref[h:h + 1, :]
    o_ref[...] = acc_ref[...].T.astype(BF16)


def _attn_call(qt, k, vt, n_lat, with_ctx, tq=256):
    b, qw, s = qt.shape
    kw = k.shape[2]
    kc_len = vt.shape[3]
    n_ctx = s - n_lat
    n_heads = qw // HEAD_DIM
    group = n_heads // (kw // HEAD_DIM)
    assert kw == 2 * HEAD_DIM and n_lat % n_ctx == 0 and n_ctx % tq == 0 and n_ctx % kc_len == 0
    body = functools.partial(_attn_kernel, group=group)
    scratch = [pltpu.VMEM((n_heads, kw, tq), BF16), pltpu.VMEM((n_heads, tq), F32),
               pltpu.VMEM((n_heads, tq), F32), pltpu.VMEM((qw, tq), F32)]
    out_sds = jax.ShapeDtypeStruct((b, s, qw), BF16)
    att = pl.pallas_call(
        body,
        grid=(b, n_lat // tq),
        in_specs=[
            pl.BlockSpec((None, qw, tq), lambda i, j: (i, 0, j)),
            pl.BlockSpec((None, s, kw), lambda i, j: (i, 0, 0)),
            pl.BlockSpec((None, s // kc_len, kw, kc_len), lambda i, j: (i, 0, 0, 0)),
        ],
        out_specs=pl.BlockSpec((None, tq, qw), lambda i, j: (i, j, 0)),
        out_shape=out_sds,
        scratch_shapes=scratch,
        compiler_params=_cparams(("parallel", "arbitrary")),
    )(qt, k, vt)
    if not with_ctx:
        return att
    lat_tiles, lat_ctx = n_lat // tq, n_lat // n_ctx
    return pl.pallas_call(
        body,
        grid=(b, n_ctx // tq),
        in_specs=[
            pl.BlockSpec((None, qw, tq), lambda i, j: (i, 0, lat_tiles + j)),
            pl.BlockSpec((None, n_ctx, kw), lambda i, j: (i, lat_ctx, 0)),
            pl.BlockSpec((None, n_ctx // kc_len, kw, kc_len), lambda i, j: (i, lat_ctx, 0, 0)),
            pl.BlockSpec(memory_space=pl.ANY),
        ],
        out_specs=pl.BlockSpec((None, tq, qw), lambda i, j: (i, lat_tiles + j, 0)),
        out_shape=out_sds,
        input_output_aliases={3: 0},
        scratch_shapes=scratch,
        compiler_params=_cparams(("parallel", "arbitrary")),
    )(qt, k, vt, att)


def _neg_expm1(y, a):
    series = -y * (1.0 + y * (1.0 / 2) * (1.0 + y * (1.0 / 3) * (1.0 + y * (1.0 / 4) * (
        1.0 + y * (1.0 / 5) * (1.0 + y * (1.0 / 6) * (1.0 + y * (1.0 / 7)))))))
    return jnp.where(y > -0.125, series, 1.0 - a * a)


def _lru_pre_kernel(cur_ref, prev_ref, next_ref, cw_ref, cb_ref, wg_ref, bg_ref, lam_ref,
                    af_ref, df_ref, ab_ref, db_ref, *, n_lat_chunks, n_chunks):
    c = pl.program_id(1)
    c_len, w = cur_ref.shape
    first = jnp.logical_or(c == 0, c == n_lat_chunks)
    last = jnp.logical_or(c == n_lat_chunks - 1, c == n_chunks - 1)
    cur = cur_ref[...]
    prev = jnp.where(first, 0.0, prev_ref[...])
    nxt = jnp.where(last, 0.0, next_ref[...])
    win = jnp.concatenate([prev, cur, nxt], axis=0)
    n_win = c_len + 16

    def shifted(off):
        return pltpu.roll(win, (-off) % n_win, axis=0)[8:8 + c_len]

    xr = (shifted(-2) * cw_ref[0:1, :] + shifted(-1) * cw_ref[1:2, :] + cur * cw_ref[2:3, :]
          + shifted(1) * cw_ref[3:4, :] + cb_ref[...])
    gates = jnp.dot(xr.astype(BF16), wg_ref[...], preferred_element_type=F32) + bg_ref[...]
    log_lam = _log_sigmoid(lam_ref[...])
    for d, (a_ref, d_ref) in enumerate(((af_ref, df_ref), (ab_ref, db_ref))):
        r = jax.nn.sigmoid(gates[:, 2 * d * w:(2 * d + 1) * w])
        i = jax.nn.sigmoid(gates[:, (2 * d + 1) * w:(2 * d + 2) * w])
        log_a = LRU_C * r * log_lam[d:d + 1, :]
        a = jnp.exp(log_a)
        a_ref[...] = a
        d_ref[...] = jnp.sqrt(_neg_expm1(2.0 * log_a, a)) * (i * xr)


def _lru_pre_call(lru_in, conv_w, conv_b, wg, bg, lam, n_lat):
    b, s, w2 = lru_in.shape
    w = w2 // 2
    nc = s // CHUNK
    hb = CHUNK // 8
    const = lambda shape: pl.BlockSpec(shape, lambda i, j: tuple(0 for _ in shape))
    out_spec = pl.BlockSpec((CHUNK, w), lambda i, j: (j, i))
    out_sds = jax.ShapeDtypeStruct((s, b * w), F32)
    return pl.pallas_call(
        functools.partial(_lru_pre_kernel, n_lat_chunks=n_lat // CHUNK, n_chunks=nc),
        grid=(b, nc),
        in_specs=[
            pl.BlockSpec((None, CHUNK, w), lambda i, j: (i, j, 0)),
            pl.BlockSpec((None, 8, w), lambda i, j: (i, jnp.maximum(j * hb - 1, 0), 0)),
            pl.BlockSpec((None, 8, w), lambda i, j: (i, jnp.minimum((j + 1) * hb, s // 8 - 1), 0)),
            const(conv_w.shape), const(conv_b.shape), const(wg.shape), const(bg.shape),
            const(lam.shape),
        ],
        out_specs=[out_spec] * 4,
        out_shape=[out_sds] * 4,
        compiler_params=_cparams(("parallel", "parallel")),
    )(lru_in, lru_in, lru_in, conv_w, conv_b, wg, bg, lam)


def _scan_kernel(af_ref, df_ref, ab_ref, db_ref, hf_ref, hb_ref, sf_ref, sb_ref):
    tt = af_ref.shape[0]

    @pl.when(pl.program_id(0) == 0)
    def _():
        sf_ref[...] = jnp.zeros_like(sf_ref)
        sb_ref[...] = jnp.zeros_like(sb_ref)

    def step(t, carry):
        hf, hb = carry
        hf = af_ref[t] * hf + df_ref[t]
        hf_ref[t] = hf
        tb = tt - 1 - t
        hb = ab_ref[tb] * hb + db_ref[tb]
        hb_ref[tb] = hb
        return hf, hb

    hf, hb = lax.fori_loop(0, tt, step, (sf_ref[...], sb_ref[...]), unroll=8)
    sf_ref[...] = hf
    sb_ref[...] = hb


def _scan_call(af, df, ab, db, n_lat):
    s, b, w = af.shape
    nt = s // SCAN_TILE
    nlt = n_lat // SCAN_TILE
    fwd = pl.BlockSpec((SCAN_TILE, b, w), lambda i: ((i + nlt) % nt, 0, 0))
    bwd = pl.BlockSpec((SCAN_TILE, b, w), lambda i: (nt - 1 - i, 0, 0))
    sds = jax.ShapeDtypeStruct((s, b, w), F32)
    return pl.pallas_call(
        _scan_kernel,
        grid=(nt,),
        in_specs=[fwd, fwd, bwd, bwd],
        out_specs=[fwd, bwd],
        out_shape=[sds, sds],
        scratch_shapes=[pltpu.VMEM((b, w), F32), pltpu.VMEM((b, w), F32)],
        compiler_params=_cparams(("arbitrary",)),
    )(af, df, ab, db)


def _outproj_kernel(x_ref, ret_ref, att_ref, hf_ref, hb_ref, lg_ref, mod_ref, w_ref, g_ref, b_ref,
                    o_ref, *, n_lat, alpha):
    tm = x_ref.shape[0]
    is_ctx = _ctx_rows(pl.program_id(1), tm, n_lat)
    rw = ret_ref.shape[1]
    aw = att_ref.shape[1]
    lru = ((hf_ref[...] + hb_ref[...]) * jax.nn.gelu(lg_ref[...])).astype(BF16)
    y = jnp.dot(ret_ref[...], w_ref[0:rw, :], preferred_element_type=F32)
    y = y + jnp.dot(att_ref[...], w_ref[rw:rw + aw, :], preferred_element_type=F32)
    y = y + jnp.dot(lru, w_ref[rw + aw:, :], preferred_element_type=F32)
    z = alpha * x_ref[...] + _row_mod(mod_ref, 2, is_ctx) * y
    o_ref[...] = _layer_norm(z, g_ref[...], b_ref[...])


def _outproj_call(x_all, ret, att, hf, hb, lru_in, mod, w_out, ln_g, ln_b, n_lat, n_rows, alpha, tm=256):
    b, _, d = x_all.shape
    rw, aw = ret.shape[2], att.shape[2]
    tok = lambda last: pl.BlockSpec((None, tm, last), lambda i, j: (i, j, 0))
    col = pl.BlockSpec((tm, rw), lambda i, j: (j, i))
    const = lambda shape: pl.BlockSpec(shape, lambda i, j: tuple(0 for _ in shape))
    return pl.pallas_call(
        functools.partial(_outproj_kernel, n_lat=n_lat, alpha=alpha),
        grid=(b, n_rows // tm),
        in_specs=[
            tok(d), tok(rw), tok(aw), col, col,
            pl.BlockSpec((None, tm, rw), lambda i, j: (i, j, 1)),
            pl.BlockSpec((None, 2, 6, d), lambda i, j: (i, 0, 0, 0)),
            const(w_out.shape), const(ln_g.shape), const(ln_b.shape),
        ],
        out_specs=tok(d),
        out_shape=jax.ShapeDtypeStruct((b, n_rows, d), F32),
        compiler_params=_cparams(("parallel", "parallel")),
    )(x_all, ret, att, hf, hb, lru_in, mod, w_out, ln_g, ln_b)


def _ffn_kernel(x_ref, mod_ref, w1_ref, w2_ref, g_ref, b_ref, o_ref, u_ref, acc_ref, *, n_lat, alpha):
    tm = x_ref.shape[0]
    kf = pl.program_id(2)
    is_ctx = _ctx_rows(pl.program_id(1), tm, n_lat)

    @pl.when(kf == 0)
    def _():
        u_ref[...] = (x_ref[...] * (1.0 + _row_mod(mod_ref, 4, is_ctx))
                      + _row_mod(mod_ref, 3, is_ctx)).astype(BF16)
        acc_ref[...] = jnp.zeros_like(acc_ref)

    h = jnp.maximum(jnp.dot(u_ref[...], w1_ref[...], preferred_element_type=F32), 0.0)
    acc_ref[...] += jnp.dot((h * h).astype(BF16), w2_ref[...], preferred_element_type=F32)

    @pl.when(kf == pl.num_programs(2) - 1)
    def _():
        z = alpha * x_ref[...] + _row_mod(mod_ref, 5, is_ctx) * acc_ref[...]
        o_ref[...] = _layer_norm(z, g_ref[...], b_ref[...])


def _ffn_call(x1, mod, w1, w2, ln_g, ln_b, n_lat, alpha, tm, tf=1024):
    b, s, d = x1.shape
    f = w1.shape[1]
    tok = pl.BlockSpec((None, tm, d), lambda i, j, k: (i, j, 0))
    const = lambda shape: pl.BlockSpec(shape, lambda i, j, k: tuple(0 for _ in shape))
    return pl.pallas_call(
        functools.partial(_ffn_kernel, n_lat=n_lat, alpha=alpha),
        grid=(b, s // tm, f // tf),
        in_specs=[
            tok,
            pl.BlockSpec((None, 2, 6, d), lambda i, j, k: (i, 0, 0, 0)),
            pl.BlockSpec((d, tf), lambda i, j, k: (0, k)),
            pl.BlockSpec((tf, d), lambda i, j, k: (k, 0)),
            const(ln_g.shape), const(ln_b.shape),
        ],
        out_specs=tok,
        out_shape=jax.ShapeDtypeStruct((b, s, d), F32),
        scratch_shapes=[pltpu.VMEM((tm, d), BF16), pltpu.VMEM((tm, d), F32)],
        compiler_params=_cparams(("parallel", "parallel", "arbitrary")),
    )(x1, mod, w1, w2, ln_g, ln_b)


def _rope_tables(n_lat, n_ctx):
    rows = n_lat // GRID_W
    row = jnp.repeat(jnp.arange(rows, dtype=F32), GRID_W)
    col = jnp.tile(jnp.arange(GRID_W, dtype=F32), rows)
    n_freq = HEAD_DIM // 4
    inv = ROPE_THETA ** (-jnp.arange(n_freq, dtype=F32) / n_freq)
    ang = jnp.concatenate([row[:, None] * inv, col[:, None] * inv], axis=-1)
    cos = jnp.concatenate([jnp.cos(ang), jnp.ones((n_ctx, HEAD_DIM // 2), F32)], axis=0)
    sin = jnp.concatenate([jnp.sin(ang), jnp.zeros((n_ctx, HEAD_DIM // 2), F32)], axis=0)
    return cos.T, sin.T


def _block_diag(w):
    k, c = w.shape[-3], w.shape[-2]
    eye = jnp.eye(k, dtype=w.dtype)
    bd = jnp.einsum('...kce,kj->...kcje', w, eye)
    return bd.reshape(*w.shape[:-3], k * c, k * c)


def kernel(x, c, ctx, c_ctx, w_ada, b_ada, w_in, ret_decay_logit, attn_q_gain, attn_k_gain,
           lru_conv_w, lru_conv_b, lru_w_a, lru_b_a, lru_w_x, lru_b_x, lru_lambda,
           w_out, ln1_g, ln1_b, w_ff1, w_ff2, ln2_g, ln2_b):
    b, n_lat, d = x.shape
    n_ctx = ctx.shape[1]
    depth = w_in.shape[0]
    s = n_lat + n_ctx
    rw, aw, kw = d // 4, d // 2, d // 8
    alpha = (2.0 * depth) ** 0.25
    assert n_lat % CHUNK == 0 and n_ctx % CHUNK == 0 and d == 16 * HEAD_DIM

    pad = (-(b + 1)) % 8
    s_in = jnp.concatenate([c, c_ctx[None, :], jnp.zeros((pad, d), F32)], axis=0)
    mods = _ada_call(s_in, w_ada, b_ada)
    mod_lat = mods[:, :b].reshape(depth, b, 1, 6, d)
    mod_ctx = jnp.broadcast_to(mods[:, b].reshape(depth, 1, 1, 6, d), (depth, b, 1, 6, d))
    mod_all = jnp.concatenate([mod_lat, mod_ctx], axis=2)

    o_aq = 4 * rw
    o_lx = o_aq + aw + 2 * kw
    wn = jnp.concatenate([w_in[:, :, :o_aq], w_in[:, :, o_lx:]], axis=2).astype(BF16)
    wt = jnp.swapaxes(w_in[:, :, o_aq:o_lx], 1, 2).astype(BF16)
    wg = jnp.concatenate([_block_diag(lru_w_a[:, 0]), _block_diag(lru_w_x[:, 0]),
                          _block_diag(lru_w_a[:, 1]), _block_diag(lru_w_x[:, 1])], axis=-1).astype(BF16)
    bg = jnp.concatenate([lru_b_a[:, 0], lru_b_x[:, 0], lru_b_a[:, 1], lru_b_x[:, 1]],
                         axis=-1)[:, None, :]
    w_out_b = w_out.astype(BF16)
    w1_b = w_ff1.astype(BF16)
    w2_b = w_ff2.astype(BF16)
    dl_lane = jnp.repeat(ret_decay_logit, HEAD_DIM, axis=-1)
    cos_t, sin_t = _rope_tables(n_lat, n_ctx)

    def ffn_tile(rows):
        return rows // 4 if rows % 32 == 0 else CHUNK

    xa = jnp.concatenate([x, ctx], axis=1)
    for l in range(depth):
        need_ctx = l < depth - 1
        mod = mod_all[l]
        ret_in, g_in, lru_in, qt, k, vt = _inproj_call(
            xa, mod, wn[l], wt[l], cos_t, sin_t,
            attn_q_gain[l][:, None], attn_k_gain[l][:, None], n_lat)
        ret = _ret_call(ret_in, g_in, dl_lane[l], n_lat)
        att = _attn_call(qt, k, vt, n_lat, need_ctx)
        af, df, ab, db = _lru_pre_call(lru_in, lru_conv_w[l], lru_conv_b[l][None, :], wg[l], bg[l],
                                       lru_lambda[l], n_lat)
        shp = (s, b, rw)
        hf, hb = _scan_call(af.reshape(shp), df.reshape(shp), ab.reshape(shp), db.reshape(shp), n_lat)
        n_rows = s if need_ctx else n_lat
        x1 = _outproj_call(xa, ret, att, hf.reshape(s, b * rw), hb.reshape(s, b * rw), lru_in, mod,
                           w_out_b[l], ln1_g[l][None, :], ln1_b[l][None, :], n_lat, n_rows, alpha)
        xa = _ffn_call(x1, mod, w1_b[l], w2_b[l], ln2_g[l][None, :], ln2_b[l][None, :],
                       n_lat, alpha, ffn_tile(n_rows))
    return xa
```

```python
import functools
import math

import jax
import jax.numpy as jnp
from jax import lax
from jax.experimental import pallas as pl
from jax.experimental.pallas import tpu as pltpu

F32 = jnp.float32
BF16 = jnp.bfloat16

HEAD_DIM = 64
GRID_W = 64
ROPE_THETA = 10000.0
LRU_C = 8.0
LRU_BLOCKS = 4
EPS = 1e-6
LOG2E = 1.4426950408889634

ONES_ROWS = 16
CHUNK = 256
SCAN_TILE = 128
VMEM_LIMIT = 56 * 1024 * 1024


def _cparams(sem):
    return pltpu.CompilerParams(dimension_semantics=sem, vmem_limit_bytes=VMEM_LIMIT)


def _log_sigmoid(x):
    return jnp.minimum(x, 0.0) - jnp.log1p(jnp.exp(-jnp.abs(x)))


def _layer_norm(z, g, b):
    mu = jnp.mean(z, axis=-1, keepdims=True)
    zc = z - mu
    var = jnp.mean(zc * zc, axis=-1, keepdims=True)
    return zc * lax.rsqrt(var + EPS) * g + b


def _row_mod(mod_ref, idx, is_ctx):
    return jnp.where(is_ctx, mod_ref[1, idx:idx + 1, :], mod_ref[0, idx:idx + 1, :])


def _ctx_rows(tile_idx, tm, n_lat):
    rows = tile_idx * tm + lax.broadcasted_iota(jnp.int32, (tm, 1), 0)
    return rows >= n_lat


def _ada_kernel(s_ref, w_ref, b_ref, o_ref):
    s = s_ref[...]
    s = s * jax.nn.sigmoid(s)
    o_ref[...] = jnp.dot(s.astype(BF16), w_ref[...].astype(BF16),
                         preferred_element_type=F32) + b_ref[...]


def _ada_call(s_in, w_ada, b_ada):
    depth, d, d6 = w_ada.shape
    rows = s_in.shape[0]
    tn = d6 // 4
    return pl.pallas_call(
        _ada_kernel,
        grid=(depth, d6 // tn),
        in_specs=[
            pl.BlockSpec((rows, d), lambda l, j: (0, 0)),
            pl.BlockSpec((None, d, tn), lambda l, j: (l, 0, j)),
            pl.BlockSpec((None, 1, tn), lambda l, j: (l, 0, j)),
        ],
        out_specs=pl.BlockSpec((None, rows, tn), lambda l, j: (l, 0, j)),
        out_shape=jax.ShapeDtypeStruct((depth, rows, d6), F32),
        compiler_params=_cparams(("parallel", "parallel")),
    )(s_in, w_ada, b_ada.reshape(depth, 1, d6))


def _inproj_kernel(x_ref, mod_ref, wn_ref, wt_ref, cos_ref, sin_ref, qg_ref, kg_ref,
                   ret_ref, g_ref, lru_ref, qt_ref, k_ref, vt_ref, *, n_lat):
    tm = x_ref.shape[0]
    is_ctx = _ctx_rows(pl.program_id(1), tm, n_lat)
    x = x_ref[...]
    u = (x * (1.0 + _row_mod(mod_ref, 1, is_ctx)) + _row_mod(mod_ref, 0, is_ctx)).astype(BF16)

    pn = jnp.dot(u, wn_ref[...], preferred_element_type=F32)
    rw = g_ref.shape[1]
    ret_ref[:, 0:rw] = pn[:, 0:rw].astype(BF16)
    ret_ref[:, rw:2 * rw] = (pn[:, rw:2 * rw] * HEAD_DIM ** -0.5).astype(BF16)
    ret_ref[:, 2 * rw:3 * rw] = pn[:, 2 * rw:3 * rw].astype(BF16)
    g_ref[...] = pn[:, 3 * rw:4 * rw]
    lru_ref[...] = pn[:, 4 * rw:]

    pt = lax.dot_general(wt_ref[...], u, (((1,), (1,)), ((), ())), preferred_element_type=F32)
    cos = cos_ref[...]
    sin = sin_ref[...]
    half = HEAD_DIM // 2

    def norm_rope(t, gain):
        ms = jnp.mean(t * t, axis=0, keepdims=True)
        t = t * lax.rsqrt(ms + EPS) * gain
        x1, x2 = t[:half], t[half:]
        return jnp.concatenate([x1 * cos - x2 * sin, x1 * sin + x2 * cos], axis=0)

    qw = qt_ref.shape[0]
    kw = k_ref.shape[1]
    qscale = HEAD_DIM ** -0.5 * LOG2E
    for h in range(qw // HEAD_DIM):
        r = h * HEAD_DIM
        qt_ref[r:r + HEAD_DIM, :] = (norm_rope(pt[r:r + HEAD_DIM], qg_ref[...]) * qscale).astype(BF16)
    kt = jnp.concatenate(
        [norm_rope(pt[qw + h * HEAD_DIM:qw + (h + 1) * HEAD_DIM], kg_ref[...])
         for h in range(kw // HEAD_DIM)], axis=0)
    k_ref[...] = kt.T.astype(BF16)
    ones = jnp.ones((ONES_ROWS, tm), BF16)
    vrows = HEAD_DIM + ONES_ROWS
    for h in range(kw // HEAD_DIM):
        v0 = qw + kw + h * HEAD_DIM
        vt_ref[h * vrows:h * vrows + HEAD_DIM, :] = pt[v0:v0 + HEAD_DIM].astype(BF16)
        vt_ref[h * vrows + HEAD_DIM:(h + 1) * vrows, :] = ones


def _inproj_call(x_all, mod, wn, wt, cos_t, sin_t, qg, kg, n_lat, tm=256):
    b, s, d = x_all.shape
    rw, qw, kw = d // 4, d // 2, d // 8
    vw = (kw // HEAD_DIM) * (HEAD_DIM + ONES_ROWS)
    grid = (b, s // tm)
    tok = lambda shape_last: pl.BlockSpec((None, tm, shape_last), lambda i, j: (i, j, 0))
    tr = lambda rows: pl.BlockSpec((None, rows, tm), lambda i, j: (i, 0, j))
    const = lambda shape: pl.BlockSpec(shape, lambda i, j: tuple(0 for _ in shape))
    return pl.pallas_call(
        functools.partial(_inproj_kernel, n_lat=n_lat),
        grid=grid,
        in_specs=[
            tok(d),
            pl.BlockSpec((None, 2, 6, d), lambda i, j: (i, 0, 0, 0)),
            const(wn.shape), const(wt.shape),
            pl.BlockSpec((HEAD_DIM // 2, tm), lambda i, j: (0, j)),
            pl.BlockSpec((HEAD_DIM // 2, tm), lambda i, j: (0, j)),
            const(qg.shape), const(kg.shape),
        ],
        out_specs=[tok(3 * rw), tok(rw), tok(2 * rw), tr(qw), tok(kw),
                   pl.BlockSpec((None, None, vw, tm), lambda i, j: (i, j, 0, 0))],
        out_shape=[
            jax.ShapeDtypeStruct((b, s, 3 * rw), BF16),
            jax.ShapeDtypeStruct((b, s, rw), F32),
            jax.ShapeDtypeStruct((b, s, 2 * rw), F32),
            jax.ShapeDtypeStruct((b, qw, s), BF16),
            jax.ShapeDtypeStruct((b, s, kw), BF16),
            jax.ShapeDtypeStruct((b, s // tm, vw, tm), BF16),
        ],
        compiler_params=_cparams(("parallel", "parallel")),
    )(x_all, mod, wn, wt, cos_t, sin_t, qg, kg)


def _ret_kernel(ret_ref, g_ref, dl_ref, o_ref, sf_ref, sb_ref, dm_ref, *, n_lat_chunks):
    c_len = CHUNK
    s_len, rw = g_ref.shape
    nc = s_len // c_len
    ncl = n_lat_chunks
    n_heads = rw // HEAD_DIM

    lg = _log_sigmoid(dl_ref[...])
    lgf, lgb = lg[0:1], lg[1:2]
    pos = lax.broadcasted_iota(jnp.int32, (c_len, 1), 0).astype(F32)
    kwf = jnp.exp(lgf * (c_len - 1.0 - pos))
    kwb = jnp.exp(lgb * pos)
    qwf = jnp.exp(lgf * (pos + 1.0))
    qwb = jnp.exp(lgb * (c_len - pos))
    cdf = jnp.exp(lgf * c_len)
    cdb = jnp.exp(lgb * c_len)
    row_head = lax.broadcasted_iota(jnp.int32, (rw, rw), 0) // HEAD_DIM
    col_head = lax.broadcasted_iota(jnp.int32, (rw, rw), 1) // HEAD_DIM
    same_head = row_head == col_head
    lane_head = lax.broadcasted_iota(jnp.int32, (1, rw), 1) // HEAD_DIM

    ii = lax.broadcasted_iota(jnp.int32, (c_len, c_len), 0)
    jj = lax.broadcasted_iota(jnp.int32, (c_len, c_len), 1)
    dij = (ii - jj).astype(F32)
    for h in range(n_heads):
        lf = lgf[:, h * HEAD_DIM:h * HEAD_DIM + 1]
        lb = lgb[:, h * HEAD_DIM:h * HEAD_DIM + 1]
        dm_ref[h] = jnp.exp(jnp.where(dij >= 0.0, lf * dij, -lb * dij))

    def contrib(c, carry):
        r0 = pl.multiple_of(c * c_len, c_len)
        k = ret_ref[pl.ds(r0, c_len), rw:2 * rw].astype(F32)
        v = ret_ref[pl.ds(r0, c_len), 2 * rw:3 * rw]
        tn = (((0,), (0,)), ((), ()))
        cf = lax.dot_general((k * kwf).astype(BF16), v, tn, preferred_element_type=F32)
        cb = lax.dot_general((k * kwb).astype(BF16), v, tn, preferred_element_type=F32)
        sf_ref[c] = jnp.where(same_head, cf, 0.0)
        sb_ref[c] = jnp.where(same_head, cb, 0.0)
        return carry

    lax.fori_loop(0, nc, contrib, 0)

    def chain(ref, decay, order):
        state = jnp.zeros((rw, rw), F32)
        for c in order:
            u = ref[c]
            ref[c] = state
            state = decay * state + u

    chain(sf_ref, cdf, list(range(ncl, nc)) + list(range(ncl)))
    chain(sb_ref, cdb, list(range(nc - 1, ncl - 1, -1)) + list(range(ncl - 1, -1, -1)))

    ones_blk = jnp.where(same_head, 1.0, 0.0).astype(BF16)

    def outputs(c, carry):
        r0 = pl.multiple_of(c * c_len, c_len)
        q = ret_ref[pl.ds(r0, c_len), 0:rw]
        k = ret_ref[pl.ds(r0, c_len), rw:2 * rw]
        v = ret_ref[pl.ds(r0, c_len), 2 * rw:3 * rw]
        qf = q.astype(F32)
        qi = jnp.concatenate([(qf * qwf).astype(BF16), (qf * qwb).astype(BF16)], axis=1)
        st = jnp.concatenate([sf_ref[c], sb_ref[c]], axis=0).astype(BF16)
        o = jnp.dot(qi, st, preferred_element_type=F32)
        for h in range(n_heads):
            hm = lane_head == h
            qh = jnp.where(hm, q, jnp.zeros_like(q))
            sc = lax.dot_general(qh, k, (((1,), (1,)), ((), ())), preferred_element_type=F32)
            p = (sc * dm_ref[h]).astype(BF16)
            o = o + jnp.where(hm, jnp.dot(p, v, preferred_element_type=F32), 0.0)
        o2 = o * o
        hi = o2.astype(BF16)
        lo = (o2 - hi.astype(F32)).astype(BF16)
        ms = (jnp.dot(hi, ones_blk, preferred_element_type=F32)
              + jnp.dot(lo, ones_blk, preferred_element_type=F32)) * (1.0 / HEAD_DIM)
        g = g_ref[pl.ds(r0, c_len), :]
        o_ref[pl.ds(r0, c_len), :] = (o * lax.rsqrt(ms + EPS) * (g * jax.nn.sigmoid(g))).astype(BF16)
        return carry

    lax.fori_loop(0, nc, outputs, 0)


def _ret_call(ret_in, g_in, dl_lane, n_lat):
    b, s, rw = g_in.shape
    nc = s // CHUNK
    return pl.pallas_call(
        functools.partial(_ret_kernel, n_lat_chunks=n_lat // CHUNK),
        grid=(b,),
        in_specs=[
            pl.BlockSpec((None, s, 3 * rw), lambda i: (i, 0, 0)),
            pl.BlockSpec((None, s, rw), lambda i: (i, 0, 0)),
            pl.BlockSpec((2, rw), lambda i: (0, 0)),
        ],
        out_specs=pl.BlockSpec((None, s, rw), lambda i: (i, 0, 0)),
        out_shape=jax.ShapeDtypeStruct((b, s, rw), BF16),
        scratch_shapes=[
            pltpu.VMEM((nc, rw, rw), F32),
            pltpu.VMEM((nc, rw, rw), F32),
            pltpu.VMEM((rw // HEAD_DIM, CHUNK, CHUNK), F32),
        ],
        compiler_params=_cparams(("parallel",)),
    )(ret_in, g_in, dl_lane)


def _attn_kernel(qt_ref, k_ref, vt_ref, *rest, group):
    o_ref, rhs_ref, sa_ref, sb_ref, m_ref, acc_ref, out_ref = rest[-7:]
    n_q_heads = qt_ref.shape[0] // HEAD_DIM
    n_chunks = vt_ref.shape[0]
    kc_len = vt_ref.shape[2]
    vrows = vt_ref.shape[1] // (n_q_heads // group)

    for h in range(n_q_heads):
        qt = qt_ref[h * HEAD_DIM:(h + 1) * HEAD_DIM, :]
        zero = jnp.zeros_like(qt)
        rhs_ref[h] = jnp.concatenate([qt, zero] if h // group == 0 else [zero, qt], axis=0)
    m_ref[...] = jnp.full(m_ref.shape, -jnp.inf, F32)
    acc_ref[...] = jnp.zeros(acc_ref.shape, F32)

    def scores(c, h):
        c0 = c * kc_len
        if not isinstance(c0, int):
            c0 = pl.multiple_of(c0, kc_len)
        return jnp.dot(k_ref[pl.ds(c0, kc_len), :], rhs_ref[h], preferred_element_type=F32)

    def step(c, cur_ref, nxt_ref):
        vtc = vt_ref[c]
        for h in range(n_q_heads):
            if nxt_ref is not None:
                nxt_ref[h] = scores(c + 1, h)
            s = cur_ref[h]
            kv = h // group
            m_old = m_ref[h:h + 1, :]
            m_new = jnp.maximum(m_old, jnp.max(s, axis=0, keepdims=True))
            alpha = jnp.exp2(m_old - m_new)
            m_ref[h:h + 1, :] = m_new
            p = jnp.exp2(s - m_new).astype(BF16)
            pv = jnp.dot(vtc[kv * vrows:(kv + 1) * vrows, :], p, preferred_element_type=F32)
            acc_ref[h] = alpha * acc_ref[h] + pv

    for h in range(n_q_heads):
        sa_ref[h] = scores(0, h)

    def pair(i, carry):
        step(2 * i, sa_ref, sb_ref)
        step(2 * i + 1, sb_ref, sa_ref)
        return carry

    n_pairs = (n_chunks - 1) // 2
    lax.fori_loop(0, n_pairs, pair, 0)
    if (n_chunks - 1) % 2 == 1:
        step(n_chunks - 2, sa_ref, sb_ref)
        step(n_chunks - 1, sb_ref, None)
    else:
        step(n_chunks - 1, sa_ref, None)

    for h in range(n_q_heads):
        a = acc_ref[h]
        out_ref[h * HEAD_DIM:(h + 1) * HEAD_DIM, :] = a[:HEAD_DIM] / a[HEAD_DIM:HEAD_DIM + 1]
    o_ref[...] = out_ref[...].T.astype(BF16)


def _attn_call(qt, k, vt, n_lat, with_ctx, tq=256):
    b, qw, s = qt.shape
    kw = k.shape[2]
    kc_len = vt.shape[3]
    n_ctx = s - n_lat
    n_heads = qw // HEAD_DIM
    group = n_heads // (kw // HEAD_DIM)
    assert kw == 2 * HEAD_DIM and n_lat % n_ctx == 0 and n_ctx % tq == 0 and n_ctx % kc_len == 0
    body = functools.partial(_attn_kernel, group=group)
    vw = vt.shape[2]
    scratch = [pltpu.VMEM((n_heads, kw, tq), BF16),
               pltpu.VMEM((n_heads, kc_len, tq), F32), pltpu.VMEM((n_heads, kc_len, tq), F32),
               pltpu.VMEM((n_heads, tq), F32), pltpu.VMEM((n_heads, vw // (kw // HEAD_DIM), tq), F32),
               pltpu.VMEM((qw, tq), F32)]
    out_sds = jax.ShapeDtypeStruct((b, s, qw), BF16)
    att = pl.pallas_call(
        body,
        grid=(b, n_lat // tq),
        in_specs=[
            pl.BlockSpec((None, qw, tq), lambda i, j: (i, 0, j)),
            pl.BlockSpec((None, s, kw), lambda i, j: (i, 0, 0)),
            pl.BlockSpec((None, s // kc_len, vw, kc_len), lambda i, j: (i, 0, 0, 0)),
        ],
        out_specs=pl.BlockSpec((None, tq, qw), lambda i, j: (i, j, 0)),
        out_shape=out_sds,
        scratch_shapes=scratch,
        compiler_params=_cparams(("parallel", "arbitrary")),
    )(qt, k, vt)
    if not with_ctx:
        return att
    lat_tiles, lat_ctx = n_lat // tq, n_lat // n_ctx
    return pl.pallas_call(
        body,
        grid=(b, n_ctx // tq),
        in_specs=[
            pl.BlockSpec((None, qw, tq), lambda i, j: (i, 0, lat_tiles + j)),
            pl.BlockSpec((None, n_ctx, kw), lambda i, j: (i, lat_ctx, 0)),
            pl.BlockSpec((None, n_ctx // kc_len, vw, kc_len), lambda i, j: (i, lat_ctx, 0, 0)),
            pl.BlockSpec(memory_space=pl.ANY),
        ],
        out_specs=pl.BlockSpec((None, tq, qw), lambda i, j: (i, lat_tiles + j, 0)),
        out_shape=out_sds,
        input_output_aliases={3: 0},
        scratch_shapes=scratch,
        compiler_params=_cparams(("parallel", "arbitrary")),
    )(qt, k, vt, att)


def _neg_expm1(y, a):
    series = -y * (1.0 + y * (1.0 / 2) * (1.0 + y * (1.0 / 3) * (1.0 + y * (1.0 / 4) * (
        1.0 + y * (1.0 / 5) * (1.0 + y * (1.0 / 6) * (1.0 + y * (1.0 / 7)))))))
    return jnp.where(y > -0.125, series, 1.0 - a * a)


def _lru_pre_kernel(cur_ref, prev_ref, next_ref, cw_ref, cb_ref, wg_ref, bg_ref, lam_ref,
                    af_ref, df_ref, ab_ref, db_ref, *, n_lat_chunks, n_chunks):
    c = pl.program_id(1)
    c_len, w = cur_ref.shape
    first = jnp.logical_or(c == 0, c == n_lat_chunks)
    last = jnp.logical_or(c == n_lat_chunks - 1, c == n_chunks - 1)
    cur = cur_ref[...]
    prev = jnp.where(first, 0.0, prev_ref[...])
    nxt = jnp.where(last, 0.0, next_ref[...])
    win = jnp.concatenate([prev, cur, nxt], axis=0)
    n_win = c_len + 16

    def shifted(off):
        return pltpu.roll(win, (-off) % n_win, axis=0)[8:8 + c_len]

    xr = (shifted(-2) * cw_ref[0:1, :] + shifted(-1) * cw_ref[1:2, :] + cur * cw_ref[2:3, :]
          + shifted(1) * cw_ref[3:4, :] + cb_ref[...])
    gates = jnp.dot(xr.astype(BF16), wg_ref[...], preferred_element_type=F32) + bg_ref[...]
    log_lam = _log_sigmoid(lam_ref[...])
    for d, (a_ref, d_ref) in enumerate(((af_ref, df_ref), (ab_ref, db_ref))):
        r = jax.nn.sigmoid(gates[:, 2 * d * w:(2 * d + 1) * w])
        i = jax.nn.sigmoid(gates[:, (2 * d + 1) * w:(2 * d + 2) * w])
        log_a = LRU_C * r * log_lam[d:d + 1, :]
        a = jnp.exp(log_a)
        a_ref[...] = a
        d_ref[...] = jnp.sqrt(_neg_expm1(2.0 * log_a, a)) * (i * xr)


def _lru_pre_call(lru_in, conv_w, conv_b, wg, bg, lam, n_lat):
    b, s, w2 = lru_in.shape
    w = w2 // 2
    nc = s // CHUNK
    hb = CHUNK // 8
    const = lambda shape: pl.BlockSpec(shape, lambda i, j: tuple(0 for _ in shape))
    out_spec = pl.BlockSpec((CHUNK, w), lambda i, j: (j, i))
    out_sds = jax.ShapeDtypeStruct((s, b * w), F32)
    return pl.pallas_call(
        functools.partial(_lru_pre_kernel, n_lat_chunks=n_lat // CHUNK, n_chunks=nc),
        grid=(b, nc),
        in_specs=[
            pl.BlockSpec((None, CHUNK, w), lambda i, j: (i, j, 0)),
            pl.BlockSpec((None, 8, w), lambda i, j: (i, jnp.maximum(j * hb - 1, 0), 0)),
            pl.BlockSpec((None, 8, w), lambda i, j: (i, jnp.minimum((j + 1) * hb, s // 8 - 1), 0)),
            const(conv_w.shape), const(conv_b.shape), const(wg.shape), const(bg.shape),
            const(lam.shape),
        ],
        out_specs=[out_spec] * 4,
        out_shape=[out_sds] * 4,
        compiler_params=_cparams(("parallel", "parallel")),
    )(lru_in, lru_in, lru_in, conv_w, conv_b, wg, bg, lam)


def _scan_kernel(af_ref, df_ref, ab_ref, db_ref, hf_ref, hb_ref, sf_ref, sb_ref):
    tt = af_ref.shape[0]

    @pl.when(pl.program_id(0) == 0)
    def _():
        sf_ref[...] = jnp.zeros_like(sf_ref)
        sb_ref[...] = jnp.zeros_like(sb_ref)

    def step(t, carry):
        hf, hb = carry
        hf = af_ref[t] * hf + df_ref[t]
        hf_ref[t] = hf
        tb = tt - 1 - t
        hb = ab_ref[tb] * hb + db_ref[tb]
        hb_ref[tb] = hb
        return hf, hb

    hf, hb = lax.fori_loop(0, tt, step, (sf_ref[...], sb_ref[...]), unroll=8)
    sf_ref[...] = hf
    sb_ref[...] = hb


def _scan_call(af, df, ab, db, n_lat):
    s, b, w = af.shape
    nt = s // SCAN_TILE
    nlt = n_lat // SCAN_TILE
    fwd = pl.BlockSpec((SCAN_TILE, b, w), lambda i: ((i + nlt) % nt, 0, 0))
    bwd = pl.BlockSpec((SCAN_TILE, b, w), lambda i: (nt - 1 - i, 0, 0))
    sds = jax.ShapeDtypeStruct((s, b, w), F32)
    return pl.pallas_call(
        _scan_kernel,
        grid=(nt,),
        in_specs=[fwd, fwd, bwd, bwd],
        out_specs=[fwd, bwd],
        out_shape=[sds, sds],
        scratch_shapes=[pltpu.VMEM((b, w), F32), pltpu.VMEM((b, w), F32)],
        compiler_params=_cparams(("arbitrary",)),
    )(af, df, ab, db)


def _outproj_kernel(x_ref, ret_ref, att_ref, hf_ref, hb_ref, lg_ref, mod_ref, w_ref, g_ref, b_ref,
                    o_ref, *, n_lat, alpha):
    tm = x_ref.shape[0]
    is_ctx = _ctx_rows(pl.program_id(1), tm, n_lat)
    rw = ret_ref.shape[1]
    aw = att_ref.shape[1]
    lru = ((hf_ref[...] + hb_ref[...]) * jax.nn.gelu(lg_ref[...])).astype(BF16)
    y = jnp.dot(ret_ref[...], w_ref[0:rw, :], preferred_element_type=F32)
    y = y + jnp.dot(att_ref[...], w_ref[rw:rw + aw, :], preferred_element_type=F32)
    y = y + jnp.dot(lru, w_ref[rw + aw:, :], preferred_element_type=F32)
    z = alpha * x_ref[...] + _row_mod(mod_ref, 2, is_ctx) * y
    o_ref[...] = _layer_norm(z, g_ref[...], b_ref[...])


def _outproj_call(x_all, ret, att, hf, hb, lru_in, mod, w_out, ln_g, ln_b, n_lat, n_rows, alpha, tm=256):
    b, _, d = x_all.shape
    rw, aw = ret.shape[2], att.shape[2]
    tok = lambda last: pl.BlockSpec((None, tm, last), lambda i, j: (i, j, 0))
    col = pl.BlockSpec((tm, rw), lambda i, j: (j, i))
    const = lambda shape: pl.BlockSpec(shape, lambda i, j: tuple(0 for _ in shape))
    return pl.pallas_call(
        functools.partial(_outproj_kernel, n_lat=n_lat, alpha=alpha),
        grid=(b, n_rows // tm),
        in_specs=[
            tok(d), tok(rw), tok(aw), col, col,
            pl.BlockSpec((None, tm, rw), lambda i, j: (i, j, 1)),
            pl.BlockSpec((None, 2, 6, d), lambda i, j: (i, 0, 0, 0)),
            const(w_out.shape), const(ln_g.shape), const(ln_b.shape),
        ],
        out_specs=tok(d),
        out_shape=jax.ShapeDtypeStruct((b, n_rows, d), F32),
        compiler_params=_cparams(("parallel", "parallel")),
    )(x_all, ret, att, hf, hb, lru_in, mod, w_out, ln_g, ln_b)


def _ffn_kernel(x_ref, mod_ref, w1_ref, w2_ref, g_ref, b_ref, o_ref, u_ref, acc_ref, *, n_lat, alpha):
    tm = x_ref.shape[0]
    kf = pl.program_id(2)
    is_ctx = _ctx_rows(pl.program_id(1), tm, n_lat)

    @pl.when(kf == 0)
    def _():
        u_ref[...] = (x_ref[...] * (1.0 + _row_mod(mod_ref, 4, is_ctx))
                      + _row_mod(mod_ref, 3, is_ctx)).astype(BF16)
        acc_ref[...] = jnp.zeros_like(acc_ref)

    h = jnp.maximum(jnp.dot(u_ref[...], w1_ref[...], preferred_element_type=F32), 0.0)
    acc_ref[...] += jnp.dot((h * h).astype(BF16), w2_ref[...], preferred_element_type=F32)

    @pl.when(kf == pl.num_programs(2) - 1)
    def _():
        z = alpha * x_ref[...] + _row_mod(mod_ref, 5, is_ctx) * acc_ref[...]
        o_ref[...] = _layer_norm(z, g_ref[...], b_ref[...])


def _ffn_call(x1, mod, w1, w2, ln_g, ln_b, n_lat, alpha, tm, tf=1024):
    b, s, d = x1.shape
    f = w1.shape[1]
    tok = pl.BlockSpec((None, tm, d), lambda i, j, k: (i, j, 0))
    const = lambda shape: pl.BlockSpec(shape, lambda i, j, k: tuple(0 for _ in shape))
    return pl.pallas_call(
        functools.partial(_ffn_kernel, n_lat=n_lat, alpha=alpha),
        grid=(b, s // tm, f // tf),
        in_specs=[
            tok,
            pl.BlockSpec((None, 2, 6, d), lambda i, j, k: (i, 0, 0, 0)),
            pl.BlockSpec((d, tf), lambda i, j, k: (0, k)),
            pl.BlockSpec((tf, d), lambda i, j, k: (k, 0)),
            const(ln_g.shape), const(ln_b.shape),
        ],
        out_specs=tok,
        out_shape=jax.ShapeDtypeStruct((b, s, d), F32),
        scratch_shapes=[pltpu.VMEM((tm, d), BF16), pltpu.VMEM((tm, d), F32)],
        compiler_params=_cparams(("parallel", "parallel", "arbitrary")),
    )(x1, mod, w1, w2, ln_g, ln_b)


def _rope_tables(n_lat, n_ctx):
    rows = n_lat // GRID_W
    row = jnp.repeat(jnp.arange(rows, dtype=F32), GRID_W)
    col = jnp.tile(jnp.arange(GRID_W, dtype=F32), rows)
    n_freq = HEAD_DIM // 4
    inv = ROPE_THETA ** (-jnp.arange(n_freq, dtype=F32) / n_freq)
    ang = jnp.concatenate([row[:, None] * inv, col[:, None] * inv], axis=-1)
    cos = jnp.concatenate([jnp.cos(ang), jnp.ones((n_ctx, HEAD_DIM // 2), F32)], axis=0)
    sin = jnp.concatenate([jnp.sin(ang), jnp.zeros((n_ctx, HEAD_DIM // 2), F32)], axis=0)
    return cos.T, sin.T


def _block_diag(w):
    k, c = w.shape[-3], w.shape[-2]
    eye = jnp.eye(k, dtype=w.dtype)
    bd = jnp.einsum('...kce,kj->...kcje', w, eye)
    return bd.reshape(*w.shape[:-3], k * c, k * c)


def kernel(x, c, ctx, c_ctx, w_ada, b_ada, w_in, ret_decay_logit, attn_q_gain, attn_k_gain,
           lru_conv_w, lru_conv_b, lru_w_a, lru_b_a, lru_w_x, lru_b_x, lru_lambda,
           w_out, ln1_g, ln1_b, w_ff1, w_ff2, ln2_g, ln2_b):
    b, n_lat, d = x.shape
    n_ctx = ctx.shape[1]
    depth = w_in.shape[0]
    s = n_lat + n_ctx
    rw, aw, kw = d // 4, d // 2, d // 8
    alpha = (2.0 * depth) ** 0.25
    assert n_lat % CHUNK == 0 and n_ctx % CHUNK == 0 and d == 16 * HEAD_DIM

    pad = (-(b + 1)) % 8
    s_in = jnp.concatenate([c, c_ctx[None, :], jnp.zeros((pad, d), F32)], axis=0)
    mods = _ada_call(s_in, w_ada, b_ada)
    mod_lat = mods[:, :b].reshape(depth, b, 1, 6, d)
    mod_ctx = jnp.broadcast_to(mods[:, b].reshape(depth, 1, 1, 6, d), (depth, b, 1, 6, d))
    mod_all = jnp.concatenate([mod_lat, mod_ctx], axis=2)

    o_aq = 4 * rw
    o_lx = o_aq + aw + 2 * kw
    wn = jnp.concatenate([w_in[:, :, :o_aq], w_in[:, :, o_lx:]], axis=2).astype(BF16)
    wt = jnp.swapaxes(w_in[:, :, o_aq:o_lx], 1, 2).astype(BF16)
    wg = jnp.concatenate([_block_diag(lru_w_a[:, 0]), _block_diag(lru_w_x[:, 0]),
                          _block_diag(lru_w_a[:, 1]), _block_diag(lru_w_x[:, 1])], axis=-1).astype(BF16)
    bg = jnp.concatenate([lru_b_a[:, 0], lru_b_x[:, 0], lru_b_a[:, 1], lru_b_x[:, 1]],
                         axis=-1)[:, None, :]
    w_out_b = w_out.astype(BF16)
    w1_b = w_ff1.astype(BF16)
    w2_b = w_ff2.astype(BF16)
    dl_lane = jnp.repeat(ret_decay_logit, HEAD_DIM, axis=-1)
    cos_t, sin_t = _rope_tables(n_lat, n_ctx)

    def ffn_tile(rows):
        return rows // 4 if rows % 32 == 0 else CHUNK

    xa = jnp.concatenate([x, ctx], axis=1)
    for l in range(depth):
        need_ctx = l < depth - 1
        mod = mod_all[l]
        ret_in, g_in, lru_in, qt, k, vt = _inproj_call(
            xa, mod, wn[l], wt[l], cos_t, sin_t,
            attn_q_gain[l][:, None], attn_k_gain[l][:, None], n_lat)
        ret = _ret_call(ret_in, g_in, dl_lane[l], n_lat)
        att = _attn_call(qt, k, vt, n_lat, need_ctx)
        af, df, ab, db = _lru_pre_call(lru_in, lru_conv_w[l], lru_conv_b[l][None, :], wg[l], bg[l],
                                       lru_lambda[l], n_lat)
        shp = (s, b, rw)
        hf, hb = _scan_call(af.reshape(shp), df.reshape(shp), ab.reshape(shp), db.reshape(shp), n_lat)
        n_rows = s if need_ctx else n_lat
        x1 = _outproj_call(xa, ret, att, hf.reshape(s, b * rw), hb.reshape(s, b * rw), lru_in, mod,
                           w_out_b[l], ln1_g[l][None, :], ln1_b[l][None, :], n_lat, n_rows, alpha)
        xa = _ffn_call(x1, mod, w1_b[l], w2_b[l], ln2_g[l][None, :], ln2_b[l][None, :],
                       n_lat, alpha, ffn_tile(n_rows))
    return xa
```

```python
import functools
import math

import jax
import jax.numpy as jnp
from jax import lax
from jax.experimental import pallas as pl
from jax.experimental.pallas import tpu as pltpu

F32 = jnp.float32
BF16 = jnp.bfloat16

HEAD_DIM = 64
GRID_W = 64
ROPE_THETA = 10000.0
LRU_C = 8.0
LRU_BLOCKS = 4
EPS = 1e-6
LOG2E = 1.4426950408889634

ONES_ROWS = 16
SCORE_BOUND_MARGIN = 1.0 + 2.0 ** -7
SCORE_BOUND_CAP = 60.0
PV_LAG = 2
CHUNK = 256
SCAN_TILE = 128
VMEM_LIMIT = 56 * 1024 * 1024


def _cparams(sem):
    return pltpu.CompilerParams(dimension_semantics=sem, vmem_limit_bytes=VMEM_LIMIT)


def _log_sigmoid(x):
    return jnp.minimum(x, 0.0) - jnp.log1p(jnp.exp(-jnp.abs(x)))


def _layer_norm(z, g, b):
    mu = jnp.mean(z, axis=-1, keepdims=True)
    zc = z - mu
    var = jnp.mean(zc * zc, axis=-1, keepdims=True)
    return zc * lax.rsqrt(var + EPS) * g + b


def _row_mod(mod_ref, idx, is_ctx):
    return jnp.where(is_ctx, mod_ref[1, idx:idx + 1, :], mod_ref[0, idx:idx + 1, :])


def _ctx_rows(tile_idx, tm, n_lat):
    rows = tile_idx * tm + lax.broadcasted_iota(jnp.int32, (tm, 1), 0)
    return rows >= n_lat


def _ada_kernel(s_ref, w_ref, b_ref, o_ref):
    s = s_ref[...]
    s = s * jax.nn.sigmoid(s)
    o_ref[...] = jnp.dot(s.astype(BF16), w_ref[...].astype(BF16),
                         preferred_element_type=F32) + b_ref[...]


def _ada_call(s_in, w_ada, b_ada):
    depth, d, d6 = w_ada.shape
    rows = s_in.shape[0]
    tn = d6 // 4
    return pl.pallas_call(
        _ada_kernel,
        grid=(depth, d6 // tn),
        in_specs=[
            pl.BlockSpec((rows, d), lambda l, j: (0, 0)),
            pl.BlockSpec((None, d, tn), lambda l, j: (l, 0, j)),
            pl.BlockSpec((None, 1, tn), lambda l, j: (l, 0, j)),
        ],
        out_specs=pl.BlockSpec((None, rows, tn), lambda l, j: (l, 0, j)),
        out_shape=jax.ShapeDtypeStruct((depth, rows, d6), F32),
        compiler_params=_cparams(("parallel", "parallel")),
    )(s_in, w_ada, b_ada.reshape(depth, 1, d6))


def _inproj_kernel(x_ref, mod_ref, wn_ref, wt_ref, cos_ref, sin_ref, qg_ref, kg_ref,
                   ret_ref, g_ref, lru_ref, qt_ref, k_ref, vt_ref, kn_ref, *, n_lat):
    tm = x_ref.shape[0]
    is_ctx = _ctx_rows(pl.program_id(1), tm, n_lat)
    x = x_ref[...]
    u = (x * (1.0 + _row_mod(mod_ref, 1, is_ctx)) + _row_mod(mod_ref, 0, is_ctx)).astype(BF16)

    pn = jnp.dot(u, wn_ref[...], preferred_element_type=F32)
    rw = g_ref.shape[1]
    ret_ref[:, 0:rw] = pn[:, 0:rw].astype(BF16)
    ret_ref[:, rw:2 * rw] = (pn[:, rw:2 * rw] * HEAD_DIM ** -0.5).astype(BF16)
    ret_ref[:, 2 * rw:3 * rw] = pn[:, 2 * rw:3 * rw].astype(BF16)
    g_ref[...] = pn[:, 3 * rw:4 * rw]
    lru_ref[...] = pn[:, 4 * rw:]

    pt = lax.dot_general(wt_ref[...], u, (((1,), (1,)), ((), ())), preferred_element_type=F32)
    cos = cos_ref[...]
    sin = sin_ref[...]
    half = HEAD_DIM // 2

    def norm_rope(t, gain):
        ms = jnp.mean(t * t, axis=0, keepdims=True)
        t = t * lax.rsqrt(ms + EPS) * gain
        x1, x2 = t[:half], t[half:]
        return jnp.concatenate([x1 * cos - x2 * sin, x1 * sin + x2 * cos], axis=0)

    qw = qt_ref.shape[0]
    kw = k_ref.shape[1]
    qscale = HEAD_DIM ** -0.5 * LOG2E
    for h in range(qw // HEAD_DIM):
        r = h * HEAD_DIM
        qt_ref[r:r + HEAD_DIM, :] = (norm_rope(pt[r:r + HEAD_DIM], qg_ref[...]) * qscale).astype(BF16)
    kt = jnp.concatenate(
        [norm_rope(pt[qw + h * HEAD_DIM:qw + (h + 1) * HEAD_DIM], kg_ref[...])
         for h in range(kw // HEAD_DIM)], axis=0)
    k_ref[...] = kt.T.astype(BF16)
    kf = kt.astype(BF16).astype(F32)
    for h in range(kw // HEAD_DIM):
        kh = kf[h * HEAD_DIM:(h + 1) * HEAD_DIM]
        kn_ref[h:h + 1, :] = jnp.sum(kh * kh, axis=0, keepdims=True)
    ones = jnp.ones((ONES_ROWS, tm), BF16)
    vrows = HEAD_DIM + ONES_ROWS
    for h in range(kw // HEAD_DIM):
        v0 = qw + kw + h * HEAD_DIM
        vt_ref[h * vrows:h * vrows + HEAD_DIM, :] = pt[v0:v0 + HEAD_DIM].astype(BF16)
        vt_ref[h * vrows + HEAD_DIM:(h + 1) * vrows, :] = ones


def _inproj_call(x_all, mod, wn, wt, cos_t, sin_t, qg, kg, n_lat, tm=256):
    b, s, d = x_all.shape
    rw, qw, kw = d // 4, d // 2, d // 8
    vw = (kw // HEAD_DIM) * (HEAD_DIM + ONES_ROWS)
    grid = (b, s // tm)
    tok = lambda shape_last: pl.BlockSpec((None, tm, shape_last), lambda i, j: (i, j, 0))
    tr = lambda rows: pl.BlockSpec((None, rows, tm), lambda i, j: (i, 0, j))
    const = lambda shape: pl.BlockSpec(shape, lambda i, j: tuple(0 for _ in shape))
    return pl.pallas_call(
        functools.partial(_inproj_kernel, n_lat=n_lat),
        grid=grid,
        in_specs=[
            tok(d),
            pl.BlockSpec((None, 2, 6, d), lambda i, j: (i, 0, 0, 0)),
            const(wn.shape), const(wt.shape),
            pl.BlockSpec((HEAD_DIM // 2, tm), lambda i, j: (0, j)),
            pl.BlockSpec((HEAD_DIM // 2, tm), lambda i, j: (0, j)),
            const(qg.shape), const(kg.shape),
        ],
        out_specs=[tok(3 * rw), tok(rw), tok(2 * rw), tr(qw), tok(kw),
                   pl.BlockSpec((None, None, vw, tm), lambda i, j: (i, j, 0, 0)),
                   tr(kw // HEAD_DIM)],
        out_shape=[
            jax.ShapeDtypeStruct((b, s, 3 * rw), BF16),
            jax.ShapeDtypeStruct((b, s, rw), F32),
            jax.ShapeDtypeStruct((b, s, 2 * rw), F32),
            jax.ShapeDtypeStruct((b, qw, s), BF16),
            jax.ShapeDtypeStruct((b, s, kw), BF16),
            jax.ShapeDtypeStruct((b, s // tm, vw, tm), BF16),
            jax.ShapeDtypeStruct((b, kw // HEAD_DIM, s), F32),
        ],
        compiler_params=_cparams(("parallel", "parallel")),
    )(x_all, mod, wn, wt, cos_t, sin_t, qg, kg)


def _ret_kernel(ret_ref, g_ref, dl_ref, o_ref, sf_ref, sb_ref, dm_ref, *, n_lat_chunks):
    c_len = CHUNK
    s_len, rw = g_ref.shape
    nc = s_len // c_len
    ncl = n_lat_chunks
    n_heads = rw // HEAD_DIM

    lg = _log_sigmoid(dl_ref[...])
    lgf, lgb = lg[0:1], lg[1:2]
    pos = lax.broadcasted_iota(jnp.int32, (c_len, 1), 0).astype(F32)
    kwf = jnp.exp(lgf * (c_len - 1.0 - pos))
    kwb = jnp.exp(lgb * pos)
    qwf = jnp.exp(lgf * (pos + 1.0))
    qwb = jnp.exp(lgb * (c_len - pos))
    cdf = jnp.exp(lgf * c_len)
    cdb = jnp.exp(lgb * c_len)
    row_head = lax.broadcasted_iota(jnp.int32, (rw, rw), 0) // HEAD_DIM
    col_head = lax.broadcasted_iota(jnp.int32, (rw, rw), 1) // HEAD_DIM
    same_head = row_head == col_head
    lane_head = lax.broadcasted_iota(jnp.int32, (1, rw), 1) // HEAD_DIM

    ii = lax.broadcasted_iota(jnp.int32, (c_len, c_len), 0)
    jj = lax.broadcasted_iota(jnp.int32, (c_len, c_len), 1)
    dij = (ii - jj).astype(F32)
    for h in range(n_heads):
        lf = lgf[:, h * HEAD_DIM:h * HEAD_DIM + 1]
        lb = lgb[:, h * HEAD_DIM:h * HEAD_DIM + 1]
        dm_ref[h] = jnp.exp(jnp.where(dij >= 0.0, lf * dij, -lb * dij))

    def contrib(c, carry):
        r0 = pl.multiple_of(c * c_len, c_len)
        k = ret_ref[pl.ds(r0, c_len), rw:2 * rw].astype(F32)
        v = ret_ref[pl.ds(r0, c_len), 2 * rw:3 * rw]
        tn = (((0,), (0,)), ((), ()))
        cf = lax.dot_general((k * kwf).astype(BF16), v, tn, preferred_element_type=F32)
        cb = lax.dot_general((k * kwb).astype(BF16), v, tn, preferred_element_type=F32)
        sf_ref[c] = jnp.where(same_head, cf, 0.0)
        sb_ref[c] = jnp.where(same_head, cb, 0.0)
        return carry

    lax.fori_loop(0, nc, contrib, 0)

    def chain(ref, decay, order):
        state = jnp.zeros((rw, rw), F32)
        for c in order:
            u = ref[c]
            ref[c] = state
            state = decay * state + u

    chain(sf_ref, cdf, list(range(ncl, nc)) + list(range(ncl)))
    chain(sb_ref, cdb, list(range(nc - 1, ncl - 1, -1)) + list(range(ncl - 1, -1, -1)))

    ones_blk = jnp.where(same_head, 1.0, 0.0).astype(BF16)

    def outputs(c, carry):
        r0 = pl.multiple_of(c * c_len, c_len)
        q = ret_ref[pl.ds(r0, c_len), 0:rw]
        k = ret_ref[pl.ds(r0, c_len), rw:2 * rw]
        v = ret_ref[pl.ds(r0, c_len), 2 * rw:3 * rw]
        qf = q.astype(F32)
        qi = jnp.concatenate([(qf * qwf).astype(BF16), (qf * qwb).astype(BF16)], axis=1)
        st = jnp.concatenate([sf_ref[c], sb_ref[c]], axis=0).astype(BF16)
        o = jnp.dot(qi, st, preferred_element_type=F32)
        for h in range(n_heads):
            hm = lane_head == h
            qh = jnp.where(hm, q, jnp.zeros_like(q))
            sc = lax.dot_general(qh, k, (((1,), (1,)), ((), ())), preferred_element_type=F32)
            p = (sc * dm_ref[h]).astype(BF16)
            o = o + jnp.where(hm, jnp.dot(p, v, preferred_element_type=F32), 0.0)
        o2 = o * o
        hi = o2.astype(BF16)
        lo = (o2 - hi.astype(F32)).astype(BF16)
        ms = (jnp.dot(hi, ones_blk, preferred_element_type=F32)
              + jnp.dot(lo, ones_blk, preferred_element_type=F32)) * (1.0 / HEAD_DIM)
        g = g_ref[pl.ds(r0, c_len), :]
        o_ref[pl.ds(r0, c_len), :] = (o * lax.rsqrt(ms + EPS) * (g * jax.nn.sigmoid(g))).astype(BF16)
        return carry

    lax.fori_loop(0, nc, outputs, 0)


def _ret_call(ret_in, g_in, dl_lane, n_lat):
    b, s, rw = g_in.shape
    nc = s // CHUNK
    return pl.pallas_call(
        functools.partial(_ret_kernel, n_lat_chunks=n_lat // CHUNK),
        grid=(b,),
        in_specs=[
            pl.BlockSpec((None, s, 3 * rw), lambda i: (i, 0, 0)),
            pl.BlockSpec((None, s, rw), lambda i: (i, 0, 0)),
            pl.BlockSpec((2, rw), lambda i: (0, 0)),
        ],
        out_specs=pl.BlockSpec((None, s, rw), lambda i: (i, 0, 0)),
        out_shape=jax.ShapeDtypeStruct((b, s, rw), BF16),
        scratch_shapes=[
            pltpu.VMEM((nc, rw, rw), F32),
            pltpu.VMEM((nc, rw, rw), F32),
            pltpu.VMEM((rw // HEAD_DIM, CHUNK, CHUNK), F32),
        ],
        compiler_params=_cparams(("parallel",)),
    )(ret_in, g_in, dl_lane)


def _attn_kernel(qt_ref, k_ref, vt_ref, kn_ref, *rest, group):
    o_ref, rhs_ref, sa_ref, sb_ref, pc_ref, m_ref, acc_ref, out_ref = rest[-8:]
    n_q_heads = qt_ref.shape[0] // HEAD_DIM
    n_chunks = vt_ref.shape[0]
    kc_len = vt_ref.shape[2]
    vrows = vt_ref.shape[1] // (n_q_heads // group)

    kmax2 = jnp.max(kn_ref[...], axis=1, keepdims=True)
    for h in range(n_q_heads):
        qt = qt_ref[h * HEAD_DIM:(h + 1) * HEAD_DIM, :]
        zero = jnp.zeros_like(qt)
        rhs_ref[h] = jnp.concatenate([qt, zero] if h // group == 0 else [zero, qt], axis=0)
        qf = qt.astype(F32)
        qn2 = jnp.sum(qf * qf, axis=0, keepdims=True)
        m_ref[h:h + 1, :] = jnp.sqrt(qn2 * kmax2[h // group:h // group + 1, :]) * SCORE_BOUND_MARGIN
    acc_ref[...] = jnp.zeros(acc_ref.shape, F32)
    bounded = jnp.max(m_ref[...]) <= SCORE_BOUND_CAP

    def scores(c, h):
        c0 = c * kc_len
        if not isinstance(c0, int):
            c0 = pl.multiple_of(c0, kc_len)
        return jnp.dot(k_ref[pl.ds(c0, kc_len), :], rhs_ref[h], preferred_element_type=F32)

    def p_times_v(c, h, p):
        kv = h // group
        return jnp.dot(vt_ref[c, kv * vrows:(kv + 1) * vrows, :], p, preferred_element_type=F32)

    def run_chunks(step):
        for h in range(n_q_heads):
            sa_ref[h] = scores(0, h)

        def pair(i, carry):
            step(2 * i, sa_ref, sb_ref)
            step(2 * i + 1, sb_ref, sa_ref)
            return carry

        lax.fori_loop(0, (n_chunks - 1) // 2, pair, 0)
        if (n_chunks - 1) % 2 == 1:
            step(n_chunks - 2, sa_ref, sb_ref)
            step(n_chunks - 1, sb_ref, None)
        else:
            step(n_chunks - 1, sa_ref, None)

    @pl.when(bounded)
    def _():
        lag = PV_LAG
        pc_ref[...] = jnp.zeros(pc_ref.shape, BF16)

        def step(c, cur_ref, nxt_ref):
            ps = [pc_ref[i] for i in range(lag)]
            c_prev = max(c - 1, 0) if isinstance(c, int) else jnp.maximum(c - 1, 0)
            for h in range(n_q_heads):
                if nxt_ref is not None:
                    nxt_ref[h] = scores(c + 1, h)
                if h < lag:
                    hp = n_q_heads - lag + h
                    acc_ref[hp] += p_times_v(c_prev, hp, ps[h])
                else:
                    acc_ref[h - lag] += p_times_v(c, h - lag, ps[h])
                ps.append(jnp.exp2(cur_ref[h] - m_ref[h:h + 1, :]).astype(BF16))
            for i in range(lag):
                pc_ref[i] = ps[n_q_heads + i]

        run_chunks(step)
        for i in range(lag):
            hp = n_q_heads - lag + i
            acc_ref[hp] += p_times_v(n_chunks - 1, hp, pc_ref[i])

    @pl.when(jnp.logical_not(bounded))
    def _():
        m_ref[...] = jnp.full(m_ref.shape, -jnp.inf, F32)

        def step(c, cur_ref, nxt_ref):
            for h in range(n_q_heads):
                if nxt_ref is not None:
                    nxt_ref[h] = scores(c + 1, h)
                s = cur_ref[h]
                m_old = m_ref[h:h + 1, :]
                m_new = jnp.maximum(m_old, jnp.max(s, axis=0, keepdims=True))
                alpha = jnp.exp2(m_old - m_new)
                m_ref[h:h + 1, :] = m_new
                p = jnp.exp2(s - m_new).astype(BF16)
                acc_ref[h] = alpha * acc_ref[h] + p_times_v(c, h, p)

        run_chunks(step)

    for h in range(n_q_heads):
        a = acc_ref[h]
        out_ref[h * HEAD_DIM:(h + 1) * HEAD_DIM, :] = a[:HEAD_DIM] / a[HEAD_DIM:HEAD_DIM + 1]
    o_ref[...] = out_ref[...].T.astype(BF16)


def _attn_call(qt, k, vt, kn, n_lat, with_ctx, tq=256):
    b, qw, s = qt.shape
    kw = k.shape[2]
    kc_len = vt.shape[3]
    n_ctx = s - n_lat
    n_heads = qw // HEAD_DIM
    n_kv = kw // HEAD_DIM
    group = n_heads // n_kv
    assert n_kv == 2 and n_lat % n_ctx == 0 and n_ctx % tq == 0 and n_ctx % kc_len == 0
    body = functools.partial(_attn_kernel, group=group)
    vw = vt.shape[2]
    scratch = [pltpu.VMEM((n_heads, kw, tq), BF16),
               pltpu.VMEM((n_heads, kc_len, tq), F32), pltpu.VMEM((n_heads, kc_len, tq), F32),
               pltpu.VMEM((PV_LAG, kc_len, tq), BF16),
               pltpu.VMEM((n_heads, tq), F32), pltpu.VMEM((n_heads, vw // n_kv, tq), F32),
               pltpu.VMEM((qw, tq), F32)]
    out_sds = jax.ShapeDtypeStruct((b, s, qw), BF16)
    att = pl.pallas_call(
        body,
        grid=(b, n_lat // tq),
        in_specs=[
            pl.BlockSpec((None, qw, tq), lambda i, j: (i, 0, j)),
            pl.BlockSpec((None, s, kw), lambda i, j: (i, 0, 0)),
            pl.BlockSpec((None, s // kc_len, vw, kc_len), lambda i, j: (i, 0, 0, 0)),
            pl.BlockSpec((None, n_kv, s), lambda i, j: (i, 0, 0)),
        ],
        out_specs=pl.BlockSpec((None, tq, qw), lambda i, j: (i, j, 0)),
        out_shape=out_sds,
        scratch_shapes=scratch,
        compiler_params=_cparams(("parallel", "arbitrary")),
    )(qt, k, vt, kn)
    if not with_ctx:
        return att
    lat_tiles, lat_ctx = n_lat // tq, n_lat // n_ctx
    return pl.pallas_call(
        body,
        grid=(b, n_ctx // tq),
        in_specs=[
            pl.BlockSpec((None, qw, tq), lambda i, j: (i, 0, lat_tiles + j)),
            pl.BlockSpec((None, n_ctx, kw), lambda i, j: (i, lat_ctx, 0)),
            pl.BlockSpec((None, n_ctx // kc_len, vw, kc_len), lambda i, j: (i, lat_ctx, 0, 0)),
            pl.BlockSpec((None, n_kv, n_ctx), lambda i, j: (i, 0, lat_ctx)),
            pl.BlockSpec(memory_space=pl.ANY),
        ],
        out_specs=pl.BlockSpec((None, tq, qw), lambda i, j: (i, lat_tiles + j, 0)),
        out_shape=out_sds,
        input_output_aliases={4: 0},
        scratch_shapes=scratch,
        compiler_params=_cparams(("parallel", "arbitrary")),
    )(qt, k, vt, kn, att)


def _neg_expm1(y, a):
    series = -y * (1.0 + y * (1.0 / 2) * (1.0 + y * (1.0 / 3) * (1.0 + y * (1.0 / 4) * (
        1.0 + y * (1.0 / 5) * (1.0 + y * (1.0 / 6) * (1.0 + y * (1.0 / 7)))))))
    return jnp.where(y > -0.125, series, 1.0 - a * a)


def _lru_pre_kernel(cur_ref, prev_ref, next_ref, cw_ref, cb_ref, wg_ref, bg_ref, lam_ref,
                    af_ref, df_ref, ab_ref, db_ref, *, n_lat_chunks, n_chunks):
    c = pl.program_id(1)
    c_len, w = cur_ref.shape
    first = jnp.logical_or(c == 0, c == n_lat_chunks)
    last = jnp.logical_or(c == n_lat_chunks - 1, c == n_chunks - 1)
    cur = cur_ref[...]
    prev = jnp.where(first, 0.0, prev_ref[...])
    nxt = jnp.where(last, 0.0, next_ref[...])
    win = jnp.concatenate([prev, cur, nxt], axis=0)
    n_win = c_len + 16

    def shifted(off):
        return pltpu.roll(win, (-off) % n_win, axis=0)[8:8 + c_len]

    xr = (shifted(-2) * cw_ref[0:1, :] + shifted(-1) * cw_ref[1:2, :] + cur * cw_ref[2:3, :]
          + shifted(1) * cw_ref[3:4, :] + cb_ref[...])
    gates = jnp.dot(xr.astype(BF16), wg_ref[...], preferred_element_type=F32) + bg_ref[...]
    log_lam = _log_sigmoid(lam_ref[...])
    for d, (a_ref, d_ref) in enumerate(((af_ref, df_ref), (ab_ref, db_ref))):
        r = jax.nn.sigmoid(gates[:, 2 * d * w:(2 * d + 1) * w])
        i = jax.nn.sigmoid(gates[:, (2 * d + 1) * w:(2 * d + 2) * w])
        log_a = LRU_C * r * log_lam[d:d + 1, :]
        a = jnp.exp(log_a)
        a_ref[...] = a
        d_ref[...] = jnp.sqrt(_neg_expm1(2.0 * log_a, a)) * (i * xr)


def _lru_pre_call(lru_in, conv_w, conv_b, wg, bg, lam, n_lat):
    b, s, w2 = lru_in.shape
    w = w2 // 2
    nc = s // CHUNK
    hb = CHUNK // 8
    const = lambda shape: pl.BlockSpec(shape, lambda i, j: tuple(0 for _ in shape))
    out_spec = pl.BlockSpec((CHUNK, w), lambda i, j: (j, i))
    out_sds = jax.ShapeDtypeStruct((s, b * w), F32)
    return pl.pallas_call(
        functools.partial(_lru_pre_kernel, n_lat_chunks=n_lat // CHUNK, n_chunks=nc),
        grid=(b, nc),
        in_specs=[
            pl.BlockSpec((None, CHUNK, w), lambda i, j: (i, j, 0)),
            pl.BlockSpec((None, 8, w), lambda i, j: (i, jnp.maximum(j * hb - 1, 0), 0)),
            pl.BlockSpec((None, 8, w), lambda i, j: (i, jnp.minimum((j + 1) * hb, s // 8 - 1), 0)),
            const(conv_w.shape), const(conv_b.shape), const(wg.shape), const(bg.shape),
            const(lam.shape),
        ],
        out_specs=[out_spec] * 4,
        out_shape=[out_sds] * 4,
        compiler_params=_cparams(("parallel", "parallel")),
    )(lru_in, lru_in, lru_in, conv_w, conv_b, wg, bg, lam)


def _scan_kernel(af_ref, df_ref, ab_ref, db_ref, hf_ref, hb_ref, sf_ref, sb_ref):
    tt = af_ref.shape[0]

    @pl.when(pl.program_id(0) == 0)
    def _():
        sf_ref[...] = jnp.zeros_like(sf_ref)
        sb_ref[...] = jnp.zeros_like(sb_ref)

    def step(t, carry):
        hf, hb = carry
        hf = af_ref[t] * hf + df_ref[t]
        hf_ref[t] = hf
        tb = tt - 1 - t
        hb = ab_ref[tb] * hb + db_ref[tb]
        hb_ref[tb] = hb
        return hf, hb

    hf, hb = lax.fori_loop(0, tt, step, (sf_ref[...], sb_ref[...]), unroll=8)
    sf_ref[...] = hf
    sb_ref[...] = hb


def _scan_call(af, df, ab, db, n_lat):
    s, b, w = af.shape
    nt = s // SCAN_TILE
    nlt = n_lat // SCAN_TILE
    fwd = pl.BlockSpec((SCAN_TILE, b, w), lambda i: ((i + nlt) % nt, 0, 0))
    bwd = pl.BlockSpec((SCAN_TILE, b, w), lambda i: (nt - 1 - i, 0, 0))
    sds = jax.ShapeDtypeStruct((s, b, w), F32)
    return pl.pallas_call(
        _scan_kernel,
        grid=(nt,),
        in_specs=[fwd, fwd, bwd, bwd],
        out_specs=[fwd, bwd],
        out_shape=[sds, sds],
        scratch_shapes=[pltpu.VMEM((b, w), F32), pltpu.VMEM((b, w), F32)],
        compiler_params=_cparams(("arbitrary",)),
    )(af, df, ab, db)


def _outproj_kernel(x_ref, ret_ref, att_ref, hf_ref, hb_ref, lg_ref, mod_ref, w_ref, g_ref, b_ref,
                    o_ref, *, n_lat, alpha):
    tm = x_ref.shape[0]
    is_ctx = _ctx_rows(pl.program_id(1), tm, n_lat)
    rw = ret_ref.shape[1]
    aw = att_ref.shape[1]
    lru = ((hf_ref[...] + hb_ref[...]) * jax.nn.gelu(lg_ref[...])).astype(BF16)
    y = jnp.dot(ret_ref[...], w_ref[0:rw, :], preferred_element_type=F32)
    y = y + jnp.dot(att_ref[...], w_ref[rw:rw + aw, :], preferred_element_type=F32)
    y = y + jnp.dot(lru, w_ref[rw + aw:, :], preferred_element_type=F32)
    z = alpha * x_ref[...] + _row_mod(mod_ref, 2, is_ctx) * y
    o_ref[...] = _layer_norm(z, g_ref[...], b_ref[...])


def _outproj_call(x_all, ret, att, hf, hb, lru_in, mod, w_out, ln_g, ln_b, n_lat, n_rows, alpha, tm=256):
    b, _, d = x_all.shape
    rw, aw = ret.shape[2], att.shape[2]
    tok = lambda last: pl.BlockSpec((None, tm, last), lambda i, j: (i, j, 0))
    col = pl.BlockSpec((tm, rw), lambda i, j: (j, i))
    const = lambda shape: pl.BlockSpec(shape, lambda i, j: tuple(0 for _ in shape))
    return pl.pallas_call(
        functools.partial(_outproj_kernel, n_lat=n_lat, alpha=alpha),
        grid=(b, n_rows // tm),
        in_specs=[
            tok(d), tok(rw), tok(aw), col, col,
            pl.BlockSpec((None, tm, rw), lambda i, j: (i, j, 1)),
            pl.BlockSpec((None, 2, 6, d), lambda i, j: (i, 0, 0, 0)),
            const(w_out.shape), const(ln_g.shape), const(ln_b.shape),
        ],
        out_specs=tok(d),
        out_shape=jax.ShapeDtypeStruct((b, n_rows, d), F32),
        compiler_params=_cparams(("parallel", "parallel")),
    )(x_all, ret, att, hf, hb, lru_in, mod, w_out, ln_g, ln_b)


def _ffn_kernel(x_ref, mod_ref, w1_ref, w2_ref, g_ref, b_ref, o_ref, u_ref, acc_ref, *, n_lat, alpha):
    tm = x_ref.shape[0]
    kf = pl.program_id(2)
    is_ctx = _ctx_rows(pl.program_id(1), tm, n_lat)

    @pl.when(kf == 0)
    def _():
        u_ref[...] = (x_ref[...] * (1.0 + _row_mod(mod_ref, 4, is_ctx))
                      + _row_mod(mod_ref, 3, is_ctx)).astype(BF16)
        acc_ref[...] = jnp.zeros_like(acc_ref)

    h = jnp.maximum(jnp.dot(u_ref[...], w1_ref[...], preferred_element_type=F32), 0.0)
    acc_ref[...] += jnp.dot((h * h).astype(BF16), w2_ref[...], preferred_element_type=F32)

    @pl.when(kf == pl.num_programs(2) - 1)
    def _():
        z = alpha * x_ref[...] + _row_mod(mod_ref, 5, is_ctx) * acc_ref[...]
        o_ref[...] = _layer_norm(z, g_ref[...], b_ref[...])


def _ffn_call(x1, mod, w1, w2, ln_g, ln_b, n_lat, alpha, tm, tf=1024):
    b, s, d = x1.shape
    f = w1.shape[1]
    tok = pl.BlockSpec((None, tm, d), lambda i, j, k: (i, j, 0))
    const = lambda shape: pl.BlockSpec(shape, lambda i, j, k: tuple(0 for _ in shape))
    return pl.pallas_call(
        functools.partial(_ffn_kernel, n_lat=n_lat, alpha=alpha),
        grid=(b, s // tm, f // tf),
        in_specs=[
            tok,
            pl.BlockSpec((None, 2, 6, d), lambda i, j, k: (i, 0, 0, 0)),
            pl.BlockSpec((d, tf), lambda i, j, k: (0, k)),
            pl.BlockSpec((tf, d), lambda i, j, k: (k, 0)),
            const(ln_g.shape), const(ln_b.shape),
        ],
        out_specs=tok,
        out_shape=jax.ShapeDtypeStruct((b, s, d), F32),
        scratch_shapes=[pltpu.VMEM((tm, d), BF16), pltpu.VMEM((tm, d), F32)],
        compiler_params=_cparams(("parallel", "parallel", "arbitrary")),
    )(x1, mod, w1, w2, ln_g, ln_b)


def _rope_tables(n_lat, n_ctx):
    rows = n_lat // GRID_W
    row = jnp.repeat(jnp.arange(rows, dtype=F32), GRID_W)
    col = jnp.tile(jnp.arange(GRID_W, dtype=F32), rows)
    n_freq = HEAD_DIM // 4
    inv = ROPE_THETA ** (-jnp.arange(n_freq, dtype=F32) / n_freq)
    ang = jnp.concatenate([row[:, None] * inv, col[:, None] * inv], axis=-1)
    cos = jnp.concatenate([jnp.cos(ang), jnp.ones((n_ctx, HEAD_DIM // 2), F32)], axis=0)
    sin = jnp.concatenate([jnp.sin(ang), jnp.zeros((n_ctx, HEAD_DIM // 2), F32)], axis=0)
    return cos.T, sin.T


def _block_diag(w):
    k, c = w.shape[-3], w.shape[-2]
    eye = jnp.eye(k, dtype=w.dtype)
    bd = jnp.einsum('...kce,kj->...kcje', w, eye)
    return bd.reshape(*w.shape[:-3], k * c, k * c)


def kernel(x, c, ctx, c_ctx, w_ada, b_ada, w_in, ret_decay_logit, attn_q_gain, attn_k_gain,
           lru_conv_w, lru_conv_b, lru_w_a, lru_b_a, lru_w_x, lru_b_x, lru_lambda,
           w_out, ln1_g, ln1_b, w_ff1, w_ff2, ln2_g, ln2_b):
    b, n_lat, d = x.shape
    n_ctx = ctx.shape[1]
    depth = w_in.shape[0]
    s = n_lat + n_ctx
    rw, aw, kw = d // 4, d // 2, d // 8
    alpha = (2.0 * depth) ** 0.25
    assert n_lat % CHUNK == 0 and n_ctx % CHUNK == 0 and d == 16 * HEAD_DIM

    pad = (-(b + 1)) % 8
    s_in = jnp.concatenate([c, c_ctx[None, :], jnp.zeros((pad, d), F32)], axis=0)
    mods = _ada_call(s_in, w_ada, b_ada)
    mod_lat = mods[:, :b].reshape(depth, b, 1, 6, d)
    mod_ctx = jnp.broadcast_to(mods[:, b].reshape(depth, 1, 1, 6, d), (depth, b, 1, 6, d))
    mod_all = jnp.concatenate([mod_lat, mod_ctx], axis=2)

    o_aq = 4 * rw
    o_lx = o_aq + aw + 2 * kw
    wn = jnp.concatenate([w_in[:, :, :o_aq], w_in[:, :, o_lx:]], axis=2).astype(BF16)
    wt = jnp.swapaxes(w_in[:, :, o_aq:o_lx], 1, 2).astype(BF16)
    wg = jnp.concatenate([_block_diag(lru_w_a[:, 0]), _block_diag(lru_w_x[:, 0]),
                          _block_diag(lru_w_a[:, 1]), _block_diag(lru_w_x[:, 1])], axis=-1).astype(BF16)
    bg = jnp.concatenate([lru_b_a[:, 0], lru_b_x[:, 0], lru_b_a[:, 1], lru_b_x[:, 1]],
                         axis=-1)[:, None, :]
    w_out_b = w_out.astype(BF16)
    w1_b = w_ff1.astype(BF16)
    w2_b = w_ff2.astype(BF16)
    dl_lane = jnp.repeat(ret_decay_logit, HEAD_DIM, axis=-1)
    cos_t, sin_t = _rope_tables(n_lat, n_ctx)

    def ffn_tile(rows):
        return rows // 4 if rows % 32 == 0 else CHUNK

    xa = jnp.concatenate([x, ctx], axis=1)
    for l in range(depth):
        need_ctx = l < depth - 1
        mod = mod_all[l]
        ret_in, g_in, lru_in, qt, k, vt, kn = _inproj_call(
            xa, mod, wn[l], wt[l], cos_t, sin_t,
            attn_q_gain[l][:, None], attn_k_gain[l][:, None], n_lat)
        ret = _ret_call(ret_in, g_in, dl_lane[l], n_lat)
        att = _attn_call(qt, k, vt, kn, n_lat, need_ctx)
        af, df, ab, db = _lru_pre_call(lru_in, lru_conv_w[l], lru_conv_b[l][None, :], wg[l], bg[l],
                                       lru_lambda[l], n_lat)
        shp = (s, b, rw)
        hf, hb = _scan_call(af.reshape(shp), df.reshape(shp), ab.reshape(shp), db.reshape(shp), n_lat)
        n_rows = s if need_ctx else n_lat
        x1 = _outproj_call(xa, ret, att, hf.reshape(s, b * rw), hb.reshape(s, b * rw), lru_in, mod,
                           w_out_b[l], ln1_g[l][None, :], ln1_b[l][None, :], n_lat, n_rows, alpha)
        xa = _ffn_call(x1, mod, w1_b[l], w2_b[l], ln2_g[l][None, :], ln2_b[l][None, :],
                       n_lat, alpha, ffn_tile(n_rows))
    return xa
```

```python
import functools
import math

import jax
import jax.numpy as jnp
from jax import lax
from jax.experimental import pallas as pl
from jax.experimental.pallas import tpu as pltpu

F32 = jnp.float32
BF16 = jnp.bfloat16

HEAD_DIM = 64
GRID_W = 64
ROPE_THETA = 10000.0
LRU_C = 8.0
LRU_BLOCKS = 4
EPS = 1e-6
LOG2E = 1.4426950408889634

ONES_ROWS = 16
SCORE_BOUND_MARGIN = 1.0 + 2.0 ** -7
SCORE_BOUND_CAP = 60.0
PV_LAG = 2
CHUNK = 256
SCAN_TILE = 128
VMEM_LIMIT = 56 * 1024 * 1024


def _cparams(sem):
    return pltpu.CompilerParams(dimension_semantics=sem, vmem_limit_bytes=VMEM_LIMIT)


def _log_sigmoid(x):
    return jnp.minimum(x, 0.0) - jnp.log1p(jnp.exp(-jnp.abs(x)))


def _layer_norm(z, g, b):
    mu = jnp.mean(z, axis=-1, keepdims=True)
    zc = z - mu
    var = jnp.mean(zc * zc, axis=-1, keepdims=True)
    return zc * lax.rsqrt(var + EPS) * g + b


def _row_mod(mod_ref, idx, is_ctx):
    return jnp.where(is_ctx, mod_ref[1, idx:idx + 1, :], mod_ref[0, idx:idx + 1, :])


def _ctx_rows(tile_idx, tm, n_lat):
    rows = tile_idx * tm + lax.broadcasted_iota(jnp.int32, (tm, 1), 0)
    return rows >= n_lat


def _ada_kernel(s_ref, w_ref, b_ref, o_ref):
    s = s_ref[...]
    s = s * jax.nn.sigmoid(s)
    o_ref[...] = jnp.dot(s.astype(BF16), w_ref[...].astype(BF16),
                         preferred_element_type=F32) + b_ref[...]


def _ada_call(s_in, w_ada, b_ada):
    depth, d, d6 = w_ada.shape
    rows = s_in.shape[0]
    tn = d6 // 4
    return pl.pallas_call(
        _ada_kernel,
        grid=(depth, d6 // tn),
        in_specs=[
            pl.BlockSpec((rows, d), lambda l, j: (0, 0)),
            pl.BlockSpec((None, d, tn), lambda l, j: (l, 0, j)),
            pl.BlockSpec((None, 1, tn), lambda l, j: (l, 0, j)),
        ],
        out_specs=pl.BlockSpec((None, rows, tn), lambda l, j: (l, 0, j)),
        out_shape=jax.ShapeDtypeStruct((depth, rows, d6), F32),
        compiler_params=_cparams(("parallel", "parallel")),
    )(s_in, w_ada, b_ada.reshape(depth, 1, d6))


def _inproj_kernel(x_ref, mod_ref, wn_ref, wt_ref, cos_ref, sin_ref, qg_ref, kg_ref,
                   ret_ref, g_ref, lru_ref, qt_ref, k_ref, vt_ref, kn_ref, *, n_lat):
    tm = x_ref.shape[0]
    is_ctx = _ctx_rows(pl.program_id(1), tm, n_lat)
    x = x_ref[...]
    u = (x * (1.0 + _row_mod(mod_ref, 1, is_ctx)) + _row_mod(mod_ref, 0, is_ctx)).astype(BF16)

    pn = jnp.dot(u, wn_ref[...], preferred_element_type=F32)
    rw = g_ref.shape[1]
    ret_ref[:, 0:rw] = pn[:, 0:rw].astype(BF16)
    ret_ref[:, rw:2 * rw] = (pn[:, rw:2 * rw] * HEAD_DIM ** -0.5).astype(BF16)
    ret_ref[:, 2 * rw:3 * rw] = pn[:, 2 * rw:3 * rw].astype(BF16)
    g_ref[...] = pn[:, 3 * rw:4 * rw]
    lru_ref[...] = pn[:, 4 * rw:]

    pt = lax.dot_general(wt_ref[...], u, (((1,), (1,)), ((), ())), preferred_element_type=F32)
    cos = cos_ref[...]
    sin = sin_ref[...]
    half = HEAD_DIM // 2

    def norm_rope(t, gain):
        ms = jnp.mean(t * t, axis=0, keepdims=True)
        t = t * lax.rsqrt(ms + EPS) * gain
        x1, x2 = t[:half], t[half:]
        return jnp.concatenate([x1 * cos - x2 * sin, x1 * sin + x2 * cos], axis=0)

    qw = qt_ref.shape[0]
    kw = k_ref.shape[1]
    qscale = HEAD_DIM ** -0.5 * LOG2E
    for h in range(qw // HEAD_DIM):
        r = h * HEAD_DIM
        qt_ref[r:r + HEAD_DIM, :] = (norm_rope(pt[r:r + HEAD_DIM], qg_ref[...]) * qscale).astype(BF16)
    kt = jnp.concatenate(
        [norm_rope(pt[qw + h * HEAD_DIM:qw + (h + 1) * HEAD_DIM], kg_ref[...])
         for h in range(kw // HEAD_DIM)], axis=0)
    k_ref[...] = kt.T.astype(BF16)
    kf = kt.astype(BF16).astype(F32)
    for h in range(kw // HEAD_DIM):
        kh = kf[h * HEAD_DIM:(h + 1) * HEAD_DIM]
        kn_ref[h:h + 1, :] = jnp.sum(kh * kh, axis=0, keepdims=True)
    ones = jnp.ones((ONES_ROWS, tm), BF16)
    vrows = HEAD_DIM + ONES_ROWS
    for h in range(kw // HEAD_DIM):
        v0 = qw + kw + h * HEAD_DIM
        vt_ref[h * vrows:h * vrows + HEAD_DIM, :] = pt[v0:v0 + HEAD_DIM].astype(BF16)
        vt_ref[h * vrows + HEAD_DIM:(h + 1) * vrows, :] = ones


def _inproj_call(x_all, mod, wn, wt, cos_t, sin_t, qg, kg, n_lat, tm=256):
    b, s, d = x_all.shape
    rw, qw, kw = d // 4, d // 2, d // 8
    vw = (kw // HEAD_DIM) * (HEAD_DIM + ONES_ROWS)
    grid = (b, s // tm)
    tok = lambda shape_last: pl.BlockSpec((None, tm, shape_last), lambda i, j: (i, j, 0))
    tr = lambda rows: pl.BlockSpec((None, rows, tm), lambda i, j: (i, 0, j))
    const = lambda shape: pl.BlockSpec(shape, lambda i, j: tuple(0 for _ in shape))
    return pl.pallas_call(
        functools.partial(_inproj_kernel, n_lat=n_lat),
        grid=grid,
        in_specs=[
            tok(d),
            pl.BlockSpec((None, 2, 6, d), lambda i, j: (i, 0, 0, 0)),
            const(wn.shape), const(wt.shape),
            pl.BlockSpec((HEAD_DIM // 2, tm), lambda i, j: (0, j)),
            pl.BlockSpec((HEAD_DIM // 2, tm), lambda i, j: (0, j)),
            const(qg.shape), const(kg.shape),
        ],
        out_specs=[tok(3 * rw), tok(rw), tok(2 * rw), tr(qw), tok(kw),
                   pl.BlockSpec((None, None, vw, tm), lambda i, j: (i, j, 0, 0)),
                   tr(kw // HEAD_DIM)],
        out_shape=[
            jax.ShapeDtypeStruct((b, s, 3 * rw), BF16),
            jax.ShapeDtypeStruct((b, s, rw), F32),
            jax.ShapeDtypeStruct((b, s, 2 * rw), F32),
            jax.ShapeDtypeStruct((b, qw, s), BF16),
            jax.ShapeDtypeStruct((b, s, kw), BF16),
            jax.ShapeDtypeStruct((b, s // tm, vw, tm), BF16),
            jax.ShapeDtypeStruct((b, kw // HEAD_DIM, s), F32),
        ],
        compiler_params=_cparams(("parallel", "parallel")),
    )(x_all, mod, wn, wt, cos_t, sin_t, qg, kg)


def _ret_kernel(ret_ref, g_ref, dl_ref, o_ref, sf_ref, sb_ref, dm_ref, *, n_lat_chunks):
    c_len = CHUNK
    s_len, rw = g_ref.shape
    nc = s_len // c_len
    ncl = n_lat_chunks
    n_heads = rw // HEAD_DIM

    lg = _log_sigmoid(dl_ref[...])
    lgf, lgb = lg[0:1], lg[1:2]
    pos = lax.broadcasted_iota(jnp.int32, (c_len, 1), 0).astype(F32)
    kwf = jnp.exp(lgf * (c_len - 1.0 - pos))
    kwb = jnp.exp(lgb * pos)
    qwf = jnp.exp(lgf * (pos + 1.0))
    qwb = jnp.exp(lgb * (c_len - pos))
    cdf = jnp.exp(lgf * c_len)
    cdb = jnp.exp(lgb * c_len)
    row_head = lax.broadcasted_iota(jnp.int32, (rw, rw), 0) // HEAD_DIM
    col_head = lax.broadcasted_iota(jnp.int32, (rw, rw), 1) // HEAD_DIM
    same_head = row_head == col_head
    lane_head = lax.broadcasted_iota(jnp.int32, (1, rw), 1) // HEAD_DIM

    ii = lax.broadcasted_iota(jnp.int32, (c_len, c_len), 0)
    jj = lax.broadcasted_iota(jnp.int32, (c_len, c_len), 1)
    dij = (ii - jj).astype(F32)
    for h in range(n_heads):
        lf = lgf[:, h * HEAD_DIM:h * HEAD_DIM + 1]
        lb = lgb[:, h * HEAD_DIM:h * HEAD_DIM + 1]
        dm_ref[h] = jnp.exp(jnp.where(dij >= 0.0, lf * dij, -lb * dij))

    def contrib(c, carry):
        r0 = pl.multiple_of(c * c_len, c_len)
        k = ret_ref[pl.ds(r0, c_len), rw:2 * rw].astype(F32)
        v = ret_ref[pl.ds(r0, c_len), 2 * rw:3 * rw]
        tn = (((0,), (0,)), ((), ()))
        cf = lax.dot_general((k * kwf).astype(BF16), v, tn, preferred_element_type=F32)
        cb = lax.dot_general((k * kwb).astype(BF16), v, tn, preferred_element_type=F32)
        sf_ref[c] = jnp.where(same_head, cf, 0.0)
        sb_ref[c] = jnp.where(same_head, cb, 0.0)
        return carry

    lax.fori_loop(0, nc, contrib, 0)

    def chain(ref, decay, order):
        state = jnp.zeros((rw, rw), F32)
        for c in order:
            u = ref[c]
            ref[c] = state
            state = decay * state + u

    chain(sf_ref, cdf, list(range(ncl, nc)) + list(range(ncl)))
    chain(sb_ref, cdb, list(range(nc - 1, ncl - 1, -1)) + list(range(ncl - 1, -1, -1)))

    ones_blk = jnp.where(same_head, 1.0, 0.0).astype(BF16)

    def outputs(c, carry):
        r0 = pl.multiple_of(c * c_len, c_len)
        q = ret_ref[pl.ds(r0, c_len), 0:rw]
        k = ret_ref[pl.ds(r0, c_len), rw:2 * rw]
        v = ret_ref[pl.ds(r0, c_len), 2 * rw:3 * rw]
        qf = q.astype(F32)
        qi = jnp.concatenate([(qf * qwf).astype(BF16), (qf * qwb).astype(BF16)], axis=1)
        st = jnp.concatenate([sf_ref[c], sb_ref[c]], axis=0).astype(BF16)
        o = jnp.dot(qi, st, preferred_element_type=F32)
        scs = [lax.dot_general(jnp.where(lane_head == h, q, jnp.zeros_like(q)), k,
                               (((1,), (1,)), ((), ())), preferred_element_type=F32)
               for h in range(n_heads)]
        for h in range(n_heads):
            p = (scs[h] * dm_ref[h]).astype(BF16)
            o = o + jnp.where(lane_head == h, jnp.dot(p, v, preferred_element_type=F32), 0.0)
        o2 = o * o
        hi = o2.astype(BF16)
        lo = (o2 - hi.astype(F32)).astype(BF16)
        ms = (jnp.dot(hi, ones_blk, preferred_element_type=F32)
              + jnp.dot(lo, ones_blk, preferred_element_type=F32)) * (1.0 / HEAD_DIM)
        g = g_ref[pl.ds(r0, c_len), :]
        o_ref[pl.ds(r0, c_len), :] = (o * lax.rsqrt(ms + EPS) * (g * jax.nn.sigmoid(g))).astype(BF16)
        return carry

    lax.fori_loop(0, nc, outputs, 0)


def _ret_call(ret_in, g_in, dl_lane, n_lat):
    b, s, rw = g_in.shape
    nc = s // CHUNK
    return pl.pallas_call(
        functools.partial(_ret_kernel, n_lat_chunks=n_lat // CHUNK),
        grid=(b,),
        in_specs=[
            pl.BlockSpec((None, s, 3 * rw), lambda i: (i, 0, 0)),
            pl.BlockSpec((None, s, rw), lambda i: (i, 0, 0)),
            pl.BlockSpec((2, rw), lambda i: (0, 0)),
        ],
        out_specs=pl.BlockSpec((None, s, rw), lambda i: (i, 0, 0)),
        out_shape=jax.ShapeDtypeStruct((b, s, rw), BF16),
        scratch_shapes=[
            pltpu.VMEM((nc, rw, rw), F32),
            pltpu.VMEM((nc, rw, rw), F32),
            pltpu.VMEM((rw // HEAD_DIM, CHUNK, CHUNK), F32),
        ],
        compiler_params=_cparams(("parallel",)),
    )(ret_in, g_in, dl_lane)


def _attn_kernel(qt_ref, k_ref, vt_ref, kn_ref, *rest, group):
    o_ref, rhs_ref, sa_ref, sb_ref, pc_ref, m_ref, acc_ref, out_ref = rest[-8:]
    n_q_heads = qt_ref.shape[0] // HEAD_DIM
    n_chunks = vt_ref.shape[0]
    kc_len = vt_ref.shape[2]
    vrows = vt_ref.shape[1] // (n_q_heads // group)

    kmax2 = jnp.max(kn_ref[...], axis=1, keepdims=True)
    for h in range(n_q_heads):
        qt = qt_ref[h * HEAD_DIM:(h + 1) * HEAD_DIM, :]
        zero = jnp.zeros_like(qt)
        rhs_ref[h] = jnp.concatenate([qt, zero] if h // group == 0 else [zero, qt], axis=0)
        qf = qt.astype(F32)
        qn2 = jnp.sum(qf * qf, axis=0, keepdims=True)
        m_ref[h:h + 1, :] = jnp.sqrt(qn2 * kmax2[h // group:h // group + 1, :]) * SCORE_BOUND_MARGIN
    acc_ref[...] = jnp.zeros(acc_ref.shape, F32)
    bounded = jnp.max(m_ref[...]) <= SCORE_BOUND_CAP

    def scores(c, h):
        c0 = c * kc_len
        if not isinstance(c0, int):
            c0 = pl.multiple_of(c0, kc_len)
        return jnp.dot(k_ref[pl.ds(c0, kc_len), :], rhs_ref[h], preferred_element_type=F32)

    def p_times_v(c, h, p):
        kv = h // group
        return jnp.dot(vt_ref[c, kv * vrows:(kv + 1) * vrows, :], p, preferred_element_type=F32)

    def run_chunks(step):
        for h in range(n_q_heads):
            sa_ref[h] = scores(0, h)

        def pair(i, carry):
            step(2 * i, sa_ref, sb_ref)
            step(2 * i + 1, sb_ref, sa_ref)
            return carry

        lax.fori_loop(0, (n_chunks - 1) // 2, pair, 0)
        if (n_chunks - 1) % 2 == 1:
            step(n_chunks - 2, sa_ref, sb_ref)
            step(n_chunks - 1, sb_ref, None)
        else:
            step(n_chunks - 1, sa_ref, None)

    @pl.when(bounded)
    def _():
        lag = PV_LAG
        pc_ref[...] = jnp.zeros(pc_ref.shape, BF16)

        def step(c, cur_ref, nxt_ref):
            ps = [pc_ref[i] for i in range(lag)]
            c_prev = max(c - 1, 0) if isinstance(c, int) else jnp.maximum(c - 1, 0)
            for h in range(n_q_heads):
                if nxt_ref is not None:
                    nxt_ref[h] = scores(c + 1, h)
                if h < lag:
                    hp = n_q_heads - lag + h
                    acc_ref[hp] += p_times_v(c_prev, hp, ps[h])
                else:
                    acc_ref[h - lag] += p_times_v(c, h - lag, ps[h])
                ps.append(jnp.exp2(cur_ref[h] - m_ref[h:h + 1, :]).astype(BF16))
            for i in range(lag):
                pc_ref[i] = ps[n_q_heads + i]

        run_chunks(step)
        for i in range(lag):
            hp = n_q_heads - lag + i
            acc_ref[hp] += p_times_v(n_chunks - 1, hp, pc_ref[i])

    @pl.when(jnp.logical_not(bounded))
    def _():
        m_ref[...] = jnp.full(m_ref.shape, -jnp.inf, F32)

        def step(c, cur_ref, nxt_ref):
            for h in range(n_q_heads):
                if nxt_ref is not None:
                    nxt_ref[h] = scores(c + 1, h)
                s = cur_ref[h]
                m_old = m_ref[h:h + 1, :]
                m_new = jnp.maximum(m_old, jnp.max(s, axis=0, keepdims=True))
                alpha = jnp.exp2(m_old - m_new)
                m_ref[h:h + 1, :] = m_new
                p = jnp.exp2(s - m_new).astype(BF16)
                acc_ref[h] = alpha * acc_ref[h] + p_times_v(c, h, p)

        run_chunks(step)

    for h in range(n_q_heads):
        a = acc_ref[h]
        out_ref[h * HEAD_DIM:(h + 1) * HEAD_DIM, :] = a[:HEAD_DIM] / a[HEAD_DIM:HEAD_DIM + 1]
    o_ref[...] = out_ref[...].T.astype(BF16)


def _attn_call(qt, k, vt, kn, n_lat, with_ctx, tq=256):
    b, qw, s = qt.shape
    kw = k.shape[2]
    kc_len = vt.shape[3]
    n_ctx = s - n_lat
    n_heads = qw // HEAD_DIM
    n_kv = kw // HEAD_DIM
    group = n_heads // n_kv
    assert n_kv == 2 and n_lat % n_ctx == 0 and n_ctx % tq == 0 and n_ctx % kc_len == 0
    body = functools.partial(_attn_kernel, group=group)
    vw = vt.shape[2]
    scratch = [pltpu.VMEM((n_heads, kw, tq), BF16),
               pltpu.VMEM((n_heads, kc_len, tq), F32), pltpu.VMEM((n_heads, kc_len, tq), F32),
               pltpu.VMEM((PV_LAG, kc_len, tq), BF16),
               pltpu.VMEM((n_heads, tq), F32), pltpu.VMEM((n_heads, vw // n_kv, tq), F32),
               pltpu.VMEM((qw, tq), F32)]
    out_sds = jax.ShapeDtypeStruct((b, s, qw), BF16)
    att = pl.pallas_call(
        body,
        grid=(b, n_lat // tq),
        in_specs=[
            pl.BlockSpec((None, qw, tq), lambda i, j: (i, 0, j)),
            pl.BlockSpec((None, s, kw), lambda i, j: (i, 0, 0)),
            pl.BlockSpec((None, s // kc_len, vw, kc_len), lambda i, j: (i, 0, 0, 0)),
            pl.BlockSpec((None, n_kv, s), lambda i, j: (i, 0, 0)),
        ],
        out_specs=pl.BlockSpec((None, tq, qw), lambda i, j: (i, j, 0)),
        out_shape=out_sds,
        scratch_shapes=scratch,
        compiler_params=_cparams(("parallel", "arbitrary")),
    )(qt, k, vt, kn)
    if not with_ctx:
        return att
    lat_tiles, lat_ctx = n_lat // tq, n_lat // n_ctx
    return pl.pallas_call(
        body,
        grid=(b, n_ctx // tq),
        in_specs=[
            pl.BlockSpec((None, qw, tq), lambda i, j: (i, 0, lat_tiles + j)),
            pl.BlockSpec((None, n_ctx, kw), lambda i, j: (i, lat_ctx, 0)),
            pl.BlockSpec((None, n_ctx // kc_len, vw, kc_len), lambda i, j: (i, lat_ctx, 0, 0)),
            pl.BlockSpec((None, n_kv, n_ctx), lambda i, j: (i, 0, lat_ctx)),
            pl.BlockSpec(memory_space=pl.ANY),
        ],
        out_specs=pl.BlockSpec((None, tq, qw), lambda i, j: (i, lat_tiles + j, 0)),
        out_shape=out_sds,
        input_output_aliases={4: 0},
        scratch_shapes=scratch,
        compiler_params=_cparams(("parallel", "arbitrary")),
    )(qt, k, vt, kn, att)


def _neg_expm1(y, a):
    series = -y * (1.0 + y * (1.0 / 2) * (1.0 + y * (1.0 / 3) * (1.0 + y * (1.0 / 4))))
    return jnp.where(y > -2.0 ** -6, series, 1.0 - a * a)


def _lru_pre_kernel(cur_ref, prev_ref, next_ref, cw_ref, cb_ref, wg_ref, bg_ref, lam_ref,
                    af_ref, df_ref, ab_ref, db_ref, *, n_lat_chunks, n_chunks):
    c = pl.program_id(1)
    c_len, w = cur_ref.shape
    first = jnp.logical_or(c == 0, c == n_lat_chunks)
    last = jnp.logical_or(c == n_lat_chunks - 1, c == n_chunks - 1)
    cur = cur_ref[...]
    prev = jnp.where(first, 0.0, prev_ref[...])
    nxt = jnp.where(last, 0.0, next_ref[...])
    win = jnp.concatenate([prev, cur, nxt], axis=0)
    n_win = c_len + 16

    def shifted(off):
        return pltpu.roll(win, (-off) % n_win, axis=0)[8:8 + c_len]

    xr = (shifted(-2) * cw_ref[0:1, :] + shifted(-1) * cw_ref[1:2, :] + cur * cw_ref[2:3, :]
          + shifted(1) * cw_ref[3:4, :] + cb_ref[...])
    gates = jnp.dot(xr.astype(BF16), wg_ref[...], preferred_element_type=F32) + bg_ref[...]
    log_lam = _log_sigmoid(lam_ref[...])
    for d, (a_ref, d_ref) in enumerate(((af_ref, df_ref), (ab_ref, db_ref))):
        r = jax.nn.sigmoid(gates[:, 2 * d * w:(2 * d + 1) * w])
        i = jax.nn.sigmoid(gates[:, (2 * d + 1) * w:(2 * d + 2) * w])
        log_a = LRU_C * r * log_lam[d:d + 1, :]
        a = jnp.exp(log_a)
        a_ref[...] = a
        d_ref[...] = jnp.sqrt(_neg_expm1(2.0 * log_a, a)) * (i * xr)


def _lru_pre_call(lru_in, conv_w, conv_b, wg, bg, lam, n_lat):
    b, s, w2 = lru_in.shape
    w = w2 // 2
    nc = s // CHUNK
    hb = CHUNK // 8
    const = lambda shape: pl.BlockSpec(shape, lambda i, j: tuple(0 for _ in shape))
    out_spec = pl.BlockSpec((CHUNK, w), lambda i, j: (j, i))
    out_sds = jax.ShapeDtypeStruct((s, b * w), F32)
    return pl.pallas_call(
        functools.partial(_lru_pre_kernel, n_lat_chunks=n_lat // CHUNK, n_chunks=nc),
        grid=(b, nc),
        in_specs=[
            pl.BlockSpec((None, CHUNK, w), lambda i, j: (i, j, 0)),
            pl.BlockSpec((None, 8, w), lambda i, j: (i, jnp.maximum(j * hb - 1, 0), 0)),
            pl.BlockSpec((None, 8, w), lambda i, j: (i, jnp.minimum((j + 1) * hb, s // 8 - 1), 0)),
            const(conv_w.shape), const(conv_b.shape), const(wg.shape), const(bg.shape),
            const(lam.shape),
        ],
        out_specs=[out_spec] * 4,
        out_shape=[out_sds] * 4,
        compiler_params=_cparams(("parallel", "parallel")),
    )(lru_in, lru_in, lru_in, conv_w, conv_b, wg, bg, lam)


def _scan_kernel(af_ref, df_ref, ab_ref, db_ref, hf_ref, hb_ref, sf_ref, sb_ref):
    tt = af_ref.shape[0]

    @pl.when(pl.program_id(0) == 0)
    def _():
        sf_ref[...] = jnp.zeros_like(sf_ref)
        sb_ref[...] = jnp.zeros_like(sb_ref)

    def step(t, carry):
        hf, hb = carry
        hf = af_ref[t] * hf + df_ref[t]
        hf_ref[t] = hf
        tb = tt - 1 - t
        hb = ab_ref[tb] * hb + db_ref[tb]
        hb_ref[tb] = hb
        return hf, hb

    hf, hb = lax.fori_loop(0, tt, step, (sf_ref[...], sb_ref[...]), unroll=8)
    sf_ref[...] = hf
    sb_ref[...] = hb


def _scan_call(af, df, ab, db, n_lat):
    s, b, w = af.shape
    nt = s // SCAN_TILE
    nlt = n_lat // SCAN_TILE
    fwd = pl.BlockSpec((SCAN_TILE, b, w), lambda i: ((i + nlt) % nt, 0, 0))
    bwd = pl.BlockSpec((SCAN_TILE, b, w), lambda i: (nt - 1 - i, 0, 0))
    sds = jax.ShapeDtypeStruct((s, b, w), F32)
    return pl.pallas_call(
        _scan_kernel,
        grid=(nt,),
        in_specs=[fwd, fwd, bwd, bwd],
        out_specs=[fwd, bwd],
        out_shape=[sds, sds],
        scratch_shapes=[pltpu.VMEM((b, w), F32), pltpu.VMEM((b, w), F32)],
        compiler_params=_cparams(("arbitrary",)),
    )(af, df, ab, db)


def _tail_kernel(x_ref, ret_ref, att_ref, hf_ref, hb_ref, lg_ref, mod_ref, wo_ref, g1_ref, b1_ref,
                 w1_ref, w2_ref, g2_ref, b2_ref, o_ref, x1_ref, u_ref, acc_ref, *, n_lat, alpha):
    tm = x_ref.shape[0]
    kf = pl.program_id(2)
    is_ctx = _ctx_rows(pl.program_id(1), tm, n_lat)

    @pl.when(kf == 0)
    def _():
        rw = ret_ref.shape[1]
        aw = att_ref.shape[1]
        lru = ((hf_ref[...] + hb_ref[...]) * jax.nn.gelu(lg_ref[...])).astype(BF16)
        y = jnp.dot(ret_ref[...], wo_ref[0:rw, :], preferred_element_type=F32)
        y = y + jnp.dot(att_ref[...], wo_ref[rw:rw + aw, :], preferred_element_type=F32)
        y = y + jnp.dot(lru, wo_ref[rw + aw:, :], preferred_element_type=F32)
        x1 = _layer_norm(alpha * x_ref[...] + _row_mod(mod_ref, 2, is_ctx) * y, g1_ref[...], b1_ref[...])
        x1_ref[...] = x1
        u_ref[...] = (x1 * (1.0 + _row_mod(mod_ref, 4, is_ctx)) + _row_mod(mod_ref, 3, is_ctx)).astype(BF16)
        acc_ref[...] = jnp.zeros_like(acc_ref)

    h = jnp.maximum(jnp.dot(u_ref[...], w1_ref[...], preferred_element_type=F32), 0.0)
    acc_ref[...] += jnp.dot((h * h).astype(BF16), w2_ref[...], preferred_element_type=F32)

    @pl.when(kf == pl.num_programs(2) - 1)
    def _():
        z = alpha * x1_ref[...] + _row_mod(mod_ref, 5, is_ctx) * acc_ref[...]
        o_ref[...] = _layer_norm(z, g2_ref[...], b2_ref[...])


def _tail_call(x_all, ret, att, hf, hb, lru_in, mod, w_out, ln1_g, ln1_b, w1, w2, ln2_g, ln2_b,
               n_lat, n_rows, alpha, tm, tf=1024):
    b, _, d = x_all.shape
    rw, aw = ret.shape[2], att.shape[2]
    f = w1.shape[1]
    tok = lambda last: pl.BlockSpec((None, tm, last), lambda i, j, k: (i, j, 0))
    col = pl.BlockSpec((tm, rw), lambda i, j, k: (j, i))
    const = lambda shape: pl.BlockSpec(shape, lambda i, j, k: tuple(0 for _ in shape))
    return pl.pallas_call(
        functools.partial(_tail_kernel, n_lat=n_lat, alpha=alpha),
        grid=(b, n_rows // tm, f // tf),
        in_specs=[
            tok(d), tok(rw), tok(aw), col, col,
            pl.BlockSpec((None, tm, rw), lambda i, j, k: (i, j, 1)),
            pl.BlockSpec((None, 2, 6, d), lambda i, j, k: (i, 0, 0, 0)),
            const(w_out.shape), const(ln1_g.shape), const(ln1_b.shape),
            pl.BlockSpec((d, tf), lambda i, j, k: (0, k)),
            pl.BlockSpec((tf, d), lambda i, j, k: (k, 0)),
            const(ln2_g.shape), const(ln2_b.shape),
        ],
        out_specs=tok(d),
        out_shape=jax.ShapeDtypeStruct((b, n_rows, d), F32),
        scratch_shapes=[pltpu.VMEM((tm, d), F32), pltpu.VMEM((tm, d), BF16), pltpu.VMEM((tm, d), F32)],
        compiler_params=_cparams(("parallel", "parallel", "arbitrary")),
    )(x_all, ret, att, hf, hb, lru_in, mod, w_out, ln1_g, ln1_b, w1, w2, ln2_g, ln2_b)


def _rope_tables(n_lat, n_ctx):
    rows = n_lat // GRID_W
    row = jnp.repeat(jnp.arange(rows, dtype=F32), GRID_W)
    col = jnp.tile(jnp.arange(GRID_W, dtype=F32), rows)
    n_freq = HEAD_DIM // 4
    inv = ROPE_THETA ** (-jnp.arange(n_freq, dtype=F32) / n_freq)
    ang = jnp.concatenate([row[:, None] * inv, col[:, None] * inv], axis=-1)
    cos = jnp.concatenate([jnp.cos(ang), jnp.ones((n_ctx, HEAD_DIM // 2), F32)], axis=0)
    sin = jnp.concatenate([jnp.sin(ang), jnp.zeros((n_ctx, HEAD_DIM // 2), F32)], axis=0)
    return cos.T, sin.T


def _block_diag(w):
    k, c = w.shape[-3], w.shape[-2]
    eye = jnp.eye(k, dtype=w.dtype)
    bd = jnp.einsum('...kce,kj->...kcje', w, eye)
    return bd.reshape(*w.shape[:-3], k * c, k * c)


def kernel(x, c, ctx, c_ctx, w_ada, b_ada, w_in, ret_decay_logit, attn_q_gain, attn_k_gain,
           lru_conv_w, lru_conv_b, lru_w_a, lru_b_a, lru_w_x, lru_b_x, lru_lambda,
           w_out, ln1_g, ln1_b, w_ff1, w_ff2, ln2_g, ln2_b):
    b, n_lat, d = x.shape
    n_ctx = ctx.shape[1]
    depth = w_in.shape[0]
    s = n_lat + n_ctx
    rw, aw, kw = d // 4, d // 2, d // 8
    alpha = (2.0 * depth) ** 0.25
    assert n_lat % CHUNK == 0 and n_ctx % CHUNK == 0 and d == 16 * HEAD_DIM

    pad = (-(b + 1)) % 8
    s_in = jnp.concatenate([c, c_ctx[None, :], jnp.zeros((pad, d), F32)], axis=0)
    mods = _ada_call(s_in, w_ada, b_ada)
    mod_lat = mods[:, :b].reshape(depth, b, 1, 6, d)
    mod_ctx = jnp.broadcast_to(mods[:, b].reshape(depth, 1, 1, 6, d), (depth, b, 1, 6, d))
    mod_all = jnp.concatenate([mod_lat, mod_ctx], axis=2)

    o_aq = 4 * rw
    o_lx = o_aq + aw + 2 * kw
    wn = jnp.concatenate([w_in[:, :, :o_aq], w_in[:, :, o_lx:]], axis=2).astype(BF16)
    wt = jnp.swapaxes(w_in[:, :, o_aq:o_lx], 1, 2).astype(BF16)
    wg = jnp.concatenate([_block_diag(lru_w_a[:, 0]), _block_diag(lru_w_x[:, 0]),
                          _block_diag(lru_w_a[:, 1]), _block_diag(lru_w_x[:, 1])], axis=-1).astype(BF16)
    bg = jnp.concatenate([lru_b_a[:, 0], lru_b_x[:, 0], lru_b_a[:, 1], lru_b_x[:, 1]],
                         axis=-1)[:, None, :]
    w_out_b = w_out.astype(BF16)
    w1_b = w_ff1.astype(BF16)
    w2_b = w_ff2.astype(BF16)
    dl_lane = jnp.repeat(ret_decay_logit, HEAD_DIM, axis=-1)
    cos_t, sin_t = _rope_tables(n_lat, n_ctx)

    def tail_tile(rows):
        return rows // 8 if rows % 64 == 0 else CHUNK

    xa = jnp.concatenate([x, ctx], axis=1)
    for l in range(depth):
        need_ctx = l < depth - 1
        mod = mod_all[l]
        ret_in, g_in, lru_in, qt, k, vt, kn = _inproj_call(
            xa, mod, wn[l], wt[l], cos_t, sin_t,
            attn_q_gain[l][:, None], attn_k_gain[l][:, None], n_lat)
        ret = _ret_call(ret_in, g_in, dl_lane[l], n_lat)
        att = _attn_call(qt, k, vt, kn, n_lat, need_ctx)
        af, df, ab, db = _lru_pre_call(lru_in, lru_conv_w[l], lru_conv_b[l][None, :], wg[l], bg[l],
                                       lru_lambda[l], n_lat)
        shp = (s, b, rw)
        hf, hb = _scan_call(af.reshape(shp), df.reshape(shp), ab.reshape(shp), db.reshape(shp), n_lat)
        n_rows = s if need_ctx else n_lat
        xa = _tail_call(xa, ret, att, hf.reshape(s, b * rw), hb.reshape(s, b * rw), lru_in, mod,
                        w_out_b[l], ln1_g[l][None, :], ln1_b[l][None, :], w1_b[l], w2_b[l],
                        ln2_g[l][None, :], ln2_b[l][None, :], n_lat, n_rows, alpha, tail_tile(n_rows))
    return xa
```

```python
import functools
import math

import jax
import jax.numpy as jnp
from jax import lax
from jax.experimental import pallas as pl
from jax.experimental.pallas import tpu as pltpu

F32 = jnp.float32
BF16 = jnp.bfloat16

HEAD_DIM = 64
GRID_W = 64
ROPE_THETA = 10000.0
LRU_C = 8.0
LRU_BLOCKS = 4
EPS = 1e-6
LOG2E = 1.4426950408889634

ONES_ROWS = 16
SCORE_BOUND_MARGIN = 1.0 + 2.0 ** -7
SCORE_BOUND_CAP = 60.0
PV_LAG = 2
CHUNK = 256
SCAN_TILE = 128
VMEM_LIMIT = 56 * 1024 * 1024


def _cparams(sem):
    return pltpu.CompilerParams(dimension_semantics=sem, vmem_limit_bytes=VMEM_LIMIT)


def _log_sigmoid(x):
    return jnp.minimum(x, 0.0) - jnp.log1p(jnp.exp(-jnp.abs(x)))


def _layer_norm(z, g, b):
    mu = jnp.mean(z, axis=-1, keepdims=True)
    zc = z - mu
    var = jnp.mean(zc * zc, axis=-1, keepdims=True)
    return zc * lax.rsqrt(var + EPS) * g + b


def _row_mod(mod_ref, idx, is_ctx):
    return jnp.where(is_ctx, mod_ref[1, idx:idx + 1, :], mod_ref[0, idx:idx + 1, :])


def _ctx_rows(tile_idx, tm, n_lat):
    rows = tile_idx * tm + lax.broadcasted_iota(jnp.int32, (tm, 1), 0)
    return rows >= n_lat


def _ada_kernel(s_ref, w_ref, b_ref, o_ref):
    s = s_ref[...]
    s = s * jax.nn.sigmoid(s)
    o_ref[...] = jnp.dot(s.astype(BF16), w_ref[...].astype(BF16),
                         preferred_element_type=F32) + b_ref[...]


def _ada_call(s_in, w_ada, b_ada):
    depth, d, d6 = w_ada.shape
    rows = s_in.shape[0]
    tn = d6 // 4
    return pl.pallas_call(
        _ada_kernel,
        grid=(depth, d6 // tn),
        in_specs=[
            pl.BlockSpec((rows, d), lambda l, j: (0, 0)),
            pl.BlockSpec((None, d, tn), lambda l, j: (l, 0, j)),
            pl.BlockSpec((None, 1, tn), lambda l, j: (l, 0, j)),
        ],
        out_specs=pl.BlockSpec((None, rows, tn), lambda l, j: (l, 0, j)),
        out_shape=jax.ShapeDtypeStruct((depth, rows, d6), F32),
        compiler_params=_cparams(("parallel", "parallel")),
    )(s_in, w_ada, b_ada.reshape(depth, 1, d6))


def _inproj_kernel(x_ref, mod_ref, wn_ref, wt_ref, cos_ref, sin_ref, qg_ref, kg_ref,
                   ret_ref, g_ref, lru_ref, qt_ref, k_ref, vt_ref, kn_ref, *, n_lat):
    tm = x_ref.shape[0]
    is_ctx = _ctx_rows(pl.program_id(1), tm, n_lat)
    x = x_ref[...]
    u = (x * (1.0 + _row_mod(mod_ref, 1, is_ctx)) + _row_mod(mod_ref, 0, is_ctx)).astype(BF16)

    pn = jnp.dot(u, wn_ref[...], preferred_element_type=F32)
    rw = g_ref.shape[1]
    ret_ref[:, 0:rw] = pn[:, 0:rw].astype(BF16)
    ret_ref[:, rw:2 * rw] = (pn[:, rw:2 * rw] * HEAD_DIM ** -0.5).astype(BF16)
    ret_ref[:, 2 * rw:3 * rw] = pn[:, 2 * rw:3 * rw].astype(BF16)
    g_ref[...] = pn[:, 3 * rw:4 * rw]
    lru_ref[...] = pn[:, 4 * rw:]

    pt = lax.dot_general(wt_ref[...], u, (((1,), (1,)), ((), ())), preferred_element_type=F32)
    cos = cos_ref[...]
    sin = sin_ref[...]
    half = HEAD_DIM // 2

    def norm_rope(t, gain):
        ms = jnp.mean(t * t, axis=0, keepdims=True)
        t = t * lax.rsqrt(ms + EPS) * gain
        x1, x2 = t[:half], t[half:]
        return jnp.concatenate([x1 * cos - x2 * sin, x1 * sin + x2 * cos], axis=0)

    qw = qt_ref.shape[0]
    kw = k_ref.shape[1]
    qscale = HEAD_DIM ** -0.5 * LOG2E
    for h in range(qw // HEAD_DIM):
        r = h * HEAD_DIM
        qt_ref[r:r + HEAD_DIM, :] = (norm_rope(pt[r:r + HEAD_DIM], qg_ref[...]) * qscale).astype(BF16)
    kt = jnp.concatenate(
        [norm_rope(pt[qw + h * HEAD_DIM:qw + (h + 1) * HEAD_DIM], kg_ref[...])
         for h in range(kw // HEAD_DIM)], axis=0)
    k_ref[...] = kt.T.astype(BF16)
    kf = kt.astype(BF16).astype(F32)
    for h in range(kw // HEAD_DIM):
        kh = kf[h * HEAD_DIM:(h + 1) * HEAD_DIM]
        kn_ref[h:h + 1, :] = jnp.sum(kh * kh, axis=0, keepdims=True)
    ones = jnp.ones((ONES_ROWS, tm), BF16)
    vrows = HEAD_DIM + ONES_ROWS
    for h in range(kw // HEAD_DIM):
        v0 = qw + kw + h * HEAD_DIM
        vt_ref[h * vrows:h * vrows + HEAD_DIM, :] = pt[v0:v0 + HEAD_DIM].astype(BF16)
        vt_ref[h * vrows + HEAD_DIM:(h + 1) * vrows, :] = ones


def _inproj_call(x_all, mod, wn, wt, cos_t, sin_t, qg, kg, n_lat, tm=256):
    b, s, d = x_all.shape
    rw, qw, kw = d // 4, d // 2, d // 8
    vw = (kw // HEAD_DIM) * (HEAD_DIM + ONES_ROWS)
    grid = (b, s // tm)
    tok = lambda shape_last: pl.BlockSpec((None, tm, shape_last), lambda i, j: (i, j, 0))
    tr = lambda rows: pl.BlockSpec((None, rows, tm), lambda i, j: (i, 0, j))
    const = lambda shape: pl.BlockSpec(shape, lambda i, j: tuple(0 for _ in shape))
    return pl.pallas_call(
        functools.partial(_inproj_kernel, n_lat=n_lat),
        grid=grid,
        in_specs=[
            tok(d),
            pl.BlockSpec((None, 2, 6, d), lambda i, j: (i, 0, 0, 0)),
            const(wn.shape), const(wt.shape),
            pl.BlockSpec((HEAD_DIM // 2, tm), lambda i, j: (0, j)),
            pl.BlockSpec((HEAD_DIM // 2, tm), lambda i, j: (0, j)),
            const(qg.shape), const(kg.shape),
        ],
        out_specs=[tok(3 * rw), tok(rw), tok(2 * rw), tr(qw), tok(kw),
                   pl.BlockSpec((None, None, vw, tm), lambda i, j: (i, j, 0, 0)),
                   tr(kw // HEAD_DIM)],
        out_shape=[
            jax.ShapeDtypeStruct((b, s, 3 * rw), BF16),
            jax.ShapeDtypeStruct((b, s, rw), F32),
            jax.ShapeDtypeStruct((b, s, 2 * rw), F32),
            jax.ShapeDtypeStruct((b, qw, s), BF16),
            jax.ShapeDtypeStruct((b, s, kw), BF16),
            jax.ShapeDtypeStruct((b, s // tm, vw, tm), BF16),
            jax.ShapeDtypeStruct((b, kw // HEAD_DIM, s), F32),
        ],
        compiler_params=_cparams(("parallel", "parallel")),
    )(x_all, mod, wn, wt, cos_t, sin_t, qg, kg)


def _ret_kernel(ret_ref, g_ref, dl_ref, o_ref, sf_ref, sb_ref, dm_ref, *, n_lat_chunks):
    c_len = CHUNK
    s_len, rw = g_ref.shape
    nc = s_len // c_len
    ncl = n_lat_chunks
    n_heads = rw // HEAD_DIM

    lg = _log_sigmoid(dl_ref[...])
    lgf, lgb = lg[0:1], lg[1:2]
    pos = lax.broadcasted_iota(jnp.int32, (c_len, 1), 0).astype(F32)
    kwf = jnp.exp(lgf * (c_len - 1.0 - pos))
    kwb = jnp.exp(lgb * pos)
    qwf = jnp.exp(lgf * (pos + 1.0))
    qwb = jnp.exp(lgb * (c_len - pos))
    cdf = jnp.exp(lgf * c_len)
    cdb = jnp.exp(lgb * c_len)
    row_head = lax.broadcasted_iota(jnp.int32, (rw, rw), 0) // HEAD_DIM
    col_head = lax.broadcasted_iota(jnp.int32, (rw, rw), 1) // HEAD_DIM
    same_head = row_head == col_head
    lane_head = lax.broadcasted_iota(jnp.int32, (1, rw), 1) // HEAD_DIM

    ii = lax.broadcasted_iota(jnp.int32, (c_len, c_len), 0)
    jj = lax.broadcasted_iota(jnp.int32, (c_len, c_len), 1)
    dij = (ii - jj).astype(F32)
    for h in range(n_heads):
        lf = lgf[:, h * HEAD_DIM:h * HEAD_DIM + 1]
        lb = lgb[:, h * HEAD_DIM:h * HEAD_DIM + 1]
        dm_ref[h] = jnp.exp(jnp.where(dij >= 0.0, lf * dij, -lb * dij))

    def contrib(c, carry):
        r0 = pl.multiple_of(c * c_len, c_len)
        k = ret_ref[pl.ds(r0, c_len), rw:2 * rw].astype(F32)
        v = ret_ref[pl.ds(r0, c_len), 2 * rw:3 * rw]
        tn = (((0,), (0,)), ((), ()))
        cf = lax.dot_general((k * kwf).astype(BF16), v, tn, preferred_element_type=F32)
        cb = lax.dot_general((k * kwb).astype(BF16), v, tn, preferred_element_type=F32)
        sf_ref[c] = jnp.where(same_head, cf, 0.0)
        sb_ref[c] = jnp.where(same_head, cb, 0.0)
        return carry

    lax.fori_loop(0, nc, contrib, 0)

    def chain(ref, decay, order):
        state = jnp.zeros((rw, rw), F32)
        for c in order:
            u = ref[c]
            ref[c] = state
            state = decay * state + u

    chain(sf_ref, cdf, list(range(ncl, nc)) + list(range(ncl)))
    chain(sb_ref, cdb, list(range(nc - 1, ncl - 1, -1)) + list(range(ncl - 1, -1, -1)))

    ones_blk = jnp.where(same_head, 1.0, 0.0).astype(BF16)

    def outputs(c, carry):
        r0 = pl.multiple_of(c * c_len, c_len)
        q = ret_ref[pl.ds(r0, c_len), 0:rw]
        k = ret_ref[pl.ds(r0, c_len), rw:2 * rw]
        v = ret_ref[pl.ds(r0, c_len), 2 * rw:3 * rw]
        qf = q.astype(F32)
        qi = jnp.concatenate([(qf * qwf).astype(BF16), (qf * qwb).astype(BF16)], axis=1)
        st = jnp.concatenate([sf_ref[c], sb_ref[c]], axis=0).astype(BF16)
        o = jnp.dot(qi, st, preferred_element_type=F32)
        scs = [lax.dot_general(jnp.where(lane_head == h, q, jnp.zeros_like(q)), k,
                               (((1,), (1,)), ((), ())), preferred_element_type=F32)
               for h in range(n_heads)]
        for h in range(n_heads):
            p = (scs[h] * dm_ref[h]).astype(BF16)
            o = o + jnp.where(lane_head == h, jnp.dot(p, v, preferred_element_type=F32), 0.0)
        o2 = o * o
        hi = o2.astype(BF16)
        lo = (o2 - hi.astype(F32)).astype(BF16)
        ms = (jnp.dot(hi, ones_blk, preferred_element_type=F32)
              + jnp.dot(lo, ones_blk, preferred_element_type=F32)) * (1.0 / HEAD_DIM)
        g = g_ref[pl.ds(r0, c_len), :]
        o_ref[pl.ds(r0, c_len), :] = (o * lax.rsqrt(ms + EPS) * (g * jax.nn.sigmoid(g))).astype(BF16)
        return carry

    lax.fori_loop(0, nc, outputs, 0)


def _ret_call(ret_in, g_in, dl_lane, n_lat):
    b, s, rw = g_in.shape
    nc = s // CHUNK
    return pl.pallas_call(
        functools.partial(_ret_kernel, n_lat_chunks=n_lat // CHUNK),
        grid=(b,),
        in_specs=[
            pl.BlockSpec((None, s, 3 * rw), lambda i: (i, 0, 0)),
            pl.BlockSpec((None, s, rw), lambda i: (i, 0, 0)),
            pl.BlockSpec((2, rw), lambda i: (0, 0)),
        ],
        out_specs=pl.BlockSpec((None, s, rw), lambda i: (i, 0, 0)),
        out_shape=jax.ShapeDtypeStruct((b, s, rw), BF16),
        scratch_shapes=[
            pltpu.VMEM((nc, rw, rw), F32),
            pltpu.VMEM((nc, rw, rw), F32),
            pltpu.VMEM((rw // HEAD_DIM, CHUNK, CHUNK), F32),
        ],
        compiler_params=_cparams(("parallel",)),
    )(ret_in, g_in, dl_lane)


def _attn_kernel(qt_ref, k_ref, vt_ref, kn_ref, *rest, group):
    o_ref, rhs_ref, sa_ref, sb_ref, pc_ref, m_ref, acc_ref, out_ref = rest[-8:]
    n_q_heads = qt_ref.shape[0] // HEAD_DIM
    n_chunks = vt_ref.shape[0]
    kc_len = vt_ref.shape[2]
    vrows = vt_ref.shape[1] // (n_q_heads // group)

    kmax2 = jnp.max(kn_ref[...], axis=1, keepdims=True)
    for h in range(n_q_heads):
        qt = qt_ref[h * HEAD_DIM:(h + 1) * HEAD_DIM, :]
        zero = jnp.zeros_like(qt)
        rhs_ref[h] = jnp.concatenate([qt, zero] if h // group == 0 else [zero, qt], axis=0)
        qf = qt.astype(F32)
        qn2 = jnp.sum(qf * qf, axis=0, keepdims=True)
        m_ref[h:h + 1, :] = jnp.sqrt(qn2 * kmax2[h // group:h // group + 1, :]) * SCORE_BOUND_MARGIN
    acc_ref[...] = jnp.zeros(acc_ref.shape, F32)
    bounded = jnp.max(m_ref[...]) <= SCORE_BOUND_CAP

    def scores(c, h):
        c0 = c * kc_len
        if not isinstance(c0, int):
            c0 = pl.multiple_of(c0, kc_len)
        return jnp.dot(k_ref[pl.ds(c0, kc_len), :], rhs_ref[h], preferred_element_type=F32)

    def p_times_v(c, h, p):
        kv = h // group
        return jnp.dot(vt_ref[c, kv * vrows:(kv + 1) * vrows, :], p, preferred_element_type=F32)

    def run_chunks(step):
        for h in range(n_q_heads):
            sa_ref[h] = scores(0, h)

        def pair(i, carry):
            step(2 * i, sa_ref, sb_ref)
            step(2 * i + 1, sb_ref, sa_ref)
            return carry

        lax.fori_loop(0, (n_chunks - 1) // 2, pair, 0)
        if (n_chunks - 1) % 2 == 1:
            step(n_chunks - 2, sa_ref, sb_ref)
            step(n_chunks - 1, sb_ref, None)
        else:
            step(n_chunks - 1, sa_ref, None)

    @pl.when(bounded)
    def _():
        lag = PV_LAG
        pc_ref[...] = jnp.zeros(pc_ref.shape, BF16)

        def step(c, cur_ref, nxt_ref):
            ps = [pc_ref[i] for i in range(lag)]
            c_prev = max(c - 1, 0) if isinstance(c, int) else jnp.maximum(c - 1, 0)
            for h in range(n_q_heads):
                if nxt_ref is not None:
                    nxt_ref[h] = scores(c + 1, h)
                if h < lag:
                    hp = n_q_heads - lag + h
                    acc_ref[hp] += p_times_v(c_prev, hp, ps[h])
                else:
                    acc_ref[h - lag] += p_times_v(c, h - lag, ps[h])
                ps.append(jnp.exp2(cur_ref[h] - m_ref[h:h + 1, :]).astype(BF16))
            for i in range(lag):
                pc_ref[i] = ps[n_q_heads + i]

        run_chunks(step)
        for i in range(lag):
            hp = n_q_heads - lag + i
            acc_ref[hp] += p_times_v(n_chunks - 1, hp, pc_ref[i])

    @pl.when(jnp.logical_not(bounded))
    def _():
        m_ref[...] = jnp.full(m_ref.shape, -jnp.inf, F32)

        def step(c, cur_ref, nxt_ref):
            for h in range(n_q_heads):
                if nxt_ref is not None:
                    nxt_ref[h] = scores(c + 1, h)
                s = cur_ref[h]
                m_old = m_ref[h:h + 1, :]
                m_new = jnp.maximum(m_old, jnp.max(s, axis=0, keepdims=True))
                alpha = jnp.exp2(m_old - m_new)
                m_ref[h:h + 1, :] = m_new
                p = jnp.exp2(s - m_new).astype(BF16)
                acc_ref[h] = alpha * acc_ref[h] + p_times_v(c, h, p)

        run_chunks(step)

    for h in range(n_q_heads):
        a = acc_ref[h]
        out_ref[h * HEAD_DIM:(h + 1) * HEAD_DIM, :] = a[:HEAD_DIM] / a[HEAD_DIM:HEAD_DIM + 1]
    o_ref[...] = out_ref[...].T.astype(BF16)


def _attn_call(qt, k, vt, kn, n_lat, with_ctx, tq=256):
    b, qw, s = qt.shape
    kw = k.shape[2]
    kc_len = vt.shape[3]
    n_ctx = s - n_lat
    n_heads = qw // HEAD_DIM
    n_kv = kw // HEAD_DIM
    group = n_heads // n_kv
    assert n_kv == 2 and n_lat % n_ctx == 0 and n_ctx % tq == 0 and n_ctx % kc_len == 0
    body = functools.partial(_attn_kernel, group=group)
    vw = vt.shape[2]
    scratch = [pltpu.VMEM((n_heads, kw, tq), BF16),
               pltpu.VMEM((n_heads, kc_len, tq), F32), pltpu.VMEM((n_heads, kc_len, tq), F32),
               pltpu.VMEM((PV_LAG, kc_len, tq), BF16),
               pltpu.VMEM((n_heads, tq), F32), pltpu.VMEM((n_heads, vw // n_kv, tq), F32),
               pltpu.VMEM((qw, tq), F32)]
    out_sds = jax.ShapeDtypeStruct((b, s, qw), BF16)
    att = pl.pallas_call(
        body,
        grid=(b, n_lat // tq),
        in_specs=[
            pl.BlockSpec((None, qw, tq), lambda i, j: (i, 0, j)),
            pl.BlockSpec((None, s, kw), lambda i, j: (i, 0, 0)),
            pl.BlockSpec((None, s // kc_len, vw, kc_len), lambda i, j: (i, 0, 0, 0)),
            pl.BlockSpec((None, n_kv, s), lambda i, j: (i, 0, 0)),
        ],
        out_specs=pl.BlockSpec((None, tq, qw), lambda i, j: (i, j, 0)),
        out_shape=out_sds,
        scratch_shapes=scratch,
        compiler_params=_cparams(("parallel", "arbitrary")),
    )(qt, k, vt, kn)
    if not with_ctx:
        return att
    lat_tiles, lat_ctx = n_lat // tq, n_lat // n_ctx
    return pl.pallas_call(
        body,
        grid=(b, n_ctx // tq),
        in_specs=[
            pl.BlockSpec((None, qw, tq), lambda i, j: (i, 0, lat_tiles + j)),
            pl.BlockSpec((None, n_ctx, kw), lambda i, j: (i, lat_ctx, 0)),
            pl.BlockSpec((None, n_ctx // kc_len, vw, kc_len), lambda i, j: (i, lat_ctx, 0, 0)),
            pl.BlockSpec((None, n_kv, n_ctx), lambda i, j: (i, 0, lat_ctx)),
            pl.BlockSpec(memory_space=pl.ANY),
        ],
        out_specs=pl.BlockSpec((None, tq, qw), lambda i, j: (i, lat_tiles + j, 0)),
        out_shape=out_sds,
        input_output_aliases={4: 0},
        scratch_shapes=scratch,
        compiler_params=_cparams(("parallel", "arbitrary")),
    )(qt, k, vt, kn, att)


def _neg_expm1(y, a):
    series = -y * (1.0 + y * (1.0 / 2) * (1.0 + y * (1.0 / 3) * (1.0 + y * (1.0 / 4))))
    return jnp.where(y > -2.0 ** -6, series, 1.0 - a * a)


def _lru_pre_kernel(cur_ref, prev_ref, next_ref, cw_ref, cb_ref, wg_ref, bg_ref, lam_ref,
                    af_ref, df_ref, ab_ref, db_ref, saf_ref, sdf_ref, sab_ref, sdb_ref,
                    *, n_lat_chunks, n_chunks):
    c = pl.program_id(0)
    n_b, c_len, w = cur_ref.shape
    first = jnp.logical_or(c == 0, c == n_lat_chunks)
    last = jnp.logical_or(c == n_lat_chunks - 1, c == n_chunks - 1)
    n_win = c_len + 16
    log_lam = _log_sigmoid(lam_ref[...])

    def one_batch(bi, carry):
        cur = cur_ref[bi]
        prev = jnp.where(first, 0.0, prev_ref[bi])
        nxt = jnp.where(last, 0.0, next_ref[bi])
        win = jnp.concatenate([prev, cur, nxt], axis=0)

        def shifted(off):
            return pltpu.roll(win, (-off) % n_win, axis=0)[8:8 + c_len]

        xr = (shifted(-2) * cw_ref[0:1, :] + shifted(-1) * cw_ref[1:2, :] + cur * cw_ref[2:3, :]
              + shifted(1) * cw_ref[3:4, :] + cb_ref[...])
        gates = jnp.dot(xr.astype(BF16), wg_ref[...], preferred_element_type=F32) + bg_ref[...]
        for d, (a_ref, d_ref) in enumerate(((saf_ref, sdf_ref), (sab_ref, sdb_ref))):
            r = jax.nn.sigmoid(gates[:, 2 * d * w:(2 * d + 1) * w])
            i = jax.nn.sigmoid(gates[:, (2 * d + 1) * w:(2 * d + 2) * w])
            log_a = LRU_C * r * log_lam[d:d + 1, :]
            a = jnp.exp(log_a)
            a_ref[bi] = a
            d_ref[bi] = jnp.sqrt(_neg_expm1(2.0 * log_a, a)) * (i * xr)
        return carry

    lax.fori_loop(0, n_b, one_batch, 0)
    for src, dst in ((saf_ref, af_ref), (sdf_ref, df_ref), (sab_ref, ab_ref), (sdb_ref, db_ref)):
        dst[...] = jnp.swapaxes(src[...], 0, 1)


def _lru_pre_call(lru_in, conv_w, conv_b, wg, bg, lam, n_lat):
    b, s, w2 = lru_in.shape
    w = w2 // 2
    nc = s // CHUNK
    hb = CHUNK // 8
    const = lambda shape: pl.BlockSpec(shape, lambda j: tuple(0 for _ in shape))
    out_spec = pl.BlockSpec((CHUNK, b, w), lambda j: (j, 0, 0))
    out_sds = jax.ShapeDtypeStruct((s, b, w), F32)
    return pl.pallas_call(
        functools.partial(_lru_pre_kernel, n_lat_chunks=n_lat // CHUNK, n_chunks=nc),
        grid=(nc,),
        in_specs=[
            pl.BlockSpec((b, CHUNK, w), lambda j: (0, j, 0)),
            pl.BlockSpec((b, 8, w), lambda j: (0, jnp.maximum(j * hb - 1, 0), 0)),
            pl.BlockSpec((b, 8, w), lambda j: (0, jnp.minimum((j + 1) * hb, s // 8 - 1), 0)),
            const(conv_w.shape), const(conv_b.shape), const(wg.shape), const(bg.shape),
            const(lam.shape),
        ],
        out_specs=[out_spec] * 4,
        out_shape=[out_sds] * 4,
        scratch_shapes=[pltpu.VMEM((b, CHUNK, w), F32)] * 4,
        compiler_params=_cparams(("parallel",)),
    )(lru_in, lru_in, lru_in, conv_w, conv_b, wg, bg, lam)


def _scan_kernel(af_ref, df_ref, ab_ref, db_ref, hf_ref, hb_ref, sf_ref, sb_ref, tf_ref, tb_ref):
    tt = af_ref.shape[0]

    @pl.when(pl.program_id(0) == 0)
    def _():
        sf_ref[...] = jnp.zeros_like(sf_ref)
        sb_ref[...] = jnp.zeros_like(sb_ref)

    def step(t, carry):
        hf, hb = carry
        hf = af_ref[t] * hf + df_ref[t]
        tf_ref[t] = hf
        tb = tt - 1 - t
        hb = ab_ref[tb] * hb + db_ref[tb]
        tb_ref[tb] = hb
        return hf, hb

    hf, hb = lax.fori_loop(0, tt, step, (sf_ref[...], sb_ref[...]), unroll=8)
    sf_ref[...] = hf
    sb_ref[...] = hb
    hf_ref[...] = jnp.swapaxes(tf_ref[...], 0, 1)
    hb_ref[...] = jnp.swapaxes(tb_ref[...], 0, 1)


def _scan_call(af, df, ab, db, n_lat):
    s, b, w = af.shape
    nt = s // SCAN_TILE
    nlt = n_lat // SCAN_TILE
    fwd = pl.BlockSpec((SCAN_TILE, b, w), lambda i: ((i + nlt) % nt, 0, 0))
    bwd = pl.BlockSpec((SCAN_TILE, b, w), lambda i: (nt - 1 - i, 0, 0))
    fwd_out = pl.BlockSpec((b, SCAN_TILE, w), lambda i: (0, (i + nlt) % nt, 0))
    bwd_out = pl.BlockSpec((b, SCAN_TILE, w), lambda i: (0, nt - 1 - i, 0))
    sds = jax.ShapeDtypeStruct((b, s, w), F32)
    return pl.pallas_call(
        _scan_kernel,
        grid=(nt,),
        in_specs=[fwd, fwd, bwd, bwd],
        out_specs=[fwd_out, bwd_out],
        out_shape=[sds, sds],
        scratch_shapes=[pltpu.VMEM((b, w), F32), pltpu.VMEM((b, w), F32),
                        pltpu.VMEM((SCAN_TILE, b, w), F32), pltpu.VMEM((SCAN_TILE, b, w), F32)],
        compiler_params=_cparams(("arbitrary",)),
    )(af, df, ab, db)


def _tail_kernel(x_ref, ret_ref, att_ref, hf_ref, hb_ref, lg_ref, mod_ref, wo_ref, g1_ref, b1_ref,
                 w1_ref, w2_ref, g2_ref, b2_ref, o_ref, x1_ref, u_ref, acc_ref, *, n_lat, alpha):
    tm = x_ref.shape[0]
    kf = pl.program_id(2)
    is_ctx = _ctx_rows(pl.program_id(1), tm, n_lat)

    @pl.when(kf == 0)
    def _():
        rw = ret_ref.shape[1]
        aw = att_ref.shape[1]
        lru = ((hf_ref[...] + hb_ref[...]) * jax.nn.gelu(lg_ref[...])).astype(BF16)
        y = jnp.dot(ret_ref[...], wo_ref[0:rw, :], preferred_element_type=F32)
        y = y + jnp.dot(att_ref[...], wo_ref[rw:rw + aw, :], preferred_element_type=F32)
        y = y + jnp.dot(lru, wo_ref[rw + aw:, :], preferred_element_type=F32)
        x1 = _layer_norm(alpha * x_ref[...] + _row_mod(mod_ref, 2, is_ctx) * y, g1_ref[...], b1_ref[...])
        x1_ref[...] = x1
        u_ref[...] = (x1 * (1.0 + _row_mod(mod_ref, 4, is_ctx)) + _row_mod(mod_ref, 3, is_ctx)).astype(BF16)
        acc_ref[...] = jnp.zeros_like(acc_ref)

    h = jnp.maximum(jnp.dot(u_ref[...], w1_ref[...], preferred_element_type=F32), 0.0)
    acc_ref[...] += jnp.dot((h * h).astype(BF16), w2_ref[...], preferred_element_type=F32)

    @pl.when(kf == pl.num_programs(2) - 1)
    def _():
        z = alpha * x1_ref[...] + _row_mod(mod_ref, 5, is_ctx) * acc_ref[...]
        o_ref[...] = _layer_norm(z, g2_ref[...], b2_ref[...])


def _tail_call(x_all, ret, att, hf, hb, lru_in, mod, w_out, ln1_g, ln1_b, w1, w2, ln2_g, ln2_b,
               n_lat, n_rows, alpha, tm, tf=2048):
    b, _, d = x_all.shape
    rw, aw = ret.shape[2], att.shape[2]
    f = w1.shape[1]
    tok = lambda last: pl.BlockSpec((None, tm, last), lambda i, j, k: (i, j, 0))
    const = lambda shape: pl.BlockSpec(shape, lambda i, j, k: tuple(0 for _ in shape))
    return pl.pallas_call(
        functools.partial(_tail_kernel, n_lat=n_lat, alpha=alpha),
        grid=(b, n_rows // tm, f // tf),
        in_specs=[
            tok(d), tok(rw), tok(aw), tok(rw), tok(rw),
            pl.BlockSpec((None, tm, rw), lambda i, j, k: (i, j, 1)),
            pl.BlockSpec((None, 2, 6, d), lambda i, j, k: (i, 0, 0, 0)),
            const(w_out.shape), const(ln1_g.shape), const(ln1_b.shape),
            pl.BlockSpec((d, tf), lambda i, j, k: (0, k)),
            pl.BlockSpec((tf, d), lambda i, j, k: (k, 0)),
            const(ln2_g.shape), const(ln2_b.shape),
        ],
        out_specs=tok(d),
        out_shape=jax.ShapeDtypeStruct((b, n_rows, d), F32),
        scratch_shapes=[pltpu.VMEM((tm, d), F32), pltpu.VMEM((tm, d), BF16), pltpu.VMEM((tm, d), F32)],
        compiler_params=_cparams(("parallel", "parallel", "arbitrary")),
    )(x_all, ret, att, hf, hb, lru_in, mod, w_out, ln1_g, ln1_b, w1, w2, ln2_g, ln2_b)


def _rope_tables(n_lat, n_ctx):
    rows = n_lat // GRID_W
    row = jnp.repeat(jnp.arange(rows, dtype=F32), GRID_W)
    col = jnp.tile(jnp.arange(GRID_W, dtype=F32), rows)
    n_freq = HEAD_DIM // 4
    inv = ROPE_THETA ** (-jnp.arange(n_freq, dtype=F32) / n_freq)
    ang = jnp.concatenate([row[:, None] * inv, col[:, None] * inv], axis=-1)
    cos = jnp.concatenate([jnp.cos(ang), jnp.ones((n_ctx, HEAD_DIM // 2), F32)], axis=0)
    sin = jnp.concatenate([jnp.sin(ang), jnp.zeros((n_ctx, HEAD_DIM // 2), F32)], axis=0)
    return cos.T, sin.T


def _block_diag(w):
    k, c = w.shape[-3], w.shape[-2]
    eye = jnp.eye(k, dtype=w.dtype)
    bd = jnp.einsum('...kce,kj->...kcje', w, eye)
    return bd.reshape(*w.shape[:-3], k * c, k * c)


def kernel(x, c, ctx, c_ctx, w_ada, b_ada, w_in, ret_decay_logit, attn_q_gain, attn_k_gain,
           lru_conv_w, lru_conv_b, lru_w_a, lru_b_a, lru_w_x, lru_b_x, lru_lambda,
           w_out, ln1_g, ln1_b, w_ff1, w_ff2, ln2_g, ln2_b):
    b, n_lat, d = x.shape
    n_ctx = ctx.shape[1]
    depth = w_in.shape[0]
    s = n_lat + n_ctx
    rw, aw, kw = d // 4, d // 2, d // 8
    alpha = (2.0 * depth) ** 0.25
    assert n_lat % CHUNK == 0 and n_ctx % CHUNK == 0 and d == 16 * HEAD_DIM

    pad = (-(b + 1)) % 8
    s_in = jnp.concatenate([c, c_ctx[None, :], jnp.zeros((pad, d), F32)], axis=0)
    mods = _ada_call(s_in, w_ada, b_ada)
    mod_lat = mods[:, :b].reshape(depth, b, 1, 6, d)
    mod_ctx = jnp.broadcast_to(mods[:, b].reshape(depth, 1, 1, 6, d), (depth, b, 1, 6, d))
    mod_all = jnp.concatenate([mod_lat, mod_ctx], axis=2)

    o_aq = 4 * rw
    o_lx = o_aq + aw + 2 * kw
    wn = jnp.concatenate([w_in[:, :, :o_aq], w_in[:, :, o_lx:]], axis=2).astype(BF16)
    wt = jnp.swapaxes(w_in[:, :, o_aq:o_lx], 1, 2).astype(BF16)
    wg = jnp.concatenate([_block_diag(lru_w_a[:, 0]), _block_diag(lru_w_x[:, 0]),
                          _block_diag(lru_w_a[:, 1]), _block_diag(lru_w_x[:, 1])], axis=-1).astype(BF16)
    bg = jnp.concatenate([lru_b_a[:, 0], lru_b_x[:, 0], lru_b_a[:, 1], lru_b_x[:, 1]],
                         axis=-1)[:, None, :]
    w_out_b = w_out.astype(BF16)
    w1_b = w_ff1.astype(BF16)
    w2_b = w_ff2.astype(BF16)
    dl_lane = jnp.repeat(ret_decay_logit, HEAD_DIM, axis=-1)
    cos_t, sin_t = _rope_tables(n_lat, n_ctx)

    def tail_tile(rows):
        return rows // 8 if rows % 64 == 0 else CHUNK

    xa = jnp.concatenate([x, ctx], axis=1)
    for l in range(depth):
        need_ctx = l < depth - 1
        mod = mod_all[l]
        ret_in, g_in, lru_in, qt, k, vt, kn = _inproj_call(
            xa, mod, wn[l], wt[l], cos_t, sin_t,
            attn_q_gain[l][:, None], attn_k_gain[l][:, None], n_lat)
        ret = _ret_call(ret_in, g_in, dl_lane[l], n_lat)
        att = _attn_call(qt, k, vt, kn, n_lat, need_ctx)
        af, df, ab, db = _lru_pre_call(lru_in, lru_conv_w[l], lru_conv_b[l][None, :], wg[l], bg[l],
                                       lru_lambda[l], n_lat)
        hf, hb = _scan_call(af, df, ab, db, n_lat)
        n_rows = s if need_ctx else n_lat
        xa = _tail_call(xa, ret, att, hf, hb, lru_in, mod,
                        w_out_b[l], ln1_g[l][None, :], ln1_b[l][None, :], w1_b[l], w2_b[l],
                        ln2_g[l][None, :], ln2_b[l][None, :], n_lat, n_rows, alpha, tail_tile(n_rows))
    return xa
```

```python
import functools
import math

import jax
import jax.numpy as jnp
from jax import lax
from jax.experimental import pallas as pl
from jax.experimental.pallas import tpu as pltpu

F32 = jnp.float32
BF16 = jnp.bfloat16

HEAD_DIM = 64
GRID_W = 64
ROPE_THETA = 10000.0
LRU_C = 8.0
LRU_BLOCKS = 4
EPS = 1e-6
LOG2E = 1.4426950408889634

ONES_ROWS = 16
SCORE_BOUND_MARGIN = 1.0 + 2.0 ** -7
SCORE_BOUND_CAP = 60.0
PV_LAG = 2
TAIL_SUB_BLOCKS = 2
CHUNK = 256
SCAN_TILE = 128
VMEM_LIMIT = 56 * 1024 * 1024


def _cparams(sem):
    return pltpu.CompilerParams(dimension_semantics=sem, vmem_limit_bytes=VMEM_LIMIT)


def _log_sigmoid(x):
    return jnp.minimum(x, 0.0) - jnp.log1p(jnp.exp(-jnp.abs(x)))


def _layer_norm(z, g, b):
    mu = jnp.mean(z, axis=-1, keepdims=True)
    zc = z - mu
    var = jnp.mean(zc * zc, axis=-1, keepdims=True)
    return zc * lax.rsqrt(var + EPS) * g + b


def _row_mod(mod_ref, idx, is_ctx):
    return jnp.where(is_ctx, mod_ref[1, idx:idx + 1, :], mod_ref[0, idx:idx + 1, :])


def _ctx_rows(tile_idx, tm, n_lat):
    rows = tile_idx * tm + lax.broadcasted_iota(jnp.int32, (tm, 1), 0)
    return rows >= n_lat


def _ada_kernel(s_ref, w_ref, b_ref, o_ref):
    s = s_ref[...]
    s = s * jax.nn.sigmoid(s)
    o_ref[...] = jnp.dot(s.astype(BF16), w_ref[...].astype(BF16),
                         preferred_element_type=F32) + b_ref[...]


def _ada_call(s_in, w_ada, b_ada):
    depth, d, d6 = w_ada.shape
    rows = s_in.shape[0]
    tn = d6 // 4
    return pl.pallas_call(
        _ada_kernel,
        grid=(depth, d6 // tn),
        in_specs=[
            pl.BlockSpec((rows, d), lambda l, j: (0, 0)),
            pl.BlockSpec((None, d, tn), lambda l, j: (l, 0, j)),
            pl.BlockSpec((None, 1, tn), lambda l, j: (l, 0, j)),
        ],
        out_specs=pl.BlockSpec((None, rows, tn), lambda l, j: (l, 0, j)),
        out_shape=jax.ShapeDtypeStruct((depth, rows, d6), F32),
        compiler_params=_cparams(("parallel", "parallel")),
    )(s_in, w_ada, b_ada.reshape(depth, 1, d6))


def _inproj_kernel(x_ref, mod_ref, wn_ref, wt_ref, cos_ref, sin_ref, qg_ref, kg_ref,
                   ret_ref, g_ref, lru_ref, qt_ref, k_ref, vt_ref, kn_ref, *, n_lat):
    tm = x_ref.shape[0]
    is_ctx = _ctx_rows(pl.program_id(1), tm, n_lat)
    x = x_ref[...]
    u = (x * (1.0 + _row_mod(mod_ref, 1, is_ctx)) + _row_mod(mod_ref, 0, is_ctx)).astype(BF16)

    pt = lax.dot_general(wt_ref[...], u, (((1,), (1,)), ((), ())), preferred_element_type=F32)

    pn = jnp.dot(u, wn_ref[...], preferred_element_type=F32)
    rw = g_ref.shape[1]
    ret_ref[:, 0:rw] = pn[:, 0:rw].astype(BF16)
    ret_ref[:, rw:2 * rw] = (pn[:, rw:2 * rw] * HEAD_DIM ** -0.5).astype(BF16)
    ret_ref[:, 2 * rw:3 * rw] = pn[:, 2 * rw:3 * rw].astype(BF16)
    g_ref[...] = pn[:, 3 * rw:4 * rw]
    lru_ref[...] = pn[:, 4 * rw:]

    cos = cos_ref[...]
    sin = sin_ref[...]
    half = HEAD_DIM // 2

    def norm_rope(t, gain):
        ms = jnp.mean(t * t, axis=0, keepdims=True)
        t = t * lax.rsqrt(ms + EPS) * gain
        x1, x2 = t[:half], t[half:]
        return jnp.concatenate([x1 * cos - x2 * sin, x1 * sin + x2 * cos], axis=0)

    qw = qt_ref.shape[0]
    kw = k_ref.shape[1]
    qscale = HEAD_DIM ** -0.5 * LOG2E
    for h in range(qw // HEAD_DIM):
        r = h * HEAD_DIM
        qt_ref[r:r + HEAD_DIM, :] = (norm_rope(pt[r:r + HEAD_DIM], qg_ref[...]) * qscale).astype(BF16)
    kt = jnp.concatenate(
        [norm_rope(pt[qw + h * HEAD_DIM:qw + (h + 1) * HEAD_DIM], kg_ref[...])
         for h in range(kw // HEAD_DIM)], axis=0)
    k_ref[...] = kt.T.astype(BF16)
    kf = kt.astype(BF16).astype(F32)
    for h in range(kw // HEAD_DIM):
        kh = kf[h * HEAD_DIM:(h + 1) * HEAD_DIM]
        kn_ref[h:h + 1, :] = jnp.sum(kh * kh, axis=0, keepdims=True)
    ones = jnp.ones((ONES_ROWS, tm), BF16)
    vrows = HEAD_DIM + ONES_ROWS
    for h in range(kw // HEAD_DIM):
        v0 = qw + kw + h * HEAD_DIM
        vt_ref[h * vrows:h * vrows + HEAD_DIM, :] = pt[v0:v0 + HEAD_DIM].astype(BF16)
        vt_ref[h * vrows + HEAD_DIM:(h + 1) * vrows, :] = ones


def _inproj_call(x_all, mod, wn, wt, cos_t, sin_t, qg, kg, n_lat, tm=256):
    b, s, d = x_all.shape
    rw, qw, kw = d // 4, d // 2, d // 8
    vw = (kw // HEAD_DIM) * (HEAD_DIM + ONES_ROWS)
    grid = (b, s // tm)
    tok = lambda shape_last: pl.BlockSpec((None, tm, shape_last), lambda i, j: (i, j, 0))
    tr = lambda rows: pl.BlockSpec((None, rows, tm), lambda i, j: (i, 0, j))
    const = lambda shape: pl.BlockSpec(shape, lambda i, j: tuple(0 for _ in shape))
    return pl.pallas_call(
        functools.partial(_inproj_kernel, n_lat=n_lat),
        grid=grid,
        in_specs=[
            tok(d),
            pl.BlockSpec((None, 2, 6, d), lambda i, j: (i, 0, 0, 0)),
            const(wn.shape), const(wt.shape),
            pl.BlockSpec((HEAD_DIM // 2, tm), lambda i, j: (0, j)),
            pl.BlockSpec((HEAD_DIM // 2, tm), lambda i, j: (0, j)),
            const(qg.shape), const(kg.shape),
        ],
        out_specs=[tok(3 * rw), tok(rw), tok(2 * rw), tr(qw), tok(kw),
                   pl.BlockSpec((None, None, vw, tm), lambda i, j: (i, j, 0, 0)),
                   tr(kw // HEAD_DIM)],
        out_shape=[
            jax.ShapeDtypeStruct((b, s, 3 * rw), BF16),
            jax.ShapeDtypeStruct((b, s, rw), F32),
            jax.ShapeDtypeStruct((b, s, 2 * rw), F32),
            jax.ShapeDtypeStruct((b, qw, s), BF16),
            jax.ShapeDtypeStruct((b, s, kw), BF16),
            jax.ShapeDtypeStruct((b, s // tm, vw, tm), BF16),
            jax.ShapeDtypeStruct((b, kw // HEAD_DIM, s), F32),
        ],
        compiler_params=_cparams(("parallel", "parallel")),
    )(x_all, mod, wn, wt, cos_t, sin_t, qg, kg)


def _ret_kernel(ret_ref, g_ref, dl_ref, o_ref, sf_ref, sb_ref, dm_ref, *, n_lat_chunks):
    c_len = CHUNK
    s_len, rw = g_ref.shape
    nc = s_len // c_len
    ncl = n_lat_chunks
    n_heads = rw // HEAD_DIM

    lg = _log_sigmoid(dl_ref[...])
    lgf, lgb = lg[0:1], lg[1:2]
    pos = lax.broadcasted_iota(jnp.int32, (c_len, 1), 0).astype(F32)
    kwf = jnp.exp(lgf * (c_len - 1.0 - pos))
    kwb = jnp.exp(lgb * pos)
    qwf = jnp.exp(lgf * (pos + 1.0))
    qwb = jnp.exp(lgb * (c_len - pos))
    cdf = jnp.exp(lgf * c_len)
    cdb = jnp.exp(lgb * c_len)
    row_head = lax.broadcasted_iota(jnp.int32, (rw, rw), 0) // HEAD_DIM
    col_head = lax.broadcasted_iota(jnp.int32, (rw, rw), 1) // HEAD_DIM
    same_head = row_head == col_head
    lane_head = lax.broadcasted_iota(jnp.int32, (1, rw), 1) // HEAD_DIM

    ii = lax.broadcasted_iota(jnp.int32, (c_len, c_len), 0)
    jj = lax.broadcasted_iota(jnp.int32, (c_len, c_len), 1)
    dij = (ii - jj).astype(F32)
    for h in range(n_heads):
        lf = lgf[:, h * HEAD_DIM:h * HEAD_DIM + 1]
        lb = lgb[:, h * HEAD_DIM:h * HEAD_DIM + 1]
        dm_ref[h] = jnp.exp(jnp.where(dij >= 0.0, lf * dij, -lb * dij))

    def contrib(c, carry):
        r0 = pl.multiple_of(c * c_len, c_len)
        k = ret_ref[pl.ds(r0, c_len), rw:2 * rw].astype(F32)
        v = ret_ref[pl.ds(r0, c_len), 2 * rw:3 * rw]
        tn = (((0,), (0,)), ((), ()))
        cf = lax.dot_general((k * kwf).astype(BF16), v, tn, preferred_element_type=F32)
        cb = lax.dot_general((k * kwb).astype(BF16), v, tn, preferred_element_type=F32)
        sf_ref[c] = jnp.where(same_head, cf, 0.0)
        sb_ref[c] = jnp.where(same_head, cb, 0.0)
        return carry

    lax.fori_loop(0, nc, contrib, 0)

    def chain(ref, decay, order):
        state = jnp.zeros((rw, rw), F32)
        for c in order:
            u = ref[c]
            ref[c] = state
            state = decay * state + u

    chain(sf_ref, cdf, list(range(ncl, nc)) + list(range(ncl)))
    chain(sb_ref, cdb, list(range(nc - 1, ncl - 1, -1)) + list(range(ncl - 1, -1, -1)))

    ones_blk = jnp.where(same_head, 1.0, 0.0).astype(BF16)

    def outputs(c, carry):
        r0 = pl.multiple_of(c * c_len, c_len)
        q = ret_ref[pl.ds(r0, c_len), 0:rw]
        k = ret_ref[pl.ds(r0, c_len), rw:2 * rw]
        v = ret_ref[pl.ds(r0, c_len), 2 * rw:3 * rw]
        qf = q.astype(F32)
        qi = jnp.concatenate([(qf * qwf).astype(BF16), (qf * qwb).astype(BF16)], axis=1)
        st = jnp.concatenate([sf_ref[c], sb_ref[c]], axis=0).astype(BF16)
        o = jnp.dot(qi, st, preferred_element_type=F32)
        scs = [lax.dot_general(jnp.where(lane_head == h, q, jnp.zeros_like(q)), k,
                               (((1,), (1,)), ((), ())), preferred_element_type=F32)
               for h in range(n_heads)]
        for h in range(n_heads):
            p = (scs[h] * dm_ref[h]).astype(BF16)
            o = o + jnp.where(lane_head == h, jnp.dot(p, v, preferred_element_type=F32), 0.0)
        o2 = o * o
        hi = o2.astype(BF16)
        lo = (o2 - hi.astype(F32)).astype(BF16)
        ms = (jnp.dot(hi, ones_blk, preferred_element_type=F32)
              + jnp.dot(lo, ones_blk, preferred_element_type=F32)) * (1.0 / HEAD_DIM)
        g = g_ref[pl.ds(r0, c_len), :]
        o_ref[pl.ds(r0, c_len), :] = (o * lax.rsqrt(ms + EPS) * (g * jax.nn.sigmoid(g))).astype(BF16)
        return carry

    lax.fori_loop(0, nc, outputs, 0)


def _ret_call(ret_in, g_in, dl_lane, n_lat):
    b, s, rw = g_in.shape
    nc = s // CHUNK
    return pl.pallas_call(
        functools.partial(_ret_kernel, n_lat_chunks=n_lat // CHUNK),
        grid=(b,),
        in_specs=[
            pl.BlockSpec((None, s, 3 * rw), lambda i: (i, 0, 0)),
            pl.BlockSpec((None, s, rw), lambda i: (i, 0, 0)),
            pl.BlockSpec((2, rw), lambda i: (0, 0)),
        ],
        out_specs=pl.BlockSpec((None, s, rw), lambda i: (i, 0, 0)),
        out_shape=jax.ShapeDtypeStruct((b, s, rw), BF16),
        scratch_shapes=[
            pltpu.VMEM((nc, rw, rw), F32),
            pltpu.VMEM((nc, rw, rw), F32),
            pltpu.VMEM((rw // HEAD_DIM, CHUNK, CHUNK), F32),
        ],
        compiler_params=_cparams(("parallel",)),
    )(ret_in, g_in, dl_lane)


def _attn_kernel(qt_ref, k_ref, vt_ref, kn_ref, *rest, group):
    o_ref, rhs_ref, sa_ref, sb_ref, pc_ref, m_ref, acc_ref, out_ref = rest[-8:]
    n_q_heads = qt_ref.shape[0] // HEAD_DIM
    n_chunks = vt_ref.shape[0]
    kc_len = vt_ref.shape[2]
    vrows = vt_ref.shape[1] // (n_q_heads // group)

    kmax2 = jnp.max(kn_ref[...], axis=1, keepdims=True)
    for h in range(n_q_heads):
        qt = qt_ref[h * HEAD_DIM:(h + 1) * HEAD_DIM, :]
        zero = jnp.zeros_like(qt)
        rhs_ref[h] = jnp.concatenate([qt, zero] if h // group == 0 else [zero, qt], axis=0)
        qf = qt.astype(F32)
        qn2 = jnp.sum(qf * qf, axis=0, keepdims=True)
        m_ref[h:h + 1, :] = jnp.sqrt(qn2 * kmax2[h // group:h // group + 1, :]) * SCORE_BOUND_MARGIN
    acc_ref[...] = jnp.zeros(acc_ref.shape, F32)
    bounded = jnp.max(m_ref[...]) <= SCORE_BOUND_CAP

    def scores(c, h):
        c0 = c * kc_len
        if not isinstance(c0, int):
            c0 = pl.multiple_of(c0, kc_len)
        return jnp.dot(k_ref[pl.ds(c0, kc_len), :], rhs_ref[h], preferred_element_type=F32)

    def p_times_v(c, h, p):
        kv = h // group
        return jnp.dot(vt_ref[c, kv * vrows:(kv + 1) * vrows, :], p, preferred_element_type=F32)

    def run_chunks(step):
        for h in range(n_q_heads):
            sa_ref[h] = scores(0, h)

        def pair(i, carry):
            step(2 * i, sa_ref, sb_ref)
            step(2 * i + 1, sb_ref, sa_ref)
            return carry

        lax.fori_loop(0, (n_chunks - 1) // 2, pair, 0)
        if (n_chunks - 1) % 2 == 1:
            step(n_chunks - 2, sa_ref, sb_ref)
            step(n_chunks - 1, sb_ref, None)
        else:
            step(n_chunks - 1, sa_ref, None)

    @pl.when(bounded)
    def _():
        lag = PV_LAG
        pc_ref[...] = jnp.zeros(pc_ref.shape, BF16)

        def step(c, cur_ref, nxt_ref):
            ps = [pc_ref[i] for i in range(lag)]
            c_prev = max(c - 1, 0) if isinstance(c, int) else jnp.maximum(c - 1, 0)
            for h in range(n_q_heads):
                if nxt_ref is not None:
                    nxt_ref[h] = scores(c + 1, h)
                if h < lag:
                    hp = n_q_heads - lag + h
                    acc_ref[hp] += p_times_v(c_prev, hp, ps[h])
                else:
                    acc_ref[h - lag] += p_times_v(c, h - lag, ps[h])
                ps.append(jnp.exp2(cur_ref[h] - m_ref[h:h + 1, :]).astype(BF16))
            for i in range(lag):
                pc_ref[i] = ps[n_q_heads + i]

        run_chunks(step)
        for i in range(lag):
            hp = n_q_heads - lag + i
            acc_ref[hp] += p_times_v(n_chunks - 1, hp, pc_ref[i])

    @pl.when(jnp.logical_not(bounded))
    def _():
        m_ref[...] = jnp.full(m_ref.shape, -jnp.inf, F32)

        def step(c, cur_ref, nxt_ref):
            for h in range(n_q_heads):
                if nxt_ref is not None:
                    nxt_ref[h] = scores(c + 1, h)
                s = cur_ref[h]
                m_old = m_ref[h:h + 1, :]
                m_new = jnp.maximum(m_old, jnp.max(s, axis=0, keepdims=True))
                alpha = jnp.exp2(m_old - m_new)
                m_ref[h:h + 1, :] = m_new
                p = jnp.exp2(s - m_new).astype(BF16)
                acc_ref[h] = alpha * acc_ref[h] + p_times_v(c, h, p)

        run_chunks(step)

    for h in range(n_q_heads):
        a = acc_ref[h]
        out_ref[h * HEAD_DIM:(h + 1) * HEAD_DIM, :] = a[:HEAD_DIM] / a[HEAD_DIM:HEAD_DIM + 1]
    o_ref[...] = out_ref[...].T.astype(BF16)


def _attn_call(qt, k, vt, kn, n_lat, with_ctx, tq=256):
    b, qw, s = qt.shape
    kw = k.shape[2]
    kc_len = vt.shape[3]
    n_ctx = s - n_lat
    n_heads = qw // HEAD_DIM
    n_kv = kw // HEAD_DIM
    group = n_heads // n_kv
    assert n_kv == 2 and n_lat % n_ctx == 0 and n_ctx % tq == 0 and n_ctx % kc_len == 0
    body = functools.partial(_attn_kernel, group=group)
    vw = vt.shape[2]
    scratch = [pltpu.VMEM((n_heads, kw, tq), BF16),
               pltpu.VMEM((n_heads, kc_len, tq), F32), pltpu.VMEM((n_heads, kc_len, tq), F32),
               pltpu.VMEM((PV_LAG, kc_len, tq), BF16),
               pltpu.VMEM((n_heads, tq), F32), pltpu.VMEM((n_heads, vw // n_kv, tq), F32),
               pltpu.VMEM((qw, tq), F32)]
    out_sds = jax.ShapeDtypeStruct((b, s, qw), BF16)
    att = pl.pallas_call(
        body,
        grid=(b, n_lat // tq),
        in_specs=[
            pl.BlockSpec((None, qw, tq), lambda i, j: (i, 0, j)),
            pl.BlockSpec((None, s, kw), lambda i, j: (i, 0, 0)),
            pl.BlockSpec((None, s // kc_len, vw, kc_len), lambda i, j: (i, 0, 0, 0)),
            pl.BlockSpec((None, n_kv, s), lambda i, j: (i, 0, 0)),
        ],
        out_specs=pl.BlockSpec((None, tq, qw), lambda i, j: (i, j, 0)),
        out_shape=out_sds,
        scratch_shapes=scratch,
        compiler_params=_cparams(("parallel", "arbitrary")),
    )(qt, k, vt, kn)
    if not with_ctx:
        return att
    lat_tiles, lat_ctx = n_lat // tq, n_lat // n_ctx
    return pl.pallas_call(
        body,
        grid=(b, n_ctx // tq),
        in_specs=[
            pl.BlockSpec((None, qw, tq), lambda i, j: (i, 0, lat_tiles + j)),
            pl.BlockSpec((None, n_ctx, kw), lambda i, j: (i, lat_ctx, 0)),
            pl.BlockSpec((None, n_ctx // kc_len, vw, kc_len), lambda i, j: (i, lat_ctx, 0, 0)),
            pl.BlockSpec((None, n_kv, n_ctx), lambda i, j: (i, 0, lat_ctx)),
            pl.BlockSpec(memory_space=pl.ANY),
        ],
        out_specs=pl.BlockSpec((None, tq, qw), lambda i, j: (i, lat_tiles + j, 0)),
        out_shape=out_sds,
        input_output_aliases={4: 0},
        scratch_shapes=scratch,
        compiler_params=_cparams(("parallel", "arbitrary")),
    )(qt, k, vt, kn, att)


def _neg_expm1(y, a):
    series = -y * (1.0 + y * (1.0 / 2) * (1.0 + y * (1.0 / 3) * (1.0 + y * (1.0 / 4))))
    return jnp.where(y > -2.0 ** -6, series, 1.0 - a * a)


def _lru_pre_kernel(cur_ref, prev_ref, next_ref, cw_ref, cb_ref, wg_ref, bg_ref, lam_ref,
                    af_ref, df_ref, ab_ref, db_ref, saf_ref, sdf_ref, sab_ref, sdb_ref,
                    *, n_lat_chunks, n_chunks):
    c = pl.program_id(0)
    n_b, c_len, w = cur_ref.shape
    first = jnp.logical_or(c == 0, c == n_lat_chunks)
    last = jnp.logical_or(c == n_lat_chunks - 1, c == n_chunks - 1)
    n_win = c_len + 16
    log_lam = _log_sigmoid(lam_ref[...])

    def one_batch(bi, carry):
        cur = cur_ref[bi]
        prev = jnp.where(first, 0.0, prev_ref[bi])
        nxt = jnp.where(last, 0.0, next_ref[bi])
        win = jnp.concatenate([prev, cur, nxt], axis=0)

        def shifted(off):
            return pltpu.roll(win, (-off) % n_win, axis=0)[8:8 + c_len]

        xr = (shifted(-2) * cw_ref[0:1, :] + shifted(-1) * cw_ref[1:2, :] + cur * cw_ref[2:3, :]
              + shifted(1) * cw_ref[3:4, :] + cb_ref[...])
        gates = jnp.dot(xr.astype(BF16), wg_ref[...], preferred_element_type=F32) + bg_ref[...]
        for d, (a_ref, d_ref) in enumerate(((saf_ref, sdf_ref), (sab_ref, sdb_ref))):
            r = jax.nn.sigmoid(gates[:, 2 * d * w:(2 * d + 1) * w])
            i = jax.nn.sigmoid(gates[:, (2 * d + 1) * w:(2 * d + 2) * w])
            log_a = LRU_C * r * log_lam[d:d + 1, :]
            a = jnp.exp(log_a)
            a_ref[bi] = a
            d_ref[bi] = jnp.sqrt(_neg_expm1(2.0 * log_a, a)) * (i * xr)
        return carry

    lax.fori_loop(0, n_b, one_batch, 0)
    for src, dst in ((saf_ref, af_ref), (sdf_ref, df_ref), (sab_ref, ab_ref), (sdb_ref, db_ref)):
        dst[...] = jnp.swapaxes(src[...], 0, 1)


def _lru_pre_call(lru_in, conv_w, conv_b, wg, bg, lam, n_lat):
    b, s, w2 = lru_in.shape
    w = w2 // 2
    nc = s // CHUNK
    hb = CHUNK // 8
    const = lambda shape: pl.BlockSpec(shape, lambda j: tuple(0 for _ in shape))
    out_spec = pl.BlockSpec((CHUNK, b, w), lambda j: (j, 0, 0))
    out_sds = jax.ShapeDtypeStruct((s, b, w), F32)
    return pl.pallas_call(
        functools.partial(_lru_pre_kernel, n_lat_chunks=n_lat // CHUNK, n_chunks=nc),
        grid=(nc,),
        in_specs=[
            pl.BlockSpec((b, CHUNK, w), lambda j: (0, j, 0)),
            pl.BlockSpec((b, 8, w), lambda j: (0, jnp.maximum(j * hb - 1, 0), 0)),
            pl.BlockSpec((b, 8, w), lambda j: (0, jnp.minimum((j + 1) * hb, s // 8 - 1), 0)),
            const(conv_w.shape), const(conv_b.shape), const(wg.shape), const(bg.shape),
            const(lam.shape),
        ],
        out_specs=[out_spec] * 4,
        out_shape=[out_sds] * 4,
        scratch_shapes=[pltpu.VMEM((b, CHUNK, w), F32)] * 4,
        compiler_params=_cparams(("parallel",)),
    )(lru_in, lru_in, lru_in, conv_w, conv_b, wg, bg, lam)


def _scan_kernel(af_ref, df_ref, ab_ref, db_ref, hf_ref, hb_ref, sf_ref, sb_ref, tf_ref, tb_ref):
    tt = af_ref.shape[0]

    @pl.when(pl.program_id(0) == 0)
    def _():
        sf_ref[...] = jnp.zeros_like(sf_ref)
        sb_ref[...] = jnp.zeros_like(sb_ref)

    def step(t, carry):
        hf, hb = carry
        hf = af_ref[t] * hf + df_ref[t]
        tf_ref[t] = hf
        tb = tt - 1 - t
        hb = ab_ref[tb] * hb + db_ref[tb]
        tb_ref[tb] = hb
        return hf, hb

    hf, hb = lax.fori_loop(0, tt, step, (sf_ref[...], sb_ref[...]), unroll=8)
    sf_ref[...] = hf
    sb_ref[...] = hb
    hf_ref[...] = jnp.swapaxes(tf_ref[...], 0, 1)
    hb_ref[...] = jnp.swapaxes(tb_ref[...], 0, 1)


def _scan_call(af, df, ab, db, n_lat):
    s, b, w = af.shape
    nt = s // SCAN_TILE
    nlt = n_lat // SCAN_TILE
    fwd = pl.BlockSpec((SCAN_TILE, b, w), lambda i: ((i + nlt) % nt, 0, 0))
    bwd = pl.BlockSpec((SCAN_TILE, b, w), lambda i: (nt - 1 - i, 0, 0))
    fwd_out = pl.BlockSpec((b, SCAN_TILE, w), lambda i: (0, (i + nlt) % nt, 0))
    bwd_out = pl.BlockSpec((b, SCAN_TILE, w), lambda i: (0, nt - 1 - i, 0))
    sds = jax.ShapeDtypeStruct((b, s, w), F32)
    return pl.pallas_call(
        _scan_kernel,
        grid=(nt,),
        in_specs=[fwd, fwd, bwd, bwd],
        out_specs=[fwd_out, bwd_out],
        out_shape=[sds, sds],
        scratch_shapes=[pltpu.VMEM((b, w), F32), pltpu.VMEM((b, w), F32),
                        pltpu.VMEM((SCAN_TILE, b, w), F32), pltpu.VMEM((SCAN_TILE, b, w), F32)],
        compiler_params=_cparams(("arbitrary",)),
    )(af, df, ab, db)


def _row_splits(tm, n_sub):
    units = tm // 16
    assert tm % 16 == 0 and units >= n_sub
    cuts = [16 * ((units * i) // n_sub) for i in range(n_sub + 1)]
    return list(zip(cuts[:-1], cuts[1:]))


def _tail_kernel(x_ref, ret_ref, att_ref, hf_ref, hb_ref, lg_ref, mod_ref, wo_ref, g1_ref, b1_ref,
                 w1_ref, w2_ref, g2_ref, b2_ref, o_ref, x1_ref, u_ref, acc_ref, *, n_lat, alpha):
    tm = x_ref.shape[0]
    kf = pl.program_id(2)
    nk = pl.num_programs(2)
    subs = _row_splits(tm, TAIL_SUB_BLOCKS)
    rw = ret_ref.shape[1]
    aw = att_ref.shape[1]

    def ctx_rows(r0, r1):
        rows = pl.program_id(1) * tm + r0 + lax.broadcasted_iota(jnp.int32, (r1 - r0, 1), 0)
        return rows >= n_lat

    def mlp_part(r0, r1):
        h = jnp.maximum(jnp.dot(u_ref[r0:r1, :], w1_ref[...], preferred_element_type=F32), 0.0)
        return jnp.dot((h * h).astype(BF16), w2_ref[...], preferred_element_type=F32)

    @pl.when(kf == 0)
    def _():
        for r0, r1 in subs:
            is_ctx = ctx_rows(r0, r1)
            lru = ((hf_ref[r0:r1, :] + hb_ref[r0:r1, :]) * jax.nn.gelu(lg_ref[r0:r1, :])).astype(BF16)
            y = jnp.dot(ret_ref[r0:r1, :], wo_ref[0:rw, :], preferred_element_type=F32)
            y = y + jnp.dot(att_ref[r0:r1, :], wo_ref[rw:rw + aw, :], preferred_element_type=F32)
            y = y + jnp.dot(lru, wo_ref[rw + aw:, :], preferred_element_type=F32)
            x1 = _layer_norm(alpha * x_ref[r0:r1, :] + _row_mod(mod_ref, 2, is_ctx) * y,
                             g1_ref[...], b1_ref[...])
            x1_ref[r0:r1, :] = x1
            u_ref[r0:r1, :] = (x1 * (1.0 + _row_mod(mod_ref, 4, is_ctx))
                               + _row_mod(mod_ref, 3, is_ctx)).astype(BF16)
        for r0, r1 in subs:
            acc_ref[r0:r1, :] = mlp_part(r0, r1)

    @pl.when(jnp.logical_and(kf > 0, kf < nk - 1))
    def _():
        acc_ref[...] += mlp_part(0, tm)

    @pl.when(kf == nk - 1)
    def _():
        for r0, r1 in subs:
            a = acc_ref[r0:r1, :] + mlp_part(r0, r1)
            z = alpha * x1_ref[r0:r1, :] + _row_mod(mod_ref, 5, ctx_rows(r0, r1)) * a
            o_ref[r0:r1, :] = _layer_norm(z, g2_ref[...], b2_ref[...])


def _tail_call(x_all, ret, att, hf, hb, lru_in, mod, w_out, ln1_g, ln1_b, w1, w2, ln2_g, ln2_b,
               n_lat, n_rows, alpha, tm, tf=2048):
    b, _, d = x_all.shape
    rw, aw = ret.shape[2], att.shape[2]
    f = w1.shape[1]
    assert f // tf >= 2
    tok = lambda last: pl.BlockSpec((None, tm, last), lambda i, j, k: (i, j, 0))
    const = lambda shape: pl.BlockSpec(shape, lambda i, j, k: tuple(0 for _ in shape))
    return pl.pallas_call(
        functools.partial(_tail_kernel, n_lat=n_lat, alpha=alpha),
        grid=(b, n_rows // tm, f // tf),
        in_specs=[
            tok(d), tok(rw), tok(aw), tok(rw), tok(rw),
            pl.BlockSpec((None, tm, rw), lambda i, j, k: (i, j, 1)),
            pl.BlockSpec((None, 2, 6, d), lambda i, j, k: (i, 0, 0, 0)),
            const(w_out.shape), const(ln1_g.shape), const(ln1_b.shape),
            pl.BlockSpec((d, tf), lambda i, j, k: (0, k)),
            pl.BlockSpec((tf, d), lambda i, j, k: (k, 0)),
            const(ln2_g.shape), const(ln2_b.shape),
        ],
        out_specs=tok(d),
        out_shape=jax.ShapeDtypeStruct((b, n_rows, d), F32),
        scratch_shapes=[pltpu.VMEM((tm, d), F32), pltpu.VMEM((tm, d), BF16), pltpu.VMEM((tm, d), F32)],
        compiler_params=_cparams(("parallel", "parallel", "arbitrary")),
    )(x_all, ret, att, hf, hb, lru_in, mod, w_out, ln1_g, ln1_b, w1, w2, ln2_g, ln2_b)


def _rope_tables(n_lat, n_ctx):
    rows = n_lat // GRID_W
    row = jnp.repeat(jnp.arange(rows, dtype=F32), GRID_W)
    col = jnp.tile(jnp.arange(GRID_W, dtype=F32), rows)
    n_freq = HEAD_DIM // 4
    inv = ROPE_THETA ** (-jnp.arange(n_freq, dtype=F32) / n_freq)
    ang = jnp.concatenate([row[:, None] * inv, col[:, None] * inv], axis=-1)
    cos = jnp.concatenate([jnp.cos(ang), jnp.ones((n_ctx, HEAD_DIM // 2), F32)], axis=0)
    sin = jnp.concatenate([jnp.sin(ang), jnp.zeros((n_ctx, HEAD_DIM // 2), F32)], axis=0)
    return cos.T, sin.T


def _block_diag(w):
    k, c = w.shape[-3], w.shape[-2]
    eye = jnp.eye(k, dtype=w.dtype)
    bd = jnp.einsum('...kce,kj->...kcje', w, eye)
    return bd.reshape(*w.shape[:-3], k * c, k * c)


def kernel(x, c, ctx, c_ctx, w_ada, b_ada, w_in, ret_decay_logit, attn_q_gain, attn_k_gain,
           lru_conv_w, lru_conv_b, lru_w_a, lru_b_a, lru_w_x, lru_b_x, lru_lambda,
           w_out, ln1_g, ln1_b, w_ff1, w_ff2, ln2_g, ln2_b):
    b, n_lat, d = x.shape
    n_ctx = ctx.shape[1]
    depth = w_in.shape[0]
    s = n_lat + n_ctx
    rw, aw, kw = d // 4, d // 2, d // 8
    alpha = (2.0 * depth) ** 0.25
    assert n_lat % CHUNK == 0 and n_ctx % CHUNK == 0 and d == 16 * HEAD_DIM

    pad = (-(b + 1)) % 8
    s_in = jnp.concatenate([c, c_ctx[None, :], jnp.zeros((pad, d), F32)], axis=0)
    mods = _ada_call(s_in, w_ada, b_ada)
    mod_lat = mods[:, :b].reshape(depth, b, 1, 6, d)
    mod_ctx = jnp.broadcast_to(mods[:, b].reshape(depth, 1, 1, 6, d), (depth, b, 1, 6, d))
    mod_all = jnp.concatenate([mod_lat, mod_ctx], axis=2)

    o_aq = 4 * rw
    o_lx = o_aq + aw + 2 * kw
    wn = jnp.concatenate([w_in[:, :, :o_aq], w_in[:, :, o_lx:]], axis=2).astype(BF16)
    wt = jnp.swapaxes(w_in[:, :, o_aq:o_lx], 1, 2).astype(BF16)
    wg = jnp.concatenate([_block_diag(lru_w_a[:, 0]), _block_diag(lru_w_x[:, 0]),
                          _block_diag(lru_w_a[:, 1]), _block_diag(lru_w_x[:, 1])], axis=-1).astype(BF16)
    bg = jnp.concatenate([lru_b_a[:, 0], lru_b_x[:, 0], lru_b_a[:, 1], lru_b_x[:, 1]],
                         axis=-1)[:, None, :]
    w_out_b = w_out.astype(BF16)
    w1_b = w_ff1.astype(BF16)
    w2_b = w_ff2.astype(BF16)
    dl_lane = jnp.repeat(ret_decay_logit, HEAD_DIM, axis=-1)
    cos_t, sin_t = _rope_tables(n_lat, n_ctx)

    def tail_tile(rows):
        return rows // 8 if rows % 64 == 0 else CHUNK

    xa = jnp.concatenate([x, ctx], axis=1)
    for l in range(depth):
        need_ctx = l < depth - 1
        mod = mod_all[l]
        ret_in, g_in, lru_in, qt, k, vt, kn = _inproj_call(
            xa, mod, wn[l], wt[l], cos_t, sin_t,
            attn_q_gain[l][:, None], attn_k_gain[l][:, None], n_lat)
        ret = _ret_call(ret_in, g_in, dl_lane[l], n_lat)
        att = _attn_call(qt, k, vt, kn, n_lat, need_ctx)
        af, df, ab, db = _lru_pre_call(lru_in, lru_conv_w[l], lru_conv_b[l][None, :], wg[l], bg[l],
                                       lru_lambda[l], n_lat)
        hf, hb = _scan_call(af, df, ab, db, n_lat)
        n_rows = s if need_ctx else n_lat
        xa = _tail_call(xa, ret, att, hf, hb, lru_in, mod,
                        w_out_b[l], ln1_g[l][None, :], ln1_b[l][None, :], w1_b[l], w2_b[l],
                        ln2_g[l][None, :], ln2_b[l][None, :], n_lat, n_rows, alpha, tail_tile(n_rows))
    return xa
```

```python
import functools
import math

import jax
import jax.numpy as jnp
from jax import lax
from jax.experimental import pallas as pl
from jax.experimental.pallas import tpu as pltpu

F32 = jnp.float32
BF16 = jnp.bfloat16

HEAD_DIM = 64
GRID_W = 64
ROPE_THETA = 10000.0
LRU_C = 8.0
LRU_BLOCKS = 4
EPS = 1e-6
LOG2E = 1.4426950408889634

ONES_ROWS = 16
SCORE_BOUND_MARGIN = 1.0 + 2.0 ** -7
SCORE_BOUND_CAP = 60.0
PV_LAG = 2
TAIL_SUB_BLOCKS = 2
CHUNK = 256
SCAN_TILE = 128
VMEM_LIMIT = 56 * 1024 * 1024


def _cparams(sem):
    return pltpu.CompilerParams(dimension_semantics=sem, vmem_limit_bytes=VMEM_LIMIT)


def _log_sigmoid(x):
    return jnp.minimum(x, 0.0) - jnp.log1p(jnp.exp(-jnp.abs(x)))


def _layer_norm(z, g, b):
    mu = jnp.mean(z, axis=-1, keepdims=True)
    zc = z - mu
    var = jnp.mean(zc * zc, axis=-1, keepdims=True)
    return zc * lax.rsqrt(var + EPS) * g + b


def _row_mod(mod_ref, idx, is_ctx):
    return jnp.where(is_ctx, mod_ref[1, idx:idx + 1, :], mod_ref[0, idx:idx + 1, :])


def _ctx_rows(tile_idx, tm, n_lat):
    rows = tile_idx * tm + lax.broadcasted_iota(jnp.int32, (tm, 1), 0)
    return rows >= n_lat


def _ada_kernel(s_ref, w_ref, b_ref, o_ref):
    s = s_ref[...]
    s = s * jax.nn.sigmoid(s)
    o_ref[...] = jnp.dot(s.astype(BF16), w_ref[...].astype(BF16),
                         preferred_element_type=F32) + b_ref[...]


def _ada_call(s_in, w_ada, b_ada):
    depth, d, d6 = w_ada.shape
    rows = s_in.shape[0]
    tn = d6 // 4
    return pl.pallas_call(
        _ada_kernel,
        grid=(depth, d6 // tn),
        in_specs=[
            pl.BlockSpec((rows, d), lambda l, j: (0, 0)),
            pl.BlockSpec((None, d, tn), lambda l, j: (l, 0, j)),
            pl.BlockSpec((None, 1, tn), lambda l, j: (l, 0, j)),
        ],
        out_specs=pl.BlockSpec((None, rows, tn), lambda l, j: (l, 0, j)),
        out_shape=jax.ShapeDtypeStruct((depth, rows, d6), F32),
        compiler_params=_cparams(("parallel", "parallel")),
    )(s_in, w_ada, b_ada.reshape(depth, 1, d6))


def _inproj_kernel(x_ref, mod_ref, wn_ref, wt_ref, cos_ref, sin_ref, qg_ref, kg_ref,
                   ret_ref, g_ref, lru_ref, qt_ref, k_ref, vt_ref, kn_ref, *, n_lat):
    tm = x_ref.shape[0]
    is_ctx = _ctx_rows(pl.program_id(1), tm, n_lat)
    x = x_ref[...]
    u = (x * (1.0 + _row_mod(mod_ref, 1, is_ctx)) + _row_mod(mod_ref, 0, is_ctx)).astype(BF16)

    pt = lax.dot_general(wt_ref[...], u, (((1,), (1,)), ((), ())), preferred_element_type=F32)

    pn = jnp.dot(u, wn_ref[...], preferred_element_type=F32)
    rw = g_ref.shape[1]
    ret_ref[:, 0:rw] = pn[:, 0:rw].astype(BF16)
    ret_ref[:, rw:2 * rw] = (pn[:, rw:2 * rw] * HEAD_DIM ** -0.5).astype(BF16)
    ret_ref[:, 2 * rw:3 * rw] = pn[:, 2 * rw:3 * rw].astype(BF16)
    g_ref[...] = pn[:, 3 * rw:4 * rw]
    lru_ref[...] = pn[:, 4 * rw:]

    cos = cos_ref[...]
    sin = sin_ref[...]
    half = HEAD_DIM // 2

    def norm_rope(t, gain):
        ms = jnp.mean(t * t, axis=0, keepdims=True)
        t = t * lax.rsqrt(ms + EPS) * gain
        x1, x2 = t[:half], t[half:]
        return jnp.concatenate([x1 * cos - x2 * sin, x1 * sin + x2 * cos], axis=0)

    qw = qt_ref.shape[0]
    kw = k_ref.shape[1]
    qscale = HEAD_DIM ** -0.5 * LOG2E
    for h in range(qw // HEAD_DIM):
        r = h * HEAD_DIM
        qt_ref[r:r + HEAD_DIM, :] = (norm_rope(pt[r:r + HEAD_DIM], qg_ref[...]) * qscale).astype(BF16)
    kt = jnp.concatenate(
        [norm_rope(pt[qw + h * HEAD_DIM:qw + (h + 1) * HEAD_DIM], kg_ref[...])
         for h in range(kw // HEAD_DIM)], axis=0)
    k_ref[...] = kt.T.astype(BF16)
    kf = kt.astype(BF16).astype(F32)
    for h in range(kw // HEAD_DIM):
        kh = kf[h * HEAD_DIM:(h + 1) * HEAD_DIM]
        kn_ref[h:h + 1, :] = jnp.sum(kh * kh, axis=0, keepdims=True)
    ones = jnp.ones((ONES_ROWS, tm), BF16)
    vrows = HEAD_DIM + ONES_ROWS
    for h in range(kw // HEAD_DIM):
        v0 = qw + kw + h * HEAD_DIM
        vt_ref[h * vrows:h * vrows + HEAD_DIM, :] = pt[v0:v0 + HEAD_DIM].astype(BF16)
        vt_ref[h * vrows + HEAD_DIM:(h + 1) * vrows, :] = ones


def _inproj_call(x_all, mod, wn, wt, cos_t, sin_t, qg, kg, n_lat, tm=256):
    b, s, d = x_all.shape
    rw, qw, kw = d // 4, d // 2, d // 8
    vw = (kw // HEAD_DIM) * (HEAD_DIM + ONES_ROWS)
    grid = (b, s // tm)
    tok = lambda shape_last: pl.BlockSpec((None, tm, shape_last), lambda i, j: (i, j, 0))
    tr = lambda rows: pl.BlockSpec((None, rows, tm), lambda i, j: (i, 0, j))
    const = lambda shape: pl.BlockSpec(shape, lambda i, j: tuple(0 for _ in shape))
    return pl.pallas_call(
        functools.partial(_inproj_kernel, n_lat=n_lat),
        grid=grid,
        in_specs=[
            tok(d),
            pl.BlockSpec((None, 2, 6, d), lambda i, j: (i, 0, 0, 0)),
            const(wn.shape), const(wt.shape),
            pl.BlockSpec((HEAD_DIM // 2, tm), lambda i, j: (0, j)),
            pl.BlockSpec((HEAD_DIM // 2, tm), lambda i, j: (0, j)),
            const(qg.shape), const(kg.shape),
        ],
        out_specs=[tok(3 * rw), tok(rw), tok(2 * rw), tr(qw), tok(kw),
                   pl.BlockSpec((None, None, vw, tm), lambda i, j: (i, j, 0, 0)),
                   tr(kw // HEAD_DIM)],
        out_shape=[
            jax.ShapeDtypeStruct((b, s, 3 * rw), BF16),
            jax.ShapeDtypeStruct((b, s, rw), F32),
            jax.ShapeDtypeStruct((b, s, 2 * rw), F32),
            jax.ShapeDtypeStruct((b, qw, s), BF16),
            jax.ShapeDtypeStruct((b, s, kw), BF16),
            jax.ShapeDtypeStruct((b, s // tm, vw, tm), BF16),
            jax.ShapeDtypeStruct((b, kw // HEAD_DIM, s), F32),
        ],
        compiler_params=_cparams(("parallel", "parallel")),
    )(x_all, mod, wn, wt, cos_t, sin_t, qg, kg)


def _ret_kernel(ret_ref, g_ref, dl_ref, o_ref, sf_ref, sb_ref, dm_ref, *, n_lat_chunks):
    c_len = CHUNK
    s_len, rw = g_ref.shape
    nc = s_len // c_len
    ncl = n_lat_chunks
    n_heads = rw // HEAD_DIM

    lg = _log_sigmoid(dl_ref[...])
    lgf, lgb = lg[0:1], lg[1:2]
    pos = lax.broadcasted_iota(jnp.int32, (c_len, 1), 0).astype(F32)
    kwf = jnp.exp(lgf * (c_len - 1.0 - pos))
    kwb = jnp.exp(lgb * pos)
    qwf = jnp.exp(lgf * (pos + 1.0))
    qwb = jnp.exp(lgb * (c_len - pos))
    cdf = jnp.exp(lgf * c_len)
    cdb = jnp.exp(lgb * c_len)
    row_head = lax.broadcasted_iota(jnp.int32, (rw, rw), 0) // HEAD_DIM
    col_head = lax.broadcasted_iota(jnp.int32, (rw, rw), 1) // HEAD_DIM
    same_head = row_head == col_head
    lane_head = lax.broadcasted_iota(jnp.int32, (1, rw), 1) // HEAD_DIM

    ii = lax.broadcasted_iota(jnp.int32, (c_len, c_len), 0)
    jj = lax.broadcasted_iota(jnp.int32, (c_len, c_len), 1)
    dij = (ii - jj).astype(F32)
    for h in range(n_heads):
        lf = lgf[:, h * HEAD_DIM:h * HEAD_DIM + 1]
        lb = lgb[:, h * HEAD_DIM:h * HEAD_DIM + 1]
        dm_ref[h] = jnp.exp(jnp.where(dij >= 0.0, lf * dij, -lb * dij))

    def contrib(c, carry):
        r0 = pl.multiple_of(c * c_len, c_len)
        k = ret_ref[pl.ds(r0, c_len), rw:2 * rw].astype(F32)
        v = ret_ref[pl.ds(r0, c_len), 2 * rw:3 * rw]
        tn = (((0,), (0,)), ((), ()))
        cf = lax.dot_general((k * kwf).astype(BF16), v, tn, preferred_element_type=F32)
        cb = lax.dot_general((k * kwb).astype(BF16), v, tn, preferred_element_type=F32)
        sf_ref[c] = jnp.where(same_head, cf, 0.0)
        sb_ref[c] = jnp.where(same_head, cb, 0.0)
        return carry

    lax.fori_loop(0, nc, contrib, 0)

    def chain(ref, decay, order):
        state = jnp.zeros((rw, rw), F32)
        for c in order:
            u = ref[c]
            ref[c] = state
            state = decay * state + u

    chain(sf_ref, cdf, list(range(ncl, nc)) + list(range(ncl)))
    chain(sb_ref, cdb, list(range(nc - 1, ncl - 1, -1)) + list(range(ncl - 1, -1, -1)))

    ones_blk = jnp.where(same_head, 1.0, 0.0).astype(BF16)

    def outputs(c, carry):
        r0 = pl.multiple_of(c * c_len, c_len)
        q = ret_ref[pl.ds(r0, c_len), 0:rw]
        k = ret_ref[pl.ds(r0, c_len), rw:2 * rw]
        v = ret_ref[pl.ds(r0, c_len), 2 * rw:3 * rw]
        qf = q.astype(F32)
        qi = jnp.concatenate([(qf * qwf).astype(BF16), (qf * qwb).astype(BF16)], axis=1)
        st = jnp.concatenate([sf_ref[c], sb_ref[c]], axis=0).astype(BF16)
        o = jnp.dot(qi, st, preferred_element_type=F32)
        scs = [lax.dot_general(jnp.where(lane_head == h, q, jnp.zeros_like(q)), k,
                               (((1,), (1,)), ((), ())), preferred_element_type=F32)
               for h in range(n_heads)]
        for h in range(n_heads):
            p = (scs[h] * dm_ref[h]).astype(BF16)
            o = o + jnp.where(lane_head == h, jnp.dot(p, v, preferred_element_type=F32), 0.0)
        o2 = o * o
        hi = o2.astype(BF16)
        lo = (o2 - hi.astype(F32)).astype(BF16)
        ms = (jnp.dot(hi, ones_blk, preferred_element_type=F32)
              + jnp.dot(lo, ones_blk, preferred_element_type=F32)) * (1.0 / HEAD_DIM)
        g = g_ref[pl.ds(r0, c_len), :]
        o_ref[pl.ds(r0, c_len), :] = (o * lax.rsqrt(ms + EPS) * (g * jax.nn.sigmoid(g))).astype(BF16)
        return carry

    lax.fori_loop(0, nc, outputs, 0)


def _ret_call(ret_in, g_in, dl_lane, n_lat):
    b, s, rw = g_in.shape
    nc = s // CHUNK
    return pl.pallas_call(
        functools.partial(_ret_kernel, n_lat_chunks=n_lat // CHUNK),
        grid=(b,),
        in_specs=[
            pl.BlockSpec((None, s, 3 * rw), lambda i: (i, 0, 0)),
            pl.BlockSpec((None, s, rw), lambda i: (i, 0, 0)),
            pl.BlockSpec((2, rw), lambda i: (0, 0)),
        ],
        out_specs=pl.BlockSpec((None, s, rw), lambda i: (i, 0, 0)),
        out_shape=jax.ShapeDtypeStruct((b, s, rw), BF16),
        scratch_shapes=[
            pltpu.VMEM((nc, rw, rw), F32),
            pltpu.VMEM((nc, rw, rw), F32),
            pltpu.VMEM((rw // HEAD_DIM, CHUNK, CHUNK), F32),
        ],
        compiler_params=_cparams(("parallel",)),
    )(ret_in, g_in, dl_lane)


def _attn_kernel(qt_ref, k_ref, vt_ref, kn_ref, *rest, group):
    o_ref, rhs_ref, sa_ref, sb_ref, pc_ref, m_ref, acc_ref, out_ref = rest[-8:]
    n_q_heads = qt_ref.shape[0] // HEAD_DIM
    n_chunks = vt_ref.shape[0]
    kc_len = vt_ref.shape[2]
    vrows = vt_ref.shape[1] // (n_q_heads // group)

    kmax2 = jnp.max(kn_ref[...], axis=1, keepdims=True)
    for h in range(n_q_heads):
        qt = qt_ref[h * HEAD_DIM:(h + 1) * HEAD_DIM, :]
        zero = jnp.zeros_like(qt)
        rhs_ref[h] = jnp.concatenate([qt, zero] if h // group == 0 else [zero, qt], axis=0)
        qf = qt.astype(F32)
        qn2 = jnp.sum(qf * qf, axis=0, keepdims=True)
        m_ref[h:h + 1, :] = jnp.sqrt(qn2 * kmax2[h // group:h // group + 1, :]) * SCORE_BOUND_MARGIN
    acc_ref[...] = jnp.zeros(acc_ref.shape, F32)
    bounded = jnp.max(m_ref[...]) <= SCORE_BOUND_CAP

    def scores(c, h):
        c0 = c * kc_len
        if not isinstance(c0, int):
            c0 = pl.multiple_of(c0, kc_len)
        return jnp.dot(k_ref[pl.ds(c0, kc_len), :], rhs_ref[h], preferred_element_type=F32)

    def p_times_v(c, h, p):
        kv = h // group
        return jnp.dot(vt_ref[c, kv * vrows:(kv + 1) * vrows, :], p, preferred_element_type=F32)

    def run_chunks(step):
        for h in range(n_q_heads):
            sa_ref[h] = scores(0, h)

        def pair(i, carry):
            step(2 * i, sa_ref, sb_ref)
            step(2 * i + 1, sb_ref, sa_ref)
            return carry

        lax.fori_loop(0, (n_chunks - 1) // 2, pair, 0)
        if (n_chunks - 1) % 2 == 1:
            step(n_chunks - 2, sa_ref, sb_ref)
            step(n_chunks - 1, sb_ref, None)
        else:
            step(n_chunks - 1, sa_ref, None)

    @pl.when(bounded)
    def _():
        lag = PV_LAG
        pc_ref[...] = jnp.zeros(pc_ref.shape, BF16)

        def step(c, cur_ref, nxt_ref):
            ps = [pc_ref[i] for i in range(lag)]
            c_prev = max(c - 1, 0) if isinstance(c, int) else jnp.maximum(c - 1, 0)
            for h in range(n_q_heads):
                if nxt_ref is not None:
                    nxt_ref[h] = scores(c + 1, h)
                if h < lag:
                    hp = n_q_heads - lag + h
                    acc_ref[hp] += p_times_v(c_prev, hp, ps[h])
                else:
                    acc_ref[h - lag] += p_times_v(c, h - lag, ps[h])
                ps.append(jnp.exp2(cur_ref[h] - m_ref[h:h + 1, :]).astype(BF16))
            for i in range(lag):
                pc_ref[i] = ps[n_q_heads + i]

        run_chunks(step)
        for i in range(lag):
            hp = n_q_heads - lag + i
            acc_ref[hp] += p_times_v(n_chunks - 1, hp, pc_ref[i])

    @pl.when(jnp.logical_not(bounded))
    def _():
        m_ref[...] = jnp.full(m_ref.shape, -jnp.inf, F32)

        def step(c, cur_ref, nxt_ref):
            for h in range(n_q_heads):
                if nxt_ref is not None:
                    nxt_ref[h] = scores(c + 1, h)
                s = cur_ref[h]
                m_old = m_ref[h:h + 1, :]
                m_new = jnp.maximum(m_old, jnp.max(s, axis=0, keepdims=True))
                alpha = jnp.exp2(m_old - m_new)
                m_ref[h:h + 1, :] = m_new
                p = jnp.exp2(s - m_new).astype(BF16)
                acc_ref[h] = alpha * acc_ref[h] + p_times_v(c, h, p)

        run_chunks(step)

    for h in range(n_q_heads):
        a = acc_ref[h]
        out_ref[h * HEAD_DIM:(h + 1) * HEAD_DIM, :] = a[:HEAD_DIM] / a[HEAD_DIM:HEAD_DIM + 1]
    o_ref[...] = out_ref[...].T.astype(BF16)


def _attn_call(qt, k, vt, kn, n_lat, with_ctx, tq_lat=512, tq_ctx=256):
    b, qw, s = qt.shape
    kw = k.shape[2]
    vw, kc_len = vt.shape[2], vt.shape[3]
    n_ctx = s - n_lat
    n_heads = qw // HEAD_DIM
    n_kv = kw // HEAD_DIM
    assert n_kv == 2 and n_lat % n_ctx == 0 and n_ctx % kc_len == 0
    tq_lat = min(tq_lat, n_lat)
    tq_ctx = min(tq_ctx, n_ctx)
    assert n_lat % tq_lat == 0 and n_ctx % tq_ctx == 0
    body = functools.partial(_attn_kernel, group=n_heads // n_kv)
    out_sds = jax.ShapeDtypeStruct((b, s, qw), BF16)

    def scratch(tq):
        return [pltpu.VMEM((n_heads, kw, tq), BF16),
                pltpu.VMEM((n_heads, kc_len, tq), F32), pltpu.VMEM((n_heads, kc_len, tq), F32),
                pltpu.VMEM((PV_LAG, kc_len, tq), BF16),
                pltpu.VMEM((n_heads, tq), F32), pltpu.VMEM((n_heads, vw // n_kv, tq), F32),
                pltpu.VMEM((qw, tq), F32)]

    att = pl.pallas_call(
        body,
        grid=(b, n_lat // tq_lat),
        in_specs=[
            pl.BlockSpec((None, qw, tq_lat), lambda i, j: (i, 0, j)),
            pl.BlockSpec((None, s, kw), lambda i, j: (i, 0, 0)),
            pl.BlockSpec((None, s // kc_len, vw, kc_len), lambda i, j: (i, 0, 0, 0)),
            pl.BlockSpec((None, n_kv, s), lambda i, j: (i, 0, 0)),
        ],
        out_specs=pl.BlockSpec((None, tq_lat, qw), lambda i, j: (i, j, 0)),
        out_shape=out_sds,
        scratch_shapes=scratch(tq_lat),
        compiler_params=_cparams(("parallel", "arbitrary")),
    )(qt, k, vt, kn)
    if not with_ctx:
        return att
    lat_tiles, lat_ctx = n_lat // tq_ctx, n_lat // n_ctx
    return pl.pallas_call(
        body,
        grid=(b, n_ctx // tq_ctx),
        in_specs=[
            pl.BlockSpec((None, qw, tq_ctx), lambda i, j: (i, 0, lat_tiles + j)),
            pl.BlockSpec((None, n_ctx, kw), lambda i, j: (i, lat_ctx, 0)),
            pl.BlockSpec((None, n_ctx // kc_len, vw, kc_len), lambda i, j: (i, lat_ctx, 0, 0)),
            pl.BlockSpec((None, n_kv, n_ctx), lambda i, j: (i, 0, lat_ctx)),
            pl.BlockSpec(memory_space=pl.ANY),
        ],
        out_specs=pl.BlockSpec((None, tq_ctx, qw), lambda i, j: (i, lat_tiles + j, 0)),
        out_shape=out_sds,
        input_output_aliases={4: 0},
        scratch_shapes=scratch(tq_ctx),
        compiler_params=_cparams(("parallel", "arbitrary")),
    )(qt, k, vt, kn, att)


def _neg_expm1(y, a):
    series = -y * (1.0 + y * (1.0 / 2) * (1.0 + y * (1.0 / 3) * (1.0 + y * (1.0 / 4))))
    return jnp.where(y > -2.0 ** -6, series, 1.0 - a * a)


def _lru_pre_kernel(cur_ref, prev_ref, next_ref, cw_ref, cb_ref, wg_ref, bg_ref, lam_ref,
                    af_ref, df_ref, ab_ref, db_ref, saf_ref, sdf_ref, sab_ref, sdb_ref,
                    *, n_lat_chunks, n_chunks):
    c = pl.program_id(0)
    n_b, c_len, w = cur_ref.shape
    first = jnp.logical_or(c == 0, c == n_lat_chunks)
    last = jnp.logical_or(c == n_lat_chunks - 1, c == n_chunks - 1)
    n_win = c_len + 16
    log_lam = _log_sigmoid(lam_ref[...])

    def one_batch(bi, carry):
        cur = cur_ref[bi]
        prev = jnp.where(first, 0.0, prev_ref[bi])
        nxt = jnp.where(last, 0.0, next_ref[bi])
        win = jnp.concatenate([prev, cur, nxt], axis=0)

        def shifted(off):
            return pltpu.roll(win, (-off) % n_win, axis=0)[8:8 + c_len]

        xr = (shifted(-2) * cw_ref[0:1, :] + shifted(-1) * cw_ref[1:2, :] + cur * cw_ref[2:3, :]
              + shifted(1) * cw_ref[3:4, :] + cb_ref[...])
        gates = jnp.dot(xr.astype(BF16), wg_ref[...], preferred_element_type=F32) + bg_ref[...]
        for d, (a_ref, d_ref) in enumerate(((saf_ref, sdf_ref), (sab_ref, sdb_ref))):
            r = jax.nn.sigmoid(gates[:, 2 * d * w:(2 * d + 1) * w])
            i = jax.nn.sigmoid(gates[:, (2 * d + 1) * w:(2 * d + 2) * w])
            log_a = LRU_C * r * log_lam[d:d + 1, :]
            a = jnp.exp(log_a)
            a_ref[bi] = a
            d_ref[bi] = jnp.sqrt(_neg_expm1(2.0 * log_a, a)) * (i * xr)
        return carry

    lax.fori_loop(0, n_b, one_batch, 0)
    for src, dst in ((saf_ref, af_ref), (sdf_ref, df_ref), (sab_ref, ab_ref), (sdb_ref, db_ref)):
        dst[...] = jnp.swapaxes(src[...], 0, 1)


def _lru_pre_call(lru_in, conv_w, conv_b, wg, bg, lam, n_lat):
    b, s, w2 = lru_in.shape
    w = w2 // 2
    nc = s // CHUNK
    hb = CHUNK // 8
    const = lambda shape: pl.BlockSpec(shape, lambda j: tuple(0 for _ in shape))
    out_spec = pl.BlockSpec((CHUNK, b, w), lambda j: (j, 0, 0))
    out_sds = jax.ShapeDtypeStruct((s, b, w), F32)
    return pl.pallas_call(
        functools.partial(_lru_pre_kernel, n_lat_chunks=n_lat // CHUNK, n_chunks=nc),
        grid=(nc,),
        in_specs=[
            pl.BlockSpec((b, CHUNK, w), lambda j: (0, j, 0)),
            pl.BlockSpec((b, 8, w), lambda j: (0, jnp.maximum(j * hb - 1, 0), 0)),
            pl.BlockSpec((b, 8, w), lambda j: (0, jnp.minimum((j + 1) * hb, s // 8 - 1), 0)),
            const(conv_w.shape), const(conv_b.shape), const(wg.shape), const(bg.shape),
            const(lam.shape),
        ],
        out_specs=[out_spec] * 4,
        out_shape=[out_sds] * 4,
        scratch_shapes=[pltpu.VMEM((b, CHUNK, w), F32)] * 4,
        compiler_params=_cparams(("parallel",)),
    )(lru_in, lru_in, lru_in, conv_w, conv_b, wg, bg, lam)


def _scan_kernel(af_ref, df_ref, ab_ref, db_ref, hf_ref, hb_ref, sf_ref, sb_ref, tf_ref, tb_ref):
    tt = af_ref.shape[0]

    @pl.when(pl.program_id(0) == 0)
    def _():
        sf_ref[...] = jnp.zeros_like(sf_ref)
        sb_ref[...] = jnp.zeros_like(sb_ref)

    def step(t, carry):
        hf, hb = carry
        hf = af_ref[t] * hf + df_ref[t]
        tf_ref[t] = hf
        tb = tt - 1 - t
        hb = ab_ref[tb] * hb + db_ref[tb]
        tb_ref[tb] = hb
        return hf, hb

    hf, hb = lax.fori_loop(0, tt, step, (sf_ref[...], sb_ref[...]), unroll=8)
    sf_ref[...] = hf
    sb_ref[...] = hb
    hf_ref[...] = jnp.swapaxes(tf_ref[...], 0, 1)
    hb_ref[...] = jnp.swapaxes(tb_ref[...], 0, 1)


def _scan_call(af, df, ab, db, n_lat):
    s, b, w = af.shape
    nt = s // SCAN_TILE
    nlt = n_lat // SCAN_TILE
    fwd = pl.BlockSpec((SCAN_TILE, b, w), lambda i: ((i + nlt) % nt, 0, 0))
    bwd = pl.BlockSpec((SCAN_TILE, b, w), lambda i: (nt - 1 - i, 0, 0))
    fwd_out = pl.BlockSpec((b, SCAN_TILE, w), lambda i: (0, (i + nlt) % nt, 0))
    bwd_out = pl.BlockSpec((b, SCAN_TILE, w), lambda i: (0, nt - 1 - i, 0))
    sds = jax.ShapeDtypeStruct((b, s, w), F32)
    return pl.pallas_call(
        _scan_kernel,
        grid=(nt,),
        in_specs=[fwd, fwd, bwd, bwd],
        out_specs=[fwd_out, bwd_out],
        out_shape=[sds, sds],
        scratch_shapes=[pltpu.VMEM((b, w), F32), pltpu.VMEM((b, w), F32),
                        pltpu.VMEM((SCAN_TILE, b, w), F32), pltpu.VMEM((SCAN_TILE, b, w), F32)],
        compiler_params=_cparams(("arbitrary",)),
    )(af, df, ab, db)


def _row_splits(tm, n_sub):
    units = tm // 16
    assert tm % 16 == 0 and units >= n_sub
    cuts = [16 * ((units * i) // n_sub) for i in range(n_sub + 1)]
    return list(zip(cuts[:-1], cuts[1:]))


def _tail_kernel(x_ref, ret_ref, att_ref, hf_ref, hb_ref, lg_ref, mod_ref, wo_ref, g1_ref, b1_ref,
                 w1_ref, w2_ref, g2_ref, b2_ref, o_ref, x1_ref, u_ref, acc_ref, *, n_lat, alpha):
    tm = x_ref.shape[0]
    kf = pl.program_id(2)
    nk = pl.num_programs(2)
    subs = _row_splits(tm, TAIL_SUB_BLOCKS)
    rw = ret_ref.shape[1]
    aw = att_ref.shape[1]

    def ctx_rows(r0, r1):
        rows = pl.program_id(1) * tm + r0 + lax.broadcasted_iota(jnp.int32, (r1 - r0, 1), 0)
        return rows >= n_lat

    def mlp_part(r0, r1):
        h = jnp.maximum(jnp.dot(u_ref[r0:r1, :], w1_ref[...], preferred_element_type=F32), 0.0)
        return jnp.dot((h * h).astype(BF16), w2_ref[...], preferred_element_type=F32)

    @pl.when(kf == 0)
    def _():
        for r0, r1 in subs:
            is_ctx = ctx_rows(r0, r1)
            lru = ((hf_ref[r0:r1, :] + hb_ref[r0:r1, :]) * jax.nn.gelu(lg_ref[r0:r1, :])).astype(BF16)
            y = jnp.dot(ret_ref[r0:r1, :], wo_ref[0:rw, :], preferred_element_type=F32)
            y = y + jnp.dot(att_ref[r0:r1, :], wo_ref[rw:rw + aw, :], preferred_element_type=F32)
            y = y + jnp.dot(lru, wo_ref[rw + aw:, :], preferred_element_type=F32)
            x1 = _layer_norm(alpha * x_ref[r0:r1, :] + _row_mod(mod_ref, 2, is_ctx) * y,
                             g1_ref[...], b1_ref[...])
            x1_ref[r0:r1, :] = x1
            u_ref[r0:r1, :] = (x1 * (1.0 + _row_mod(mod_ref, 4, is_ctx))
                               + _row_mod(mod_ref, 3, is_ctx)).astype(BF16)
        for r0, r1 in subs:
            acc_ref[r0:r1, :] = mlp_part(r0, r1)

    @pl.when(jnp.logical_and(kf > 0, kf < nk - 1))
    def _():
        acc_ref[...] += mlp_part(0, tm)

    @pl.when(kf == nk - 1)
    def _():
        for r0, r1 in subs:
            a = acc_ref[r0:r1, :] + mlp_part(r0, r1)
            z = alpha * x1_ref[r0:r1, :] + _row_mod(mod_ref, 5, ctx_rows(r0, r1)) * a
            o_ref[r0:r1, :] = _layer_norm(z, g2_ref[...], b2_ref[...])


def _tail_call(x_all, ret, att, hf, hb, lru_in, mod, w_out, ln1_g, ln1_b, w1, w2, ln2_g, ln2_b,
               n_lat, n_rows, alpha, tm, tf=2048):
    b, _, d = x_all.shape
    rw, aw = ret.shape[2], att.shape[2]
    f = w1.shape[1]
    assert f // tf >= 2
    tok = lambda last: pl.BlockSpec((None, tm, last), lambda i, j, k: (i, j, 0))
    const = lambda shape: pl.BlockSpec(shape, lambda i, j, k: tuple(0 for _ in shape))
    return pl.pallas_call(
        functools.partial(_tail_kernel, n_lat=n_lat, alpha=alpha),
        grid=(b, n_rows // tm, f // tf),
        in_specs=[
            tok(d), tok(rw), tok(aw), tok(rw), tok(rw),
            pl.BlockSpec((None, tm, rw), lambda i, j, k: (i, j, 1)),
            pl.BlockSpec((None, 2, 6, d), lambda i, j, k: (i, 0, 0, 0)),
            const(w_out.shape), const(ln1_g.shape), const(ln1_b.shape),
            pl.BlockSpec((d, tf), lambda i, j, k: (0, k)),
            pl.BlockSpec((tf, d), lambda i, j, k: (k, 0)),
            const(ln2_g.shape), const(ln2_b.shape),
        ],
        out_specs=tok(d),
        out_shape=jax.ShapeDtypeStruct((b, n_rows, d), F32),
        scratch_shapes=[pltpu.VMEM((tm, d), F32), pltpu.VMEM((tm, d), BF16), pltpu.VMEM((tm, d), F32)],
        compiler_params=_cparams(("parallel", "parallel", "arbitrary")),
    )(x_all, ret, att, hf, hb, lru_in, mod, w_out, ln1_g, ln1_b, w1, w2, ln2_g, ln2_b)


def _rope_tables(n_lat, n_ctx):
    rows = n_lat // GRID_W
    row = jnp.repeat(jnp.arange(rows, dtype=F32), GRID_W)
    col = jnp.tile(jnp.arange(GRID_W, dtype=F32), rows)
    n_freq = HEAD_DIM // 4
    inv = ROPE_THETA ** (-jnp.arange(n_freq, dtype=F32) / n_freq)
    ang = jnp.concatenate([row[:, None] * inv, col[:, None] * inv], axis=-1)
    cos = jnp.concatenate([jnp.cos(ang), jnp.ones((n_ctx, HEAD_DIM // 2), F32)], axis=0)
    sin = jnp.concatenate([jnp.sin(ang), jnp.zeros((n_ctx, HEAD_DIM // 2), F32)], axis=0)
    return cos.T, sin.T


def _block_diag(w):
    k, c = w.shape[-3], w.shape[-2]
    eye = jnp.eye(k, dtype=w.dtype)
    bd = jnp.einsum('...kce,kj->...kcje', w, eye)
    return bd.reshape(*w.shape[:-3], k * c, k * c)


def kernel(x, c, ctx, c_ctx, w_ada, b_ada, w_in, ret_decay_logit, attn_q_gain, attn_k_gain,
           lru_conv_w, lru_conv_b, lru_w_a, lru_b_a, lru_w_x, lru_b_x, lru_lambda,
           w_out, ln1_g, ln1_b, w_ff1, w_ff2, ln2_g, ln2_b):
    b, n_lat, d = x.shape
    n_ctx = ctx.shape[1]
    depth = w_in.shape[0]
    s = n_lat + n_ctx
    rw, aw, kw = d // 4, d // 2, d // 8
    alpha = (2.0 * depth) ** 0.25
    assert n_lat % CHUNK == 0 and n_ctx % CHUNK == 0 and d == 16 * HEAD_DIM

    pad = (-(b + 1)) % 8
    s_in = jnp.concatenate([c, c_ctx[None, :], jnp.zeros((pad, d), F32)], axis=0)
    mods = _ada_call(s_in, w_ada, b_ada)
    mod_lat = mods[:, :b].reshape(depth, b, 1, 6, d)
    mod_ctx = jnp.broadcast_to(mods[:, b].reshape(depth, 1, 1, 6, d), (depth, b, 1, 6, d))
    mod_all = jnp.concatenate([mod_lat, mod_ctx], axis=2)

    o_aq = 4 * rw
    o_lx = o_aq + aw + 2 * kw
    wn = jnp.concatenate([w_in[:, :, :o_aq], w_in[:, :, o_lx:]], axis=2).astype(BF16)
    wt = jnp.swapaxes(w_in[:, :, o_aq:o_lx], 1, 2).astype(BF16)
    wg = jnp.concatenate([_block_diag(lru_w_a[:, 0]), _block_diag(lru_w_x[:, 0]),
                          _block_diag(lru_w_a[:, 1]), _block_diag(lru_w_x[:, 1])], axis=-1).astype(BF16)
    bg = jnp.concatenate([lru_b_a[:, 0], lru_b_x[:, 0], lru_b_a[:, 1], lru_b_x[:, 1]],
                         axis=-1)[:, None, :]
    w_out_b = w_out.astype(BF16)
    w1_b = w_ff1.astype(BF16)
    w2_b = w_ff2.astype(BF16)
    dl_lane = jnp.repeat(ret_decay_logit, HEAD_DIM, axis=-1)
    cos_t, sin_t = _rope_tables(n_lat, n_ctx)

    def tail_tile(rows):
        return rows // 8 if rows % 64 == 0 else CHUNK

    xa = jnp.concatenate([x, ctx], axis=1)
    for l in range(depth):
        need_ctx = l < depth - 1
        mod = mod_all[l]
        ret_in, g_in, lru_in, qt, k, vt, kn = _inproj_call(
            xa, mod, wn[l], wt[l], cos_t, sin_t,
            attn_q_gain[l][:, None], attn_k_gain[l][:, None], n_lat)
        ret = _ret_call(ret_in, g_in, dl_lane[l], n_lat)
        att = _attn_call(qt, k, vt, kn, n_lat, need_ctx)
        af, df, ab, db = _lru_pre_call(lru_in, lru_conv_w[l], lru_conv_b[l][None, :], wg[l], bg[l],
                                       lru_lambda[l], n_lat)
        hf, hb = _scan_call(af, df, ab, db, n_lat)
        n_rows = s if need_ctx else n_lat
        xa = _tail_call(xa, ret, att, hf, hb, lru_in, mod,
                        w_out_b[l], ln1_g[l][None, :], ln1_b[l][None, :], w1_b[l], w2_b[l],
                        ln2_g[l][None, :], ln2_b[l][None, :], n_lat, n_rows, alpha, tail_tile(n_rows))
    return xa
```

```python
import functools
import math

import jax
import jax.numpy as jnp
from jax import lax
from jax.experimental import pallas as pl
from jax.experimental.pallas import tpu as pltpu

F32 = jnp.float32
BF16 = jnp.bfloat16

HEAD_DIM = 64
GRID_W = 64
ROPE_THETA = 10000.0
LRU_C = 8.0
LRU_BLOCKS = 4
EPS = 1e-6
LOG2E = 1.4426950408889634

ONES_ROWS = 16
SCORE_BOUND_MARGIN = 1.0 + 2.0 ** -7
SCORE_BOUND_CAP = 60.0
PV_LAG = 2
TAIL_SUB_BLOCKS = 2
CHUNK = 256
SCAN_TILE = 128
LRU_SUB_TOKENS = 32
VMEM_LIMIT = 56 * 1024 * 1024


def _cparams(sem):
    return pltpu.CompilerParams(dimension_semantics=sem, vmem_limit_bytes=VMEM_LIMIT)


def _log_sigmoid(x):
    return jnp.minimum(x, 0.0) - jnp.log1p(jnp.exp(-jnp.abs(x)))


def _layer_norm(z, g, b):
    mu = jnp.mean(z, axis=-1, keepdims=True)
    zc = z - mu
    var = jnp.mean(zc * zc, axis=-1, keepdims=True)
    return zc * lax.rsqrt(var + EPS) * g + b


def _row_mod(mod_ref, idx, is_ctx):
    return jnp.where(is_ctx, mod_ref[1, idx:idx + 1, :], mod_ref[0, idx:idx + 1, :])


def _ctx_rows(tile_idx, tm, n_lat):
    rows = tile_idx * tm + lax.broadcasted_iota(jnp.int32, (tm, 1), 0)
    return rows >= n_lat


def _ada_kernel(s_ref, w_ref, b_ref, o_ref):
    s = s_ref[...]
    s = s * jax.nn.sigmoid(s)
    o_ref[...] = jnp.dot(s.astype(BF16), w_ref[...].astype(BF16),
                         preferred_element_type=F32) + b_ref[...]


def _ada_call(s_in, w_ada, b_ada):
    depth, d, d6 = w_ada.shape
    rows = s_in.shape[0]
    tn = d6 // 4
    return pl.pallas_call(
        _ada_kernel,
        grid=(depth, d6 // tn),
        in_specs=[
            pl.BlockSpec((rows, d), lambda l, j: (0, 0)),
            pl.BlockSpec((None, d, tn), lambda l, j: (l, 0, j)),
            pl.BlockSpec((None, 1, tn), lambda l, j: (l, 0, j)),
        ],
        out_specs=pl.BlockSpec((None, rows, tn), lambda l, j: (l, 0, j)),
        out_shape=jax.ShapeDtypeStruct((depth, rows, d6), F32),
        compiler_params=_cparams(("parallel", "parallel")),
    )(s_in, w_ada, b_ada.reshape(depth, 1, d6))


def _inproj_kernel(x_ref, mod_ref, wn_ref, wt_ref, cos_ref, sin_ref, qg_ref, kg_ref,
                   ret_ref, g_ref, lru_ref, qt_ref, k_ref, vt_ref, kn_ref, *, n_lat):
    tm = x_ref.shape[0]
    is_ctx = _ctx_rows(pl.program_id(1), tm, n_lat)
    x = x_ref[...]
    u = (x * (1.0 + _row_mod(mod_ref, 1, is_ctx)) + _row_mod(mod_ref, 0, is_ctx)).astype(BF16)

    pt = lax.dot_general(wt_ref[...], u, (((1,), (1,)), ((), ())), preferred_element_type=F32)

    pn = jnp.dot(u, wn_ref[...], preferred_element_type=F32)
    rw = g_ref.shape[1]
    ret_ref[:, 0:rw] = pn[:, 0:rw].astype(BF16)
    ret_ref[:, rw:2 * rw] = (pn[:, rw:2 * rw] * HEAD_DIM ** -0.5).astype(BF16)
    ret_ref[:, 2 * rw:3 * rw] = pn[:, 2 * rw:3 * rw].astype(BF16)
    g_ref[...] = pn[:, 3 * rw:4 * rw]
    lru_ref[...] = pn[:, 4 * rw:]

    cos = cos_ref[...]
    sin = sin_ref[...]
    half = HEAD_DIM // 2

    def norm_rope(t, gain):
        ms = jnp.mean(t * t, axis=0, keepdims=True)
        t = t * lax.rsqrt(ms + EPS) * gain
        x1, x2 = t[:half], t[half:]
        return jnp.concatenate([x1 * cos - x2 * sin, x1 * sin + x2 * cos], axis=0)

    qw = qt_ref.shape[0]
    kw = k_ref.shape[1]
    qscale = HEAD_DIM ** -0.5 * LOG2E
    for h in range(qw // HEAD_DIM):
        r = h * HEAD_DIM
        qt_ref[r:r + HEAD_DIM, :] = (norm_rope(pt[r:r + HEAD_DIM], qg_ref[...]) * qscale).astype(BF16)
    kt = jnp.concatenate(
        [norm_rope(pt[qw + h * HEAD_DIM:qw + (h + 1) * HEAD_DIM], kg_ref[...])
         for h in range(kw // HEAD_DIM)], axis=0)
    k_ref[...] = kt.T.astype(BF16)
    kf = kt.astype(BF16).astype(F32)
    for h in range(kw // HEAD_DIM):
        kh = kf[h * HEAD_DIM:(h + 1) * HEAD_DIM]
        kn_ref[h:h + 1, :] = jnp.sum(kh * kh, axis=0, keepdims=True)
    ones = jnp.ones((ONES_ROWS, tm), BF16)
    vrows = HEAD_DIM + ONES_ROWS
    for h in range(kw // HEAD_DIM):
        v0 = qw + kw + h * HEAD_DIM
        vt_ref[h * vrows:h * vrows + HEAD_DIM, :] = pt[v0:v0 + HEAD_DIM].astype(BF16)
        vt_ref[h * vrows + HEAD_DIM:(h + 1) * vrows, :] = ones


def _inproj_call(x_all, mod, wn, wt, cos_t, sin_t, qg, kg, n_lat, tm=256):
    b, s, d = x_all.shape
    rw, qw, kw = d // 4, d // 2, d // 8
    vw = (kw // HEAD_DIM) * (HEAD_DIM + ONES_ROWS)
    grid = (b, s // tm)
    tok = lambda shape_last: pl.BlockSpec((None, tm, shape_last), lambda i, j: (i, j, 0))
    tr = lambda rows: pl.BlockSpec((None, rows, tm), lambda i, j: (i, 0, j))
    const = lambda shape: pl.BlockSpec(shape, lambda i, j: tuple(0 for _ in shape))
    return pl.pallas_call(
        functools.partial(_inproj_kernel, n_lat=n_lat),
        grid=grid,
        in_specs=[
            tok(d),
            pl.BlockSpec((None, 2, 6, d), lambda i, j: (i, 0, 0, 0)),
            const(wn.shape), const(wt.shape),
            pl.BlockSpec((HEAD_DIM // 2, tm), lambda i, j: (0, j)),
            pl.BlockSpec((HEAD_DIM // 2, tm), lambda i, j: (0, j)),
            const(qg.shape), const(kg.shape),
        ],
        out_specs=[tok(3 * rw), tok(rw), tok(2 * rw), tr(qw), tok(kw),
                   pl.BlockSpec((None, None, vw, tm), lambda i, j: (i, j, 0, 0)),
                   tr(kw // HEAD_DIM)],
        out_shape=[
            jax.ShapeDtypeStruct((b, s, 3 * rw), BF16),
            jax.ShapeDtypeStruct((b, s, rw), F32),
            jax.ShapeDtypeStruct((b, s, 2 * rw), F32),
            jax.ShapeDtypeStruct((b, qw, s), BF16),
            jax.ShapeDtypeStruct((b, s, kw), BF16),
            jax.ShapeDtypeStruct((b, s // tm, vw, tm), BF16),
            jax.ShapeDtypeStruct((b, kw // HEAD_DIM, s), F32),
        ],
        compiler_params=_cparams(("parallel", "parallel")),
    )(x_all, mod, wn, wt, cos_t, sin_t, qg, kg)


def _ret_kernel(ret_ref, g_ref, dl_ref, o_ref, sf_ref, sb_ref, dm_ref, *, n_lat_chunks):
    c_len = CHUNK
    s_len, rw = g_ref.shape
    nc = s_len // c_len
    ncl = n_lat_chunks
    n_heads = rw // HEAD_DIM

    lg = _log_sigmoid(dl_ref[...])
    lgf, lgb = lg[0:1], lg[1:2]
    pos = lax.broadcasted_iota(jnp.int32, (c_len, 1), 0).astype(F32)
    kwf = jnp.exp(lgf * (c_len - 1.0 - pos))
    kwb = jnp.exp(lgb * pos)
    qwf = jnp.exp(lgf * (pos + 1.0))
    qwb = jnp.exp(lgb * (c_len - pos))
    cdf = jnp.exp(lgf * c_len)
    cdb = jnp.exp(lgb * c_len)
    row_head = lax.broadcasted_iota(jnp.int32, (rw, rw), 0) // HEAD_DIM
    col_head = lax.broadcasted_iota(jnp.int32, (rw, rw), 1) // HEAD_DIM
    same_head = row_head == col_head
    lane_head = lax.broadcasted_iota(jnp.int32, (1, rw), 1) // HEAD_DIM

    ii = lax.broadcasted_iota(jnp.int32, (c_len, c_len), 0)
    jj = lax.broadcasted_iota(jnp.int32, (c_len, c_len), 1)
    dij = (ii - jj).astype(F32)
    for h in range(n_heads):
        lf = lgf[:, h * HEAD_DIM:h * HEAD_DIM + 1]
        lb = lgb[:, h * HEAD_DIM:h * HEAD_DIM + 1]
        dm_ref[h] = jnp.exp(jnp.where(dij >= 0.0, lf * dij, -lb * dij))

    def contrib(c, carry):
        r0 = pl.multiple_of(c * c_len, c_len)
        k = ret_ref[pl.ds(r0, c_len), rw:2 * rw].astype(F32)
        v = ret_ref[pl.ds(r0, c_len), 2 * rw:3 * rw]
        tn = (((0,), (0,)), ((), ()))
        cf = lax.dot_general((k * kwf).astype(BF16), v, tn, preferred_element_type=F32)
        cb = lax.dot_general((k * kwb).astype(BF16), v, tn, preferred_element_type=F32)
        sf_ref[c] = jnp.where(same_head, cf, 0.0)
        sb_ref[c] = jnp.where(same_head, cb, 0.0)
        return carry

    lax.fori_loop(0, nc, contrib, 0)

    def chain(ref, decay, order):
        state = jnp.zeros((rw, rw), F32)
        for c in order:
            u = ref[c]
            ref[c] = state
            state = decay * state + u

    chain(sf_ref, cdf, list(range(ncl, nc)) + list(range(ncl)))
    chain(sb_ref, cdb, list(range(nc - 1, ncl - 1, -1)) + list(range(ncl - 1, -1, -1)))

    ones_blk = jnp.where(same_head, 1.0, 0.0).astype(BF16)

    def outputs(chunks):
        rows = [c * c_len if isinstance(c, int) else pl.multiple_of(c * c_len, c_len) for c in chunks]
        qs =[ret_ref[pl.ds(r0, c_len), 0:rw] for r0 in rows]
        ks = [ret_ref[pl.ds(r0, c_len), rw:2 * rw] for r0 in rows]
        vs = [ret_ref[pl.ds(r0, c_len), 2 * rw:3 * rw] for r0 in rows]
        os_, scs = [], []
        for c, q, k in zip(chunks, qs, ks):
            qf = q.astype(F32)
            qi = jnp.concatenate([(qf * qwf).astype(BF16), (qf * qwb).astype(BF16)], axis=1)
            st = jnp.concatenate([sf_ref[c], sb_ref[c]], axis=0).astype(BF16)
            os_.append(jnp.dot(qi, st, preferred_element_type=F32))
            scs.append([lax.dot_general(jnp.where(lane_head == h, q, jnp.zeros_like(q)), k,
                                        (((1,), (1,)), ((), ())), preferred_element_type=F32)
                        for h in range(n_heads)])
        for i, v in enumerate(vs):
            for h in range(n_heads):
                p = (scs[i][h] * dm_ref[h]).astype(BF16)
                os_[i] = os_[i] + jnp.where(lane_head == h, jnp.dot(p, v, preferred_element_type=F32), 0.0)
        mss = []
        for o in os_:
            o2 = o * o
            hi = o2.astype(BF16)
            lo = (o2 - hi.astype(F32)).astype(BF16)
            mss.append((jnp.dot(hi, ones_blk, preferred_element_type=F32)
                        + jnp.dot(lo, ones_blk, preferred_element_type=F32)) * (1.0 / HEAD_DIM))
        for r0, o, ms in zip(rows, os_, mss):
            g = g_ref[pl.ds(r0, c_len), :]
            o_ref[pl.ds(r0, c_len), :] = (o * lax.rsqrt(ms + EPS) * (g * jax.nn.sigmoid(g))).astype(BF16)

    def output_pair(i, carry):
        outputs([2 * i, 2 * i + 1])
        return carry

    lax.fori_loop(0, nc // 2, output_pair, 0)
    if nc % 2:
        outputs([nc - 1])


def _ret_call(ret_in, g_in, dl_lane, n_lat):
    b, s, rw = g_in.shape
    nc = s // CHUNK
    return pl.pallas_call(
        functools.partial(_ret_kernel, n_lat_chunks=n_lat // CHUNK),
        grid=(b,),
        in_specs=[
            pl.BlockSpec((None, s, 3 * rw), lambda i: (i, 0, 0)),
            pl.BlockSpec((None, s, rw), lambda i: (i, 0, 0)),
            pl.BlockSpec((2, rw), lambda i: (0, 0)),
        ],
        out_specs=pl.BlockSpec((None, s, rw), lambda i: (i, 0, 0)),
        out_shape=jax.ShapeDtypeStruct((b, s, rw), BF16),
        scratch_shapes=[
            pltpu.VMEM((nc, rw, rw), F32),
            pltpu.VMEM((nc, rw, rw), F32),
            pltpu.VMEM((rw // HEAD_DIM, CHUNK, CHUNK), F32),
        ],
        compiler_params=_cparams(("parallel",)),
    )(ret_in, g_in, dl_lane)


def _attn_kernel(qt_ref, k_ref, vt_ref, kn_ref, *rest, group):
    o_ref, rhs_ref, sa_ref, sb_ref, pc_ref, m_ref, acc_ref, out_ref = rest[-8:]
    n_q_heads = qt_ref.shape[0] // HEAD_DIM
    n_chunks = vt_ref.shape[0]
    kc_len = vt_ref.shape[2]
    vrows = vt_ref.shape[1] // (n_q_heads // group)

    kmax2 = jnp.max(kn_ref[...], axis=1, keepdims=True)
    for h in range(n_q_heads):
        qt = qt_ref[h * HEAD_DIM:(h + 1) * HEAD_DIM, :]
        zero = jnp.zeros_like(qt)
        rhs_ref[h] = jnp.concatenate([qt, zero] if h // group == 0 else [zero, qt], axis=0)
        qf = qt.astype(F32)
        qn2 = jnp.sum(qf * qf, axis=0, keepdims=True)
        m_ref[h:h + 1, :] = jnp.sqrt(qn2 * kmax2[h // group:h // group + 1, :]) * SCORE_BOUND_MARGIN
    acc_ref[...] = jnp.zeros(acc_ref.shape, F32)
    bounded = jnp.max(m_ref[...]) <= SCORE_BOUND_CAP

    def scores(c, h):
        c0 = c * kc_len
        if not isinstance(c0, int):
            c0 = pl.multiple_of(c0, kc_len)
        return jnp.dot(k_ref[pl.ds(c0, kc_len), :], rhs_ref[h], preferred_element_type=F32)

    def p_times_v(c, h, p):
        kv = h // group
        return jnp.dot(vt_ref[c, kv * vrows:(kv + 1) * vrows, :], p, preferred_element_type=F32)

    def run_chunks(step):
        for h in range(n_q_heads):
            sa_ref[h] = scores(0, h)

        def pair(i, carry):
            step(2 * i, sa_ref, sb_ref)
            step(2 * i + 1, sb_ref, sa_ref)
            return carry

        lax.fori_loop(0, (n_chunks - 1) // 2, pair, 0)
        if (n_chunks - 1) % 2 == 1:
            step(n_chunks - 2, sa_ref, sb_ref)
            step(n_chunks - 1, sb_ref, None)
        else:
            step(n_chunks - 1, sa_ref, None)

    @pl.when(bounded)
    def _():
        lag = PV_LAG
        pc_ref[...] = jnp.zeros(pc_ref.shape, BF16)

        def step(c, cur_ref, nxt_ref):
            ps = [pc_ref[i] for i in range(lag)]
            c_prev = max(c - 1, 0) if isinstance(c, int) else jnp.maximum(c - 1, 0)
            for h in range(n_q_heads):
                if nxt_ref is not None:
                    nxt_ref[h] = scores(c + 1, h)
                if h < lag:
                    hp = n_q_heads - lag + h
                    acc_ref[hp] += p_times_v(c_prev, hp, ps[h])
                else:
                    acc_ref[h - lag] += p_times_v(c, h - lag, ps[h])
                ps.append(jnp.exp2(cur_ref[h] - m_ref[h:h + 1, :]).astype(BF16))
            for i in range(lag):
                pc_ref[i] = ps[n_q_heads + i]

        run_chunks(step)
        for i in range(lag):
            hp = n_q_heads - lag + i
            acc_ref[hp] += p_times_v(n_chunks - 1, hp, pc_ref[i])

    @pl.when(jnp.logical_not(bounded))
    def _():
        m_ref[...] = jnp.full(m_ref.shape, -jnp.inf, F32)

        def step(c, cur_ref, nxt_ref):
            for h in range(n_q_heads):
                if nxt_ref is not None:
                    nxt_ref[h] = scores(c + 1, h)
                s = cur_ref[h]
                m_old = m_ref[h:h + 1, :]
                m_new = jnp.maximum(m_old, jnp.max(s, axis=0, keepdims=True))
                alpha = jnp.exp2(m_old - m_new)
                m_ref[h:h + 1, :] = m_new
                p = jnp.exp2(s - m_new).astype(BF16)
                acc_ref[h] = alpha * acc_ref[h] + p_times_v(c, h, p)

        run_chunks(step)

    for h in range(n_q_heads):
        a = acc_ref[h]
        out_ref[h * HEAD_DIM:(h + 1) * HEAD_DIM, :] = a[:HEAD_DIM] / a[HEAD_DIM:HEAD_DIM + 1]
    o_ref[...] = out_ref[...].T.astype(BF16)


def _attn_call(qt, k, vt, kn, n_lat, with_ctx, tq_lat=512, tq_ctx=256):
    b, qw, s = qt.shape
    kw = k.shape[2]
    vw, kc_len = vt.shape[2], vt.shape[3]
    n_ctx = s - n_lat
    n_heads = qw // HEAD_DIM
    n_kv = kw // HEAD_DIM
    assert n_kv == 2 and n_lat % n_ctx == 0 and n_ctx % kc_len == 0
    tq_lat = min(tq_lat, n_lat)
    tq_ctx = min(tq_ctx, n_ctx)
    assert n_lat % tq_lat == 0 and n_ctx % tq_ctx == 0
    body = functools.partial(_attn_kernel, group=n_heads // n_kv)
    out_sds = jax.ShapeDtypeStruct((b, s, qw), BF16)

    def scratch(tq):
        return [pltpu.VMEM((n_heads, kw, tq), BF16),
                pltpu.VMEM((n_heads, kc_len, tq), F32), pltpu.VMEM((n_heads, kc_len, tq), F32),
                pltpu.VMEM((PV_LAG, kc_len, tq), BF16),
                pltpu.VMEM((n_heads, tq), F32), pltpu.VMEM((n_heads, vw // n_kv, tq), F32),
                pltpu.VMEM((qw, tq), F32)]

    att = pl.pallas_call(
        body,
        grid=(b, n_lat // tq_lat),
        in_specs=[
            pl.BlockSpec((None, qw, tq_lat), lambda i, j: (i, 0, j)),
            pl.BlockSpec((None, s, kw), lambda i, j: (i, 0, 0)),
            pl.BlockSpec((None, s // kc_len, vw, kc_len), lambda i, j: (i, 0, 0, 0)),
            pl.BlockSpec((None, n_kv, s), lambda i, j: (i, 0, 0)),
        ],
        out_specs=pl.BlockSpec((None, tq_lat, qw), lambda i, j: (i, j, 0)),
        out_shape=out_sds,
        scratch_shapes=scratch(tq_lat),
        compiler_params=_cparams(("parallel", "arbitrary")),
    )(qt, k, vt, kn)
    if not with_ctx:
        return att
    lat_tiles, lat_ctx = n_lat // tq_ctx, n_lat // n_ctx
    return pl.pallas_call(
        body,
        grid=(b, n_ctx // tq_ctx),
        in_specs=[
            pl.BlockSpec((None, qw, tq_ctx), lambda i, j: (i, 0, lat_tiles + j)),
            pl.BlockSpec((None, n_ctx, kw), lambda i, j: (i, lat_ctx, 0)),
            pl.BlockSpec((None, n_ctx // kc_len, vw, kc_len), lambda i, j: (i, lat_ctx, 0, 0)),
            pl.BlockSpec((None, n_kv, n_ctx), lambda i, j: (i, 0, lat_ctx)),
            pl.BlockSpec(memory_space=pl.ANY),
        ],
        out_specs=pl.BlockSpec((None, tq_ctx, qw), lambda i, j: (i, lat_tiles + j, 0)),
        out_shape=out_sds,
        input_output_aliases={4: 0},
        scratch_shapes=scratch(tq_ctx),
        compiler_params=_cparams(("parallel", "arbitrary")),
    )(qt, k, vt, kn, att)


def _lru_pre_kernel(cur_ref, prev_ref, next_ref, cw_ref, cb_ref, wg_ref, bg_ref, lam_ref,
                    af_ref, df_ref, ab_ref, db_ref, xs_ref, *, n_lat_chunks, n_chunks):
    c = pl.program_id(0)
    n_b, c_len, w = cur_ref.shape
    first = jnp.logical_or(c == 0, c == n_lat_chunks)
    last = jnp.logical_or(c == n_lat_chunks - 1, c == n_chunks - 1)
    xs_ref[0:8] = jnp.where(first, 0.0, jnp.swapaxes(prev_ref[...], 0, 1))
    xs_ref[8:8 + c_len] = jnp.swapaxes(cur_ref[...], 0, 1)
    xs_ref[8 + c_len:] = jnp.where(last, 0.0, jnp.swapaxes(next_ref[...], 0, 1))
    half_c_lam = (0.5 * LRU_C) * _log_sigmoid(lam_ref[...])
    ts = LRU_SUB_TOKENS

    def sub_chunk(i, carry):
        t0 = pl.multiple_of(i * ts, ts)

        def shifted(off):
            return xs_ref[pl.ds(8 + t0 + off, ts)]

        xr = (shifted(-2) * cw_ref[0:1, :] + shifted(-1) * cw_ref[1:2, :] + shifted(0) * cw_ref[2:3, :]
              + shifted(1) * cw_ref[3:4, :] + cb_ref[...]).reshape(ts * n_b, w)
        half_gates = jnp.dot(xr.astype(BF16), wg_ref[...], preferred_element_type=F32) + bg_ref[...]
        half_xr = 0.5 * xr
        for d, (a_ref, d_ref) in enumerate(((af_ref, df_ref), (ab_ref, db_ref))):
            t_r = jnp.tanh(half_gates[:, 2 * d * w:(2 * d + 1) * w])
            t_i = jnp.tanh(half_gates[:, (2 * d + 1) * w:(2 * d + 2) * w])
            half_c = half_c_lam[d:d + 1, :]
            log_a = half_c * t_r + half_c
            a = jnp.exp(log_a)
            om = (1.0 + a * a) * jnp.tanh(-log_a)
            drive = om * lax.rsqrt(jnp.maximum(om, 1e-30)) * (half_xr * t_i + half_xr)
            a_ref[pl.ds(t0, ts)] = a.reshape(ts, n_b, w)
            d_ref[pl.ds(t0, ts)] = drive.reshape(ts, n_b, w)
        return carry

    lax.fori_loop(0, c_len // ts, sub_chunk, 0)


def _lru_pre_call(lru_in, conv_w, conv_b, wg, bg, lam, n_lat):
    b, s, w2 = lru_in.shape
    w = w2 // 2
    nc = s // CHUNK
    hb = CHUNK // 8
    const = lambda shape: pl.BlockSpec(shape, lambda j: tuple(0 for _ in shape))
    out_spec = pl.BlockSpec((CHUNK, b, w), lambda j: (j, 0, 0))
    out_sds = jax.ShapeDtypeStruct((s, b, w), F32)
    return pl.pallas_call(
        functools.partial(_lru_pre_kernel, n_lat_chunks=n_lat // CHUNK, n_chunks=nc),
        grid=(nc,),
        in_specs=[
            pl.BlockSpec((b, CHUNK, w), lambda j: (0, j, 0)),
            pl.BlockSpec((b, 8, w), lambda j: (0, jnp.maximum(j * hb - 1, 0), 0)),
            pl.BlockSpec((b, 8, w), lambda j: (0, jnp.minimum((j + 1) * hb, s // 8 - 1), 0)),
            const(conv_w.shape), const(conv_b.shape), const(wg.shape), const(bg.shape),
            const(lam.shape),
        ],
        out_specs=[out_spec] * 4,
        out_shape=[out_sds] * 4,
        scratch_shapes=[pltpu.VMEM((CHUNK + 16, b, w), F32)],
        compiler_params=_cparams(("parallel",)),
    )(lru_in, lru_in, lru_in, conv_w, conv_b, wg, bg, lam)


def _scan_kernel(af_ref, df_ref, ab_ref, db_ref, hf_ref, hb_ref, sf_ref, sb_ref, tf_ref, tb_ref):
    tt = af_ref.shape[0]

    @pl.when(pl.program_id(0) == 0)
    def _():
        sf_ref[...] = jnp.zeros_like(sf_ref)
        sb_ref[...] = jnp.zeros_like(sb_ref)

    def step(t, carry):
        hf, hb = carry
        hf = af_ref[t] * hf + df_ref[t]
        tf_ref[t] = hf
        tb = tt - 1 - t
        hb = ab_ref[tb] * hb + db_ref[tb]
        tb_ref[tb] = hb
        return hf, hb

    hf, hb = lax.fori_loop(0, tt, step, (sf_ref[...], sb_ref[...]), unroll=8)
    sf_ref[...] = hf
    sb_ref[...] = hb
    hf_ref[...] = jnp.swapaxes(tf_ref[...], 0, 1)
    hb_ref[...] = jnp.swapaxes(tb_ref[...], 0, 1)


def _scan_call(af, df, ab, db, n_lat):
    s, b, w = af.shape
    nt = s // SCAN_TILE
    nlt = n_lat // SCAN_TILE
    fwd = pl.BlockSpec((SCAN_TILE, b, w), lambda i: ((i + nlt) % nt, 0, 0))
    bwd = pl.BlockSpec((SCAN_TILE, b, w), lambda i: (nt - 1 - i, 0, 0))
    fwd_out = pl.BlockSpec((b, SCAN_TILE, w), lambda i: (0, (i + nlt) % nt, 0))
    bwd_out = pl.BlockSpec((b, SCAN_TILE, w), lambda i: (0, nt - 1 - i, 0))
    sds = jax.ShapeDtypeStruct((b, s, w), F32)
    return pl.pallas_call(
        _scan_kernel,
        grid=(nt,),
        in_specs=[fwd, fwd, bwd, bwd],
        out_specs=[fwd_out, bwd_out],
        out_shape=[sds, sds],
        scratch_shapes=[pltpu.VMEM((b, w), F32), pltpu.VMEM((b, w), F32),
                        pltpu.VMEM((SCAN_TILE, b, w), F32), pltpu.VMEM((SCAN_TILE, b, w), F32)],
        compiler_params=_cparams(("arbitrary",)),
    )(af, df, ab, db)


def _row_splits(tm, n_sub):
    units = tm // 16
    assert tm % 16 == 0 and units >= n_sub
    cuts = [16 * ((units * i) // n_sub) for i in range(n_sub + 1)]
    return list(zip(cuts[:-1], cuts[1:]))


def _tail_kernel(x_ref, ret_ref, att_ref, hf_ref, hb_ref, lg_ref, mod_ref, wo_ref, g1_ref, b1_ref,
                 w1_ref, w2_ref, g2_ref, b2_ref, o_ref, x1_ref, u_ref, acc_ref, *, n_lat, alpha):
    tm = x_ref.shape[0]
    kf = pl.program_id(2)
    nk = pl.num_programs(2)
    subs = _row_splits(tm, TAIL_SUB_BLOCKS)
    rw = ret_ref.shape[1]
    aw = att_ref.shape[1]

    def ctx_rows(r0, r1):
        rows = pl.program_id(1) * tm + r0 + lax.broadcasted_iota(jnp.int32, (r1 - r0, 1), 0)
        return rows >= n_lat

    def mlp_part(r0, r1):
        h = jnp.maximum(jnp.dot(u_ref[r0:r1, :], w1_ref[...], preferred_element_type=F32), 0.0)
        return jnp.dot((h * h).astype(BF16), w2_ref[...], preferred_element_type=F32)

    @pl.when(kf == 0)
    def _():
        for r0, r1 in subs:
            is_ctx = ctx_rows(r0, r1)
            lru = ((hf_ref[r0:r1, :] + hb_ref[r0:r1, :]) * jax.nn.gelu(lg_ref[r0:r1, :])).astype(BF16)
            y = jnp.dot(ret_ref[r0:r1, :], wo_ref[0:rw, :], preferred_element_type=F32)
            y = y + jnp.dot(att_ref[r0:r1, :], wo_ref[rw:rw + aw, :], preferred_element_type=F32)
            y = y + jnp.dot(lru, wo_ref[rw + aw:, :], preferred_element_type=F32)
            x1 = _layer_norm(alpha * x_ref[r0:r1, :] + _row_mod(mod_ref, 2, is_ctx) * y,
                             g1_ref[...], b1_ref[...])
            x1_ref[r0:r1, :] = x1
            u_ref[r0:r1, :] = (x1 * (1.0 + _row_mod(mod_ref, 4, is_ctx))
                               + _row_mod(mod_ref, 3, is_ctx)).astype(BF16)
        for r0, r1 in subs:
            acc_ref[r0:r1, :] = mlp_part(r0, r1)

    @pl.when(jnp.logical_and(kf > 0, kf < nk - 1))
    def _():
        acc_ref[...] += mlp_part(0, tm)

    @pl.when(kf == nk - 1)
    def _():
        for r0, r1 in subs:
            a = acc_ref[r0:r1, :] + mlp_part(r0, r1)
            z = alpha * x1_ref[r0:r1, :] + _row_mod(mod_ref, 5, ctx_rows(r0, r1)) * a
            o_ref[r0:r1, :] = _layer_norm(z, g2_ref[...], b2_ref[...])


def _tail_call(x_all, ret, att, hf, hb, lru_in, mod, w_out, ln1_g, ln1_b, w1, w2, ln2_g, ln2_b,
               n_lat, n_rows, alpha, tm, tf=2048):
    b, _, d = x_all.shape
    rw, aw = ret.shape[2], att.shape[2]
    f = w1.shape[1]
    assert f // tf >= 2
    tok = lambda last: pl.BlockSpec((None, tm, last), lambda i, j, k: (i, j, 0))
    const = lambda shape: pl.BlockSpec(shape, lambda i, j, k: tuple(0 for _ in shape))
    return pl.pallas_call(
        functools.partial(_tail_kernel, n_lat=n_lat, alpha=alpha),
        grid=(b, n_rows // tm, f // tf),
        in_specs=[
            tok(d), tok(rw), tok(aw), tok(rw), tok(rw),
            pl.BlockSpec((None, tm, rw), lambda i, j, k: (i, j, 1)),
            pl.BlockSpec((None, 2, 6, d), lambda i, j, k: (i, 0, 0, 0)),
            const(w_out.shape), const(ln1_g.shape), const(ln1_b.shape),
            pl.BlockSpec((d, tf), lambda i, j, k: (0, k)),
            pl.BlockSpec((tf, d), lambda i, j, k: (k, 0)),
            const(ln2_g.shape), const(ln2_b.shape),
        ],
        out_specs=tok(d),
        out_shape=jax.ShapeDtypeStruct((b, n_rows, d), F32),
        scratch_shapes=[pltpu.VMEM((tm, d), F32), pltpu.VMEM((tm, d), BF16), pltpu.VMEM((tm, d), F32)],
        compiler_params=_cparams(("parallel", "parallel", "arbitrary")),
    )(x_all, ret, att, hf, hb, lru_in, mod, w_out, ln1_g, ln1_b, w1, w2, ln2_g, ln2_b)


def _rope_tables(n_lat, n_ctx):
    rows = n_lat // GRID_W
    row = jnp.repeat(jnp.arange(rows, dtype=F32), GRID_W)
    col = jnp.tile(jnp.arange(GRID_W, dtype=F32), rows)
    n_freq = HEAD_DIM // 4
    inv = ROPE_THETA ** (-jnp.arange(n_freq, dtype=F32) / n_freq)
    ang = jnp.concatenate([row[:, None] * inv, col[:, None] * inv], axis=-1)
    cos = jnp.concatenate([jnp.cos(ang), jnp.ones((n_ctx, HEAD_DIM // 2), F32)], axis=0)
    sin = jnp.concatenate([jnp.sin(ang), jnp.zeros((n_ctx, HEAD_DIM // 2), F32)], axis=0)
    return cos.T, sin.T


def _block_diag(w):
    k, c = w.shape[-3], w.shape[-2]
    eye = jnp.eye(k, dtype=w.dtype)
    bd = jnp.einsum('...kce,kj->...kcje', w, eye)
    return bd.reshape(*w.shape[:-3], k * c, k * c)


def kernel(x, c, ctx, c_ctx, w_ada, b_ada, w_in, ret_decay_logit, attn_q_gain, attn_k_gain,
           lru_conv_w, lru_conv_b, lru_w_a, lru_b_a, lru_w_x, lru_b_x, lru_lambda,
           w_out, ln1_g, ln1_b, w_ff1, w_ff2, ln2_g, ln2_b):
    b, n_lat, d = x.shape
    n_ctx = ctx.shape[1]
    depth = w_in.shape[0]
    s = n_lat + n_ctx
    rw, aw, kw = d // 4, d // 2, d // 8
    alpha = (2.0 * depth) ** 0.25
    assert n_lat % CHUNK == 0 and n_ctx % CHUNK == 0 and d == 16 * HEAD_DIM

    pad = (-(b + 1)) % 8
    s_in = jnp.concatenate([c, c_ctx[None, :], jnp.zeros((pad, d), F32)], axis=0)
    mods = _ada_call(s_in, w_ada, b_ada)
    mod_lat = mods[:, :b].reshape(depth, b, 1, 6, d)
    mod_ctx = jnp.broadcast_to(mods[:, b].reshape(depth, 1, 1, 6, d), (depth, b, 1, 6, d))
    mod_all = jnp.concatenate([mod_lat, mod_ctx], axis=2)

    o_aq = 4 * rw
    o_lx = o_aq + aw + 2 * kw
    wn = jnp.concatenate([w_in[:, :, :o_aq], w_in[:, :, o_lx:]], axis=2).astype(BF16)
    wt = jnp.swapaxes(w_in[:, :, o_aq:o_lx], 1, 2).astype(BF16)
    wg = (0.5 * jnp.concatenate([_block_diag(lru_w_a[:, 0]), _block_diag(lru_w_x[:, 0]),
                                 _block_diag(lru_w_a[:, 1]), _block_diag(lru_w_x[:, 1])],
                                axis=-1)).astype(BF16)
    bg = 0.5 * jnp.concatenate([lru_b_a[:, 0], lru_b_x[:, 0], lru_b_a[:, 1], lru_b_x[:, 1]],
                               axis=-1)[:, None, :]
    w_out_b = w_out.astype(BF16)
    w1_b = w_ff1.astype(BF16)
    w2_b = w_ff2.astype(BF16)
    dl_lane = jnp.repeat(ret_decay_logit, HEAD_DIM, axis=-1)
    cos_t, sin_t = _rope_tables(n_lat, n_ctx)

    def tail_tile(rows):
        return rows // 8 if rows % 64 == 0 else CHUNK

    xa = jnp.concatenate([x, ctx], axis=1)
    for l in range(depth):
        need_ctx = l < depth - 1
        mod = mod_all[l]
        ret_in, g_in, lru_in, qt, k, vt, kn = _inproj_call(
            xa, mod, wn[l], wt[l], cos_t, sin_t,
            attn_q_gain[l][:, None], attn_k_gain[l][:, None], n_lat)
        ret = _ret_call(ret_in, g_in, dl_lane[l], n_lat)
        att = _attn_call(qt, k, vt, kn, n_lat, need_ctx)
        af, df, ab, db = _lru_pre_call(lru_in, lru_conv_w[l], lru_conv_b[l][None, :], wg[l], bg[l],
                                       lru_lambda[l], n_lat)
        hf, hb = _scan_call(af, df, ab, db, n_lat)
        n_rows = s if need_ctx else n_lat
        xa = _tail_call(xa, ret, att, hf, hb, lru_in, mod,
                        w_out_b[l], ln1_g[l][None, :], ln1_b[l][None, :], w1_b[l], w2_b[l],
                        ln2_g[l][None, :], ln2_b[l][None, :], n_lat, n_rows, alpha, tail_tile(n_rows))
    return xa
```

```python
import functools
import math

import jax
import jax.numpy as jnp
from jax import lax
from jax.experimental import pallas as pl
from jax.experimental.pallas import tpu as pltpu

F32 = jnp.float32
BF16 = jnp.bfloat16

HEAD_DIM = 64
GRID_W = 64
ROPE_THETA = 10000.0
LRU_C = 8.0
LRU_BLOCKS = 4
EPS = 1e-6
LOG2E = 1.4426950408889634

ONES_ROWS = 16
SCORE_BOUND_MARGIN = 1.0 + 2.0 ** -7
SCORE_BOUND_CAP = 60.0
PV_LAG = 2
TAIL_SUB_BLOCKS = 2
CHUNK = 256
SCAN_TILE = 128
LRU_SUB_TOKENS = 32
VMEM_LIMIT = 56 * 1024 * 1024


def _cparams(sem):
    return pltpu.CompilerParams(dimension_semantics=sem, vmem_limit_bytes=VMEM_LIMIT)


def _log_sigmoid(x):
    return jnp.minimum(x, 0.0) - jnp.log1p(jnp.exp(-jnp.abs(x)))


def _layer_norm(z, g, b):
    mu = jnp.mean(z, axis=-1, keepdims=True)
    zc = z - mu
    var = jnp.mean(zc * zc, axis=-1, keepdims=True)
    return zc * lax.rsqrt(var + EPS) * g + b


def _row_mod(mod_ref, idx, is_ctx):
    return jnp.where(is_ctx, mod_ref[1, idx:idx + 1, :], mod_ref[0, idx:idx + 1, :])


def _ctx_rows(tile_idx, tm, n_lat):
    rows = tile_idx * tm + lax.broadcasted_iota(jnp.int32, (tm, 1), 0)
    return rows >= n_lat


def _ada_kernel(s_ref, w_ref, b_ref, o_ref):
    s = s_ref[...]
    s = s * jax.nn.sigmoid(s)
    o_ref[...] = jnp.dot(s.astype(BF16), w_ref[...].astype(BF16),
                         preferred_element_type=F32) + b_ref[...]


def _ada_call(s_in, w_ada, b_ada):
    depth, d, d6 = w_ada.shape
    rows = s_in.shape[0]
    tn = d6 // 4
    return pl.pallas_call(
        _ada_kernel,
        grid=(depth, d6 // tn),
        in_specs=[
            pl.BlockSpec((rows, d), lambda l, j: (0, 0)),
            pl.BlockSpec((None, d, tn), lambda l, j: (l, 0, j)),
            pl.BlockSpec((None, 1, tn), lambda l, j: (l, 0, j)),
        ],
        out_specs=pl.BlockSpec((None, rows, tn), lambda l, j: (l, 0, j)),
        out_shape=jax.ShapeDtypeStruct((depth, rows, d6), F32),
        compiler_params=_cparams(("parallel", "parallel")),
    )(s_in, w_ada, b_ada.reshape(depth, 1, d6))


def _inproj_kernel(x_ref, mod_ref, wn_ref, wt_ref, cos_ref, sin_ref, qg_ref, kg_ref,
                   ret_ref, g_ref, lru_ref, qt_ref, k_ref, vt_ref, kn_ref, *, n_lat):
    tm = x_ref.shape[0]
    is_ctx = _ctx_rows(pl.program_id(1), tm, n_lat)
    x = x_ref[...]
    u = (x * (1.0 + _row_mod(mod_ref, 1, is_ctx)) + _row_mod(mod_ref, 0, is_ctx)).astype(BF16)

    pt = lax.dot_general(wt_ref[...], u, (((1,), (1,)), ((), ())), preferred_element_type=F32)

    pn = jnp.dot(u, wn_ref[...], preferred_element_type=F32)
    rw = g_ref.shape[1]
    ret_ref[:, 0:rw] = pn[:, 0:rw].astype(BF16)
    ret_ref[:, rw:2 * rw] = (pn[:, rw:2 * rw] * HEAD_DIM ** -0.5).astype(BF16)
    ret_ref[:, 2 * rw:3 * rw] = pn[:, 2 * rw:3 * rw].astype(BF16)
    g_ref[...] = pn[:, 3 * rw:4 * rw]
    lru_ref[...] = pn[:, 4 * rw:]

    cos = cos_ref[...]
    sin = sin_ref[...]
    half = HEAD_DIM // 2

    def norm_rope(t, gain):
        ms = jnp.mean(t * t, axis=0, keepdims=True)
        t = t * lax.rsqrt(ms + EPS) * gain
        x1, x2 = t[:half], t[half:]
        return jnp.concatenate([x1 * cos - x2 * sin, x1 * sin + x2 * cos], axis=0)

    qw = qt_ref.shape[0]
    kw = k_ref.shape[1]
    qscale = HEAD_DIM ** -0.5 * LOG2E
    for h in range(qw // HEAD_DIM):
        r = h * HEAD_DIM
        qt_ref[r:r + HEAD_DIM, :] = (norm_rope(pt[r:r + HEAD_DIM], qg_ref[...]) * qscale).astype(BF16)
    kt = jnp.concatenate(
        [norm_rope(pt[qw + h * HEAD_DIM:qw + (h + 1) * HEAD_DIM], kg_ref[...])
         for h in range(kw // HEAD_DIM)], axis=0)
    k_ref[...] = kt.T.astype(BF16)
    kf = kt.astype(BF16).astype(F32)
    for h in range(kw // HEAD_DIM):
        kh = kf[h * HEAD_DIM:(h + 1) * HEAD_DIM]
        kn_ref[h:h + 1, :] = jnp.sum(kh * kh, axis=0, keepdims=True)
    ones = jnp.ones((ONES_ROWS, tm), BF16)
    vrows = HEAD_DIM + ONES_ROWS
    for h in range(kw // HEAD_DIM):
        v0 = qw + kw + h * HEAD_DIM
        vt_ref[h * vrows:h * vrows + HEAD_DIM, :] = pt[v0:v0 + HEAD_DIM].astype(BF16)
        vt_ref[h * vrows + HEAD_DIM:(h + 1) * vrows, :] = ones


def _inproj_call(x_all, mod, wn, wt, cos_t, sin_t, qg, kg, n_lat, tm=256):
    b, s, d = x_all.shape
    rw, qw, kw = d // 4, d // 2, d // 8
    vw = (kw // HEAD_DIM) * (HEAD_DIM + ONES_ROWS)
    grid = (b, s // tm)
    tok = lambda shape_last: pl.BlockSpec((None, tm, shape_last), lambda i, j: (i, j, 0))
    tr = lambda rows: pl.BlockSpec((None, rows, tm), lambda i, j: (i, 0, j))
    const = lambda shape: pl.BlockSpec(shape, lambda i, j: tuple(0 for _ in shape))
    return pl.pallas_call(
        functools.partial(_inproj_kernel, n_lat=n_lat),
        grid=grid,
        in_specs=[
            tok(d),
            pl.BlockSpec((None, 2, 6, d), lambda i, j: (i, 0, 0, 0)),
            const(wn.shape), const(wt.shape),
            pl.BlockSpec((HEAD_DIM // 2, tm), lambda i, j: (0, j)),
            pl.BlockSpec((HEAD_DIM // 2, tm), lambda i, j: (0, j)),
            const(qg.shape), const(kg.shape),
        ],
        out_specs=[tok(3 * rw), tok(rw), tok(2 * rw), tr(qw), tok(kw),
                   pl.BlockSpec((None, None, vw, tm), lambda i, j: (i, j, 0, 0)),
                   tr(kw // HEAD_DIM)],
        out_shape=[
            jax.ShapeDtypeStruct((b, s, 3 * rw), BF16),
            jax.ShapeDtypeStruct((b, s, rw), F32),
            jax.ShapeDtypeStruct((b, s, 2 * rw), F32),
            jax.ShapeDtypeStruct((b, qw, s), BF16),
            jax.ShapeDtypeStruct((b, s, kw), BF16),
            jax.ShapeDtypeStruct((b, s // tm, vw, tm), BF16),
            jax.ShapeDtypeStruct((b, kw // HEAD_DIM, s), F32),
        ],
        compiler_params=_cparams(("parallel", "parallel")),
    )(x_all, mod, wn, wt, cos_t, sin_t, qg, kg)


def _ret_kernel(ret_ref, g_ref, dl_ref, o_ref, sf_ref, sb_ref, dm_ref, *, n_lat_chunks):
    c_len = CHUNK
    s_len, rw = g_ref.shape
    nc = s_len // c_len
    ncl = n_lat_chunks
    n_heads = rw // HEAD_DIM

    lg = _log_sigmoid(dl_ref[...])
    lgf, lgb = lg[0:1], lg[1:2]
    pos = lax.broadcasted_iota(jnp.int32, (c_len, 1), 0).astype(F32)
    kwf = jnp.exp(lgf * (c_len - 1.0 - pos))
    kwb = jnp.exp(lgb * pos)
    qwf = jnp.exp(lgf * (pos + 1.0))
    qwb = jnp.exp(lgb * (c_len - pos))
    cdf = jnp.exp(lgf * c_len)
    cdb = jnp.exp(lgb * c_len)
    row_head = lax.broadcasted_iota(jnp.int32, (rw, rw), 0) // HEAD_DIM
    col_head = lax.broadcasted_iota(jnp.int32, (rw, rw), 1) // HEAD_DIM
    same_head = row_head == col_head
    lane_head = lax.broadcasted_iota(jnp.int32, (1, rw), 1) // HEAD_DIM

    ii = lax.broadcasted_iota(jnp.int32, (c_len, c_len), 0)
    jj = lax.broadcasted_iota(jnp.int32, (c_len, c_len), 1)
    dij = (ii - jj).astype(F32)
    for h in range(n_heads):
        lf = lgf[:, h * HEAD_DIM:h * HEAD_DIM + 1]
        lb = lgb[:, h * HEAD_DIM:h * HEAD_DIM + 1]
        dm_ref[h] = jnp.exp(jnp.where(dij >= 0.0, lf * dij, -lb * dij))

    def contrib(c, carry):
        r0 = pl.multiple_of(c * c_len, c_len)
        k = ret_ref[pl.ds(r0, c_len), rw:2 * rw].astype(F32)
        v = ret_ref[pl.ds(r0, c_len), 2 * rw:3 * rw]
        tn = (((0,), (0,)), ((), ()))
        cf = lax.dot_general((k * kwf).astype(BF16), v, tn, preferred_element_type=F32)
        cb = lax.dot_general((k * kwb).astype(BF16), v, tn, preferred_element_type=F32)
        sf_ref[c] = jnp.where(same_head, cf, 0.0)
        sb_ref[c] = jnp.where(same_head, cb, 0.0)
        return carry

    lax.fori_loop(0, nc, contrib, 0)

    def chain(ref, decay, order):
        state = jnp.zeros((rw, rw), F32)
        for c in order:
            u = ref[c]
            ref[c] = state
            state = decay * state + u

    chain(sf_ref, cdf, list(range(ncl, nc)) + list(range(ncl)))
    chain(sb_ref, cdb, list(range(nc - 1, ncl - 1, -1)) + list(range(ncl - 1, -1, -1)))

    ones_blk = jnp.where(same_head, 1.0, 0.0).astype(BF16)

    def outputs(chunks):
        rows = [c * c_len if isinstance(c, int) else pl.multiple_of(c * c_len, c_len) for c in chunks]
        qs =[ret_ref[pl.ds(r0, c_len), 0:rw] for r0 in rows]
        ks = [ret_ref[pl.ds(r0, c_len), rw:2 * rw] for r0 in rows]
        vs = [ret_ref[pl.ds(r0, c_len), 2 * rw:3 * rw] for r0 in rows]
        os_, scs = [], []
        for c, q, k in zip(chunks, qs, ks):
            qf = q.astype(F32)
            qi = jnp.concatenate([(qf * qwf).astype(BF16), (qf * qwb).astype(BF16)], axis=1)
            st = jnp.concatenate([sf_ref[c], sb_ref[c]], axis=0).astype(BF16)
            os_.append(jnp.dot(qi, st, preferred_element_type=F32))
            scs.append([lax.dot_general(jnp.where(lane_head == h, q, jnp.zeros_like(q)), k,
                                        (((1,), (1,)), ((), ())), preferred_element_type=F32)
                        for h in range(n_heads)])
        for i, v in enumerate(vs):
            for h in range(n_heads):
                p = (scs[i][h] * dm_ref[h]).astype(BF16)
                os_[i] = os_[i] + jnp.where(lane_head == h, jnp.dot(p, v, preferred_element_type=F32), 0.0)
        mss = []
        for o in os_:
            o2 = o * o
            hi = o2.astype(BF16)
            lo = (o2 - hi.astype(F32)).astype(BF16)
            mss.append((jnp.dot(hi, ones_blk, preferred_element_type=F32)
                        + jnp.dot(lo, ones_blk, preferred_element_type=F32)) * (1.0 / HEAD_DIM))
        for r0, o, ms in zip(rows, os_, mss):
            g = g_ref[pl.ds(r0, c_len), :]
            o_ref[pl.ds(r0, c_len), :] = (o * lax.rsqrt(ms + EPS) * (g * jax.nn.sigmoid(g))).astype(BF16)

    def output_pair(i, carry):
        outputs([2 * i, 2 * i + 1])
        return carry

    lax.fori_loop(0, nc // 2, output_pair, 0)
    if nc % 2:
        outputs([nc - 1])


def _ret_call(ret_in, g_in, dl_lane, n_lat):
    b, s, rw = g_in.shape
    nc = s // CHUNK
    return pl.pallas_call(
        functools.partial(_ret_kernel, n_lat_chunks=n_lat // CHUNK),
        grid=(b,),
        in_specs=[
            pl.BlockSpec((None, s, 3 * rw), lambda i: (i, 0, 0)),
            pl.BlockSpec((None, s, rw), lambda i: (i, 0, 0)),
            pl.BlockSpec((2, rw), lambda i: (0, 0)),
        ],
        out_specs=pl.BlockSpec((None, s, rw), lambda i: (i, 0, 0)),
        out_shape=jax.ShapeDtypeStruct((b, s, rw), BF16),
        scratch_shapes=[
            pltpu.VMEM((nc, rw, rw), F32),
            pltpu.VMEM((nc, rw, rw), F32),
            pltpu.VMEM((rw // HEAD_DIM, CHUNK, CHUNK), F32),
        ],
        compiler_params=_cparams(("parallel",)),
    )(ret_in, g_in, dl_lane)


def _attn_kernel(qt_ref, k_ref, vt_ref, kn_ref, *rest, group):
    o_ref, rhs_ref, sa_ref, sb_ref, pc_ref, m_ref, acc_ref, out_ref = rest[-8:]
    n_q_heads = qt_ref.shape[0] // HEAD_DIM
    n_chunks = vt_ref.shape[0]
    kc_len = vt_ref.shape[2]
    vrows = vt_ref.shape[1] // (n_q_heads // group)

    def scores(c, h):
        c0 = c * kc_len
        if not isinstance(c0, int):
            c0 = pl.multiple_of(c0, kc_len)
        return jnp.dot(k_ref[pl.ds(c0, kc_len), :], rhs_ref[h], preferred_element_type=F32)

    def p_times_v(c, h, p):
        kv = h // group
        return jnp.dot(vt_ref[c, kv * vrows:(kv + 1) * vrows, :], p, preferred_element_type=F32)

    kmax2 = jnp.max(kn_ref[...], axis=1, keepdims=True)
    for h in range(n_q_heads):
        qt = qt_ref[h * HEAD_DIM:(h + 1) * HEAD_DIM, :]
        zero = jnp.zeros_like(qt)
        rhs_ref[h] = jnp.concatenate([qt, zero] if h // group == 0 else [zero, qt], axis=0)
        sa_ref[h] = scores(0, h)
        qf = qt.astype(F32)
        qn2 = jnp.sum(qf * qf, axis=0, keepdims=True)
        m_ref[h:h + 1, :] = jnp.sqrt(qn2 * kmax2[h // group:h // group + 1, :]) * SCORE_BOUND_MARGIN
    acc_ref[...] = jnp.zeros(acc_ref.shape, F32)
    bounded = jnp.max(m_ref[...]) <= SCORE_BOUND_CAP

    def run_chunks(step):
        def pair(i, carry):
            step(2 * i, sa_ref, sb_ref)
            step(2 * i + 1, sb_ref, sa_ref)
            return carry

        lax.fori_loop(0, (n_chunks - 1) // 2, pair, 0)
        if (n_chunks - 1) % 2 == 1:
            step(n_chunks - 2, sa_ref, sb_ref)
            step(n_chunks - 1, sb_ref, None)
        else:
            step(n_chunks - 1, sa_ref, None)

    @pl.when(bounded)
    def _():
        lag = PV_LAG
        pc_ref[...] = jnp.zeros(pc_ref.shape, BF16)

        def step(c, cur_ref, nxt_ref):
            ps = [pc_ref[i] for i in range(lag)]
            c_prev = max(c - 1, 0) if isinstance(c, int) else jnp.maximum(c - 1, 0)
            for h in range(n_q_heads):
                if nxt_ref is not None:
                    nxt_ref[h] = scores(c + 1, h)
                if h < lag:
                    hp = n_q_heads - lag + h
                    acc_ref[hp] += p_times_v(c_prev, hp, ps[h])
                else:
                    acc_ref[h - lag] += p_times_v(c, h - lag, ps[h])
                ps.append(jnp.exp2(cur_ref[h] - m_ref[h:h + 1, :]).astype(BF16))
            for i in range(lag):
                pc_ref[i] = ps[n_q_heads + i]

        run_chunks(step)
        for i in range(lag):
            hp = n_q_heads - lag + i
            acc_ref[hp] += p_times_v(n_chunks - 1, hp, pc_ref[i])

    @pl.when(jnp.logical_not(bounded))
    def _():
        m_ref[...] = jnp.full(m_ref.shape, -jnp.inf, F32)

        def step(c, cur_ref, nxt_ref):
            for h in range(n_q_heads):
                if nxt_ref is not None:
                    nxt_ref[h] = scores(c + 1, h)
                s = cur_ref[h]
                m_old = m_ref[h:h + 1, :]
                m_new = jnp.maximum(m_old, jnp.max(s, axis=0, keepdims=True))
                alpha = jnp.exp2(m_old - m_new)
                m_ref[h:h + 1, :] = m_new
                p = jnp.exp2(s - m_new).astype(BF16)
                acc_ref[h] = alpha * acc_ref[h] + p_times_v(c, h, p)

        run_chunks(step)

    for h in range(n_q_heads):
        a = acc_ref[h]
        out_ref[h * HEAD_DIM:(h + 1) * HEAD_DIM, :] = a[:HEAD_DIM] / a[HEAD_DIM:HEAD_DIM + 1]
    o_ref[...] = out_ref[...].T.astype(BF16)


def _attn_call(qt, k, vt, kn, n_lat, with_ctx, tq_lat=512, tq_ctx=256):
    b, qw, s = qt.shape
    kw = k.shape[2]
    vw, kc_len = vt.shape[2], vt.shape[3]
    n_ctx = s - n_lat
    n_heads = qw // HEAD_DIM
    n_kv = kw // HEAD_DIM
    assert n_kv == 2 and n_lat % n_ctx == 0 and n_ctx % kc_len == 0
    tq_lat = min(tq_lat, n_lat)
    tq_ctx = min(tq_ctx, n_ctx)
    assert n_lat % tq_lat == 0 and n_ctx % tq_ctx == 0
    body = functools.partial(_attn_kernel, group=n_heads // n_kv)
    out_sds = jax.ShapeDtypeStruct((b, s, qw), BF16)

    def scratch(tq):
        return [pltpu.VMEM((n_heads, kw, tq), BF16),
                pltpu.VMEM((n_heads, kc_len, tq), F32), pltpu.VMEM((n_heads, kc_len, tq), F32),
                pltpu.VMEM((PV_LAG, kc_len, tq), BF16),
                pltpu.VMEM((n_heads, tq), F32), pltpu.VMEM((n_heads, vw // n_kv, tq), F32),
                pltpu.VMEM((qw, tq), F32)]

    att = pl.pallas_call(
        body,
        grid=(b, n_lat // tq_lat),
        in_specs=[
            pl.BlockSpec((None, qw, tq_lat), lambda i, j: (i, 0, j)),
            pl.BlockSpec((None, s, kw), lambda i, j: (i, 0, 0)),
            pl.BlockSpec((None, s // kc_len, vw, kc_len), lambda i, j: (i, 0, 0, 0)),
            pl.BlockSpec((None, n_kv, s), lambda i, j: (i, 0, 0)),
        ],
        out_specs=pl.BlockSpec((None, tq_lat, qw), lambda i, j: (i, j, 0)),
        out_shape=out_sds,
        scratch_shapes=scratch(tq_lat),
        compiler_params=_cparams(("parallel", "arbitrary")),
    )(qt, k, vt, kn)
    if not with_ctx:
        return att
    lat_tiles, lat_ctx = n_lat // tq_ctx, n_lat // n_ctx
    return pl.pallas_call(
        body,
        grid=(b, n_ctx // tq_ctx),
        in_specs=[
            pl.BlockSpec((None, qw, tq_ctx), lambda i, j: (i, 0, lat_tiles + j)),
            pl.BlockSpec((None, n_ctx, kw), lambda i, j: (i, lat_ctx, 0)),
            pl.BlockSpec((None, n_ctx // kc_len, vw, kc_len), lambda i, j: (i, lat_ctx, 0, 0)),
            pl.BlockSpec((None, n_kv, n_ctx), lambda i, j: (i, 0, lat_ctx)),
            pl.BlockSpec(memory_space=pl.ANY),
        ],
        out_specs=pl.BlockSpec((None, tq_ctx, qw), lambda i, j: (i, lat_tiles + j, 0)),
        out_shape=out_sds,
        input_output_aliases={4: 0},
        scratch_shapes=scratch(tq_ctx),
        compiler_params=_cparams(("parallel", "arbitrary")),
    )(qt, k, vt, kn, att)


def _lru_pre_kernel(cur_ref, prev_ref, next_ref, cw_ref, cb_ref, wg_ref, bg_ref, lam_ref,
                    af_ref, df_ref, ab_ref, db_ref, xs_ref, *, n_lat_chunks, n_chunks):
    c = pl.program_id(0)
    n_b, c_len, w = cur_ref.shape
    first = jnp.logical_or(c == 0, c == n_lat_chunks)
    last = jnp.logical_or(c == n_lat_chunks - 1, c == n_chunks - 1)
    xs_ref[0:8] = jnp.where(first, 0.0, jnp.swapaxes(prev_ref[...], 0, 1))
    xs_ref[8:8 + c_len] = jnp.swapaxes(cur_ref[...], 0, 1)
    xs_ref[8 + c_len:] = jnp.where(last, 0.0, jnp.swapaxes(next_ref[...], 0, 1))
    half_c_lam = (0.5 * LRU_C) * _log_sigmoid(lam_ref[...])
    ts = LRU_SUB_TOKENS

    def sub_chunk(i, carry):
        t0 = pl.multiple_of(i * ts, ts)

        def shifted(off):
            return xs_ref[pl.ds(8 + t0 + off, ts)]

        xr = (shifted(-2) * cw_ref[0:1, :] + shifted(-1) * cw_ref[1:2, :] + shifted(0) * cw_ref[2:3, :]
              + shifted(1) * cw_ref[3:4, :] + cb_ref[...]).reshape(ts * n_b, w)
        half_gates = jnp.dot(xr.astype(BF16), wg_ref[...], preferred_element_type=F32) + bg_ref[...]
        half_xr = 0.5 * xr
        for d, (a_ref, d_ref) in enumerate(((af_ref, df_ref), (ab_ref, db_ref))):
            t_r = jnp.tanh(half_gates[:, 2 * d * w:(2 * d + 1) * w])
            t_i = jnp.tanh(half_gates[:, (2 * d + 1) * w:(2 * d + 2) * w])
            half_c = half_c_lam[d:d + 1, :]
            log_a = half_c * t_r + half_c
            a = jnp.exp(log_a)
            om = (1.0 + a * a) * jnp.tanh(-log_a)
            drive = om * lax.rsqrt(jnp.maximum(om, 1e-30)) * (half_xr * t_i + half_xr)
            a_ref[pl.ds(t0, ts)] = a.reshape(ts, n_b, w)
            d_ref[pl.ds(t0, ts)] = drive.reshape(ts, n_b, w)
        return carry

    lax.fori_loop(0, c_len // ts, sub_chunk, 0)


def _lru_pre_call(lru_in, conv_w, conv_b, wg, bg, lam, n_lat):
    b, s, w2 = lru_in.shape
    w = w2 // 2
    nc = s // CHUNK
    hb = CHUNK // 8
    const = lambda shape: pl.BlockSpec(shape, lambda j: tuple(0 for _ in shape))
    out_spec = pl.BlockSpec((CHUNK, b, w), lambda j: (j, 0, 0))
    out_sds = jax.ShapeDtypeStruct((s, b, w), F32)
    return pl.pallas_call(
        functools.partial(_lru_pre_kernel, n_lat_chunks=n_lat // CHUNK, n_chunks=nc),
        grid=(nc,),
        in_specs=[
            pl.BlockSpec((b, CHUNK, w), lambda j: (0, j, 0)),
            pl.BlockSpec((b, 8, w), lambda j: (0, jnp.maximum(j * hb - 1, 0), 0)),
            pl.BlockSpec((b, 8, w), lambda j: (0, jnp.minimum((j + 1) * hb, s // 8 - 1), 0)),
            const(conv_w.shape), const(conv_b.shape), const(wg.shape), const(bg.shape),
            const(lam.shape),
        ],
        out_specs=[out_spec] * 4,
        out_shape=[out_sds] * 4,
        scratch_shapes=[pltpu.VMEM((CHUNK + 16, b, w), F32)],
        compiler_params=_cparams(("parallel",)),
    )(lru_in, lru_in, lru_in, conv_w, conv_b, wg, bg, lam)


def _scan_kernel(af_ref, df_ref, ab_ref, db_ref, hf_ref, hb_ref, sf_ref, sb_ref, tf_ref, tb_ref):
    tt = af_ref.shape[0]

    @pl.when(pl.program_id(0) == 0)
    def _():
        sf_ref[...] = jnp.zeros_like(sf_ref)
        sb_ref[...] = jnp.zeros_like(sb_ref)

    def step(t, carry):
        hf, hb = carry
        hf = af_ref[t] * hf + df_ref[t]
        tf_ref[t] = hf
        tb = tt - 1 - t
        hb = ab_ref[tb] * hb + db_ref[tb]
        tb_ref[tb] = hb
        return hf, hb

    hf, hb = lax.fori_loop(0, tt, step, (sf_ref[...], sb_ref[...]), unroll=8)
    sf_ref[...] = hf
    sb_ref[...] = hb
    hf_ref[...] = jnp.swapaxes(tf_ref[...], 0, 1)
    hb_ref[...] = jnp.swapaxes(tb_ref[...], 0, 1)


def _scan_call(af, df, ab, db, n_lat):
    s, b, w = af.shape
    nt = s // SCAN_TILE
    nlt = n_lat // SCAN_TILE
    fwd = pl.BlockSpec((SCAN_TILE, b, w), lambda i: ((i + nlt) % nt, 0, 0))
    bwd = pl.BlockSpec((SCAN_TILE, b, w), lambda i: (nt - 1 - i, 0, 0))
    fwd_out = pl.BlockSpec((b, SCAN_TILE, w), lambda i: (0, (i + nlt) % nt, 0))
    bwd_out = pl.BlockSpec((b, SCAN_TILE, w), lambda i: (0, nt - 1 - i, 0))
    sds = jax.ShapeDtypeStruct((b, s, w), F32)
    return pl.pallas_call(
        _scan_kernel,
        grid=(nt,),
        in_specs=[fwd, fwd, bwd, bwd],
        out_specs=[fwd_out, bwd_out],
        out_shape=[sds, sds],
        scratch_shapes=[pltpu.VMEM((b, w), F32), pltpu.VMEM((b, w), F32),
                        pltpu.VMEM((SCAN_TILE, b, w), F32), pltpu.VMEM((SCAN_TILE, b, w), F32)],
        compiler_params=_cparams(("arbitrary",)),
    )(af, df, ab, db)


def _row_splits(tm, n_sub):
    units = tm // 16
    assert tm % 16 == 0 and units >= n_sub
    cuts = [16 * ((units * i) // n_sub) for i in range(n_sub + 1)]
    return list(zip(cuts[:-1], cuts[1:]))


def _tail_kernel(x_ref, ret_ref, att_ref, hf_ref, hb_ref, lg_ref, mod_ref, wo_ref, g1_ref, b1_ref,
                 w1_ref, w2_ref, g2_ref, b2_ref, o_ref, x1_ref, u_ref, acc_ref, *, n_lat, alpha):
    tm = x_ref.shape[0]
    kf = pl.program_id(2)
    nk = pl.num_programs(2)
    subs = _row_splits(tm, TAIL_SUB_BLOCKS)
    rw = ret_ref.shape[1]
    aw = att_ref.shape[1]

    def ctx_rows(r0, r1):
        rows = pl.program_id(1) * tm + r0 + lax.broadcasted_iota(jnp.int32, (r1 - r0, 1), 0)
        return rows >= n_lat

    def mlp_part(r0, r1):
        h = jnp.maximum(jnp.dot(u_ref[r0:r1, :], w1_ref[...], preferred_element_type=F32), 0.0)
        return jnp.dot((h * h).astype(BF16), w2_ref[...], preferred_element_type=F32)

    @pl.when(kf == 0)
    def _():
        for r0, r1 in subs:
            is_ctx = ctx_rows(r0, r1)
            lru = ((hf_ref[r0:r1, :] + hb_ref[r0:r1, :]) * jax.nn.gelu(lg_ref[r0:r1, :])).astype(BF16)
            y = jnp.dot(ret_ref[r0:r1, :], wo_ref[0:rw, :], preferred_element_type=F32)
            y = y + jnp.dot(att_ref[r0:r1, :], wo_ref[rw:rw + aw, :], preferred_element_type=F32)
            y = y + jnp.dot(lru, wo_ref[rw + aw:, :], preferred_element_type=F32)
            x1 = _layer_norm(alpha * x_ref[r0:r1, :] + _row_mod(mod_ref, 2, is_ctx) * y,
                             g1_ref[...], b1_ref[...])
            x1_ref[r0:r1, :] = x1
            u_ref[r0:r1, :] = (x1 * (1.0 + _row_mod(mod_ref, 4, is_ctx))
                               + _row_mod(mod_ref, 3, is_ctx)).astype(BF16)
        for r0, r1 in subs:
            acc_ref[r0:r1, :] = mlp_part(r0, r1)

    @pl.when(jnp.logical_and(kf > 0, kf < nk - 1))
    def _():
        acc_ref[...] += mlp_part(0, tm)

    @pl.when(kf == nk - 1)
    def _():
        for r0, r1 in subs:
            a = acc_ref[r0:r1, :] + mlp_part(r0, r1)
            z = alpha * x1_ref[r0:r1, :] + _row_mod(mod_ref, 5, ctx_rows(r0, r1)) * a
            o_ref[r0:r1, :] = _layer_norm(z, g2_ref[...], b2_ref[...])


def _tail_call(x_all, ret, att, hf, hb, lru_in, mod, w_out, ln1_g, ln1_b, w1, w2, ln2_g, ln2_b,
               n_lat, n_rows, alpha, tm, tf=2048):
    b, _, d = x_all.shape
    rw, aw = ret.shape[2], att.shape[2]
    f = w1.shape[1]
    assert f // tf >= 2
    tok = lambda last: pl.BlockSpec((None, tm, last), lambda i, j, k: (i, j, 0))
    const = lambda shape: pl.BlockSpec(shape, lambda i, j, k: tuple(0 for _ in shape))
    return pl.pallas_call(
        functools.partial(_tail_kernel, n_lat=n_lat, alpha=alpha),
        grid=(b, n_rows // tm, f // tf),
        in_specs=[
            tok(d), tok(rw), tok(aw), tok(rw), tok(rw),
            pl.BlockSpec((None, tm, rw), lambda i, j, k: (i, j, 1)),
            pl.BlockSpec((None, 2, 6, d), lambda i, j, k: (i, 0, 0, 0)),
            const(w_out.shape), const(ln1_g.shape), const(ln1_b.shape),
            pl.BlockSpec((d, tf), lambda i, j, k: (0, k)),
            pl.BlockSpec((tf, d), lambda i, j, k: (k, 0)),
            const(ln2_g.shape), const(ln2_b.shape),
        ],
        out_specs=tok(d),
        out_shape=jax.ShapeDtypeStruct((b, n_rows, d), F32),
        scratch_shapes=[pltpu.VMEM((tm, d), F32), pltpu.VMEM((tm, d), BF16), pltpu.VMEM((tm, d), F32)],
        compiler_params=_cparams(("parallel", "parallel", "arbitrary")),
    )(x_all, ret, att, hf, hb, lru_in, mod, w_out, ln1_g, ln1_b, w1, w2, ln2_g, ln2_b)


def _rope_tables(n_lat, n_ctx):
    rows = n_lat // GRID_W
    row = jnp.repeat(jnp.arange(rows, dtype=F32), GRID_W)
    col = jnp.tile(jnp.arange(GRID_W, dtype=F32), rows)
    n_freq = HEAD_DIM // 4
    inv = ROPE_THETA ** (-jnp.arange(n_freq, dtype=F32) / n_freq)
    ang = jnp.concatenate([row[:, None] * inv, col[:, None] * inv], axis=-1)
    cos = jnp.concatenate([jnp.cos(ang), jnp.ones((n_ctx, HEAD_DIM // 2), F32)], axis=0)
    sin = jnp.concatenate([jnp.sin(ang), jnp.zeros((n_ctx, HEAD_DIM // 2), F32)], axis=0)
    return cos.T, sin.T


def _block_diag(w):
    k, c = w.shape[-3], w.shape[-2]
    eye = jnp.eye(k, dtype=w.dtype)
    bd = jnp.einsum('...kce,kj->...kcje', w, eye)
    return bd.reshape(*w.shape[:-3], k * c, k * c)


def kernel(x, c, ctx, c_ctx, w_ada, b_ada, w_in, ret_decay_logit, attn_q_gain, attn_k_gain,
           lru_conv_w, lru_conv_b, lru_w_a, lru_b_a, lru_w_x, lru_b_x, lru_lambda,
           w_out, ln1_g, ln1_b, w_ff1, w_ff2, ln2_g, ln2_b):
    b, n_lat, d = x.shape
    n_ctx = ctx.shape[1]
    depth = w_in.shape[0]
    s = n_lat + n_ctx
    rw, aw, kw = d // 4, d // 2, d // 8
    alpha = (2.0 * depth) ** 0.25
    assert n_lat % CHUNK == 0 and n_ctx % CHUNK == 0 and d == 16 * HEAD_DIM

    pad = (-(b + 1)) % 8
    s_in = jnp.concatenate([c, c_ctx[None, :], jnp.zeros((pad, d), F32)], axis=0)
    mods = _ada_call(s_in, w_ada, b_ada)
    mod_lat = mods[:, :b].reshape(depth, b, 1, 6, d)
    mod_ctx = jnp.broadcast_to(mods[:, b].reshape(depth, 1, 1, 6, d), (depth, b, 1, 6, d))
    mod_all = jnp.concatenate([mod_lat, mod_ctx], axis=2)

    o_aq = 4 * rw
    o_lx = o_aq + aw + 2 * kw
    wn = jnp.concatenate([w_in[:, :, :o_aq], w_in[:, :, o_lx:]], axis=2).astype(BF16)
    wt = jnp.swapaxes(w_in[:, :, o_aq:o_lx], 1, 2).astype(BF16)
    wg = (0.5 * jnp.concatenate([_block_diag(lru_w_a[:, 0]), _block_diag(lru_w_x[:, 0]),
                                 _block_diag(lru_w_a[:, 1]), _block_diag(lru_w_x[:, 1])],
                                axis=-1)).astype(BF16)
    bg = 0.5 * jnp.concatenate([lru_b_a[:, 0], lru_b_x[:, 0], lru_b_a[:, 1], lru_b_x[:, 1]],
                               axis=-1)[:, None, :]
    w_out_b = w_out.astype(BF16)
    w1_b = w_ff1.astype(BF16)
    w2_b = w_ff2.astype(BF16)
    dl_lane = jnp.repeat(ret_decay_logit, HEAD_DIM, axis=-1)
    cos_t, sin_t = _rope_tables(n_lat, n_ctx)

    def tail_tile(rows):
        return rows // 8 if rows % 64 == 0 else CHUNK

    xa = jnp.concatenate([x, ctx], axis=1)
    for l in range(depth):
        need_ctx = l < depth - 1
        mod = mod_all[l]
        ret_in, g_in, lru_in, qt, k, vt, kn = _inproj_call(
            xa, mod, wn[l], wt[l], cos_t, sin_t,
            attn_q_gain[l][:, None], attn_k_gain[l][:, None], n_lat)
        ret = _ret_call(ret_in, g_in, dl_lane[l], n_lat)
        att = _attn_call(qt, k, vt, kn, n_lat, need_ctx)
        af, df, ab, db = _lru_pre_call(lru_in, lru_conv_w[l], lru_conv_b[l][None, :], wg[l], bg[l],
                                       lru_lambda[l], n_lat)
        hf, hb = _scan_call(af, df, ab, db, n_lat)
        n_rows = s if need_ctx else n_lat
        xa = _tail_call(xa, ret, att, hf, hb, lru_in, mod,
                        w_out_b[l], ln1_g[l][None, :], ln1_b[l][None, :], w1_b[l], w2_b[l],
                        ln2_g[l][None, :], ln2_b[l][None, :], n_lat, n_rows, alpha, tail_tile(n_rows))
    return xa
```

```python
import functools
import math

import jax
import jax.numpy as jnp
from jax import lax
from jax.experimental import pallas as pl
from jax.experimental.pallas import tpu as pltpu

F32 = jnp.float32
BF16 = jnp.bfloat16

HEAD_DIM = 64
GRID_W = 64
ROPE_THETA = 10000.0
LRU_C = 8.0
LRU_BLOCKS = 4
EPS = 1e-6
LOG2E = 1.4426950408889634

ONES_ROWS = 16
SCORE_BOUND_MARGIN = 1.0 + 2.0 ** -7
SCORE_BOUND_CAP = 60.0
PV_LAG = 2
TAIL_SUB_BLOCKS = 2
CHUNK = 256
SCAN_TILE = 128
LRU_SUB_TOKENS = 32
VMEM_LIMIT = 56 * 1024 * 1024


def _cparams(sem):
    return pltpu.CompilerParams(dimension_semantics=sem, vmem_limit_bytes=VMEM_LIMIT)


def _log_sigmoid(x):
    return jnp.minimum(x, 0.0) - jnp.log1p(jnp.exp(-jnp.abs(x)))


def _layer_norm(z, g, b):
    mu = jnp.mean(z, axis=-1, keepdims=True)
    zc = z - mu
    var = jnp.mean(zc * zc, axis=-1, keepdims=True)
    return zc * lax.rsqrt(var + EPS) * g + b


def _row_mod(mod_ref, idx, is_ctx):
    return jnp.where(is_ctx, mod_ref[1, idx:idx + 1, :], mod_ref[0, idx:idx + 1, :])


def _ctx_rows(tile_idx, tm, n_lat):
    rows = tile_idx * tm + lax.broadcasted_iota(jnp.int32, (tm, 1), 0)
    return rows >= n_lat


def _ada_kernel(s_ref, w_ref, b_ref, o_ref):
    s = s_ref[...]
    s = s * jax.nn.sigmoid(s)
    o_ref[...] = jnp.dot(s.astype(BF16), w_ref[...].astype(BF16),
                         preferred_element_type=F32) + b_ref[...]


def _ada_call(s_in, w_ada, b_ada):
    depth, d, d6 = w_ada.shape
    rows = s_in.shape[0]
    tn = d6 // 4
    return pl.pallas_call(
        _ada_kernel,
        grid=(depth, d6 // tn),
        in_specs=[
            pl.BlockSpec((rows, d), lambda l, j: (0, 0)),
            pl.BlockSpec((None, d, tn), lambda l, j: (l, 0, j)),
            pl.BlockSpec((None, 1, tn), lambda l, j: (l, 0, j)),
        ],
        out_specs=pl.BlockSpec((None, rows, tn), lambda l, j: (l, 0, j)),
        out_shape=jax.ShapeDtypeStruct((depth, rows, d6), F32),
        compiler_params=_cparams(("parallel", "parallel")),
    )(s_in, w_ada, b_ada.reshape(depth, 1, d6))


def _inproj_kernel(x_ref, *refs, n_lat, joins_streams):
    if joins_streams:
        ctx_ref, refs, xa_ref = refs[0], refs[1:-1], refs[-1]
    (mod_ref, wn_ref, wt_ref, cos_ref, sin_ref, qg_ref, kg_ref,
     ret_ref, g_ref, lru_ref, qt_ref, k_ref, vt_ref, kn_ref) = refs
    tm = x_ref.shape[0]
    is_ctx = _ctx_rows(pl.program_id(1), tm, n_lat)
    if joins_streams:
        x = jnp.where(pl.program_id(1) * tm >= n_lat, ctx_ref[...], x_ref[...])
        xa_ref[...] = x
    else:
        x = x_ref[...]
    u = (x * (1.0 + _row_mod(mod_ref, 1, is_ctx)) + _row_mod(mod_ref, 0, is_ctx)).astype(BF16)

    pt = lax.dot_general(wt_ref[...], u, (((1,), (1,)), ((), ())), preferred_element_type=F32)

    pn = jnp.dot(u, wn_ref[...], preferred_element_type=F32)
    rw = g_ref.shape[1]
    ret_ref[:, 0:rw] = pn[:, 0:rw].astype(BF16)
    ret_ref[:, rw:2 * rw] = (pn[:, rw:2 * rw] * HEAD_DIM ** -0.5).astype(BF16)
    ret_ref[:, 2 * rw:3 * rw] = pn[:, 2 * rw:3 * rw].astype(BF16)
    g_ref[...] = pn[:, 3 * rw:4 * rw]
    lru_ref[...] = pn[:, 4 * rw:]

    cos = cos_ref[...]
    sin = sin_ref[...]
    half = HEAD_DIM // 2

    def norm_rope(t, gain):
        ms = jnp.mean(t * t, axis=0, keepdims=True)
        t = t * lax.rsqrt(ms + EPS) * gain
        x1, x2 = t[:half], t[half:]
        return jnp.concatenate([x1 * cos - x2 * sin, x1 * sin + x2 * cos], axis=0)

    qw = qt_ref.shape[0]
    kw = k_ref.shape[1]
    qscale = HEAD_DIM ** -0.5 * LOG2E
    for h in range(qw // HEAD_DIM):
        r = h * HEAD_DIM
        qt_ref[r:r + HEAD_DIM, :] = (norm_rope(pt[r:r + HEAD_DIM], qg_ref[...]) * qscale).astype(BF16)
    kt = jnp.concatenate(
        [norm_rope(pt[qw + h * HEAD_DIM:qw + (h + 1) * HEAD_DIM], kg_ref[...])
         for h in range(kw // HEAD_DIM)], axis=0)
    k_ref[...] = kt.T.astype(BF16)
    kf = kt.astype(BF16).astype(F32)
    for h in range(kw // HEAD_DIM):
        kh = kf[h * HEAD_DIM:(h + 1) * HEAD_DIM]
        kn_ref[h:h + 1, :] = jnp.sum(kh * kh, axis=0, keepdims=True)
    ones = jnp.ones((ONES_ROWS, tm), BF16)
    vrows = HEAD_DIM + ONES_ROWS
    for h in range(kw // HEAD_DIM):
        v0 = qw + kw + h * HEAD_DIM
        vt_ref[h * vrows:h * vrows + HEAD_DIM, :] = pt[v0:v0 + HEAD_DIM].astype(BF16)
        vt_ref[h * vrows + HEAD_DIM:(h + 1) * vrows, :] = ones


def _inproj_call(streams, mod, wn, wt, cos_t, sin_t, qg, kg, n_lat, tm=256):
    joins = isinstance(streams, tuple)
    b, _, d = streams[0].shape if joins else streams.shape
    s = n_lat + streams[1].shape[1] if joins else streams.shape[1]
    assert n_lat % tm == 0 and s % tm == 0
    nlt = n_lat // tm
    rw, qw, kw = d // 4, d // 2, d // 8
    vw = (kw // HEAD_DIM) * (HEAD_DIM + ONES_ROWS)
    grid = (b, s // tm)
    tok = lambda shape_last: pl.BlockSpec((None, tm, shape_last), lambda i, j: (i, j, 0))
    tr = lambda rows: pl.BlockSpec((None, rows, tm), lambda i, j: (i, 0, j))
    const = lambda shape: pl.BlockSpec(shape, lambda i, j: tuple(0 for _ in shape))
    if joins:
        x_specs = [pl.BlockSpec((None, tm, d), lambda i, j: (i, jnp.minimum(j, nlt - 1), 0)),
                   pl.BlockSpec((None, tm, d), lambda i, j: (i, jnp.maximum(j - nlt, 0), 0))]
        x_args = list(streams)
    else:
        x_specs, x_args = [tok(d)], [streams]
    return pl.pallas_call(
        functools.partial(_inproj_kernel, n_lat=n_lat, joins_streams=joins),
        grid=grid,
        in_specs=x_specs + [
            pl.BlockSpec((None, 2, 6, d), lambda i, j: (i, 0, 0, 0)),
            const(wn.shape), const(wt.shape),
            pl.BlockSpec((HEAD_DIM // 2, tm), lambda i, j: (0, j)),
            pl.BlockSpec((HEAD_DIM // 2, tm), lambda i, j: (0, j)),
            const(qg.shape), const(kg.shape),
        ],
        out_specs=[tok(3 * rw), tok(rw), tok(2 * rw), tr(qw), tok(kw),
                   pl.BlockSpec((None, None, vw, tm), lambda i, j: (i, j, 0, 0)),
                   tr(kw // HEAD_DIM)] + ([tok(d)] if joins else []),
        out_shape=[
            jax.ShapeDtypeStruct((b, s, 3 * rw), BF16),
            jax.ShapeDtypeStruct((b, s, rw), F32),
            jax.ShapeDtypeStruct((b, s, 2 * rw), F32),
            jax.ShapeDtypeStruct((b, qw, s), BF16),
            jax.ShapeDtypeStruct((b, s, kw), BF16),
            jax.ShapeDtypeStruct((b, s // tm, vw, tm), BF16),
            jax.ShapeDtypeStruct((b, kw // HEAD_DIM, s), F32),
        ] + ([jax.ShapeDtypeStruct((b, s, d), F32)] if joins else []),
        compiler_params=_cparams(("parallel", "parallel")),
    )(*x_args, mod, wn, wt, cos_t, sin_t, qg, kg)


def _ret_kernel(ret_ref, g_ref, dl_ref, o_ref, sf_ref, sb_ref, dm_ref, *, n_lat_chunks):
    c_len = CHUNK
    s_len, rw = g_ref.shape
    nc = s_len // c_len
    ncl = n_lat_chunks
    n_heads = rw // HEAD_DIM

    lg = _log_sigmoid(dl_ref[...])
    lgf, lgb = lg[0:1], lg[1:2]
    pos = lax.broadcasted_iota(jnp.int32, (c_len, 1), 0).astype(F32)
    kwf = jnp.exp(lgf * (c_len - 1.0 - pos))
    kwb = jnp.exp(lgb * pos)
    qwf = jnp.exp(lgf * (pos + 1.0))
    qwb = jnp.exp(lgb * (c_len - pos))
    cdf = jnp.exp(lgf * c_len)
    cdb = jnp.exp(lgb * c_len)
    row_head = lax.broadcasted_iota(jnp.int32, (rw, rw), 0) // HEAD_DIM
    col_head = lax.broadcasted_iota(jnp.int32, (rw, rw), 1) // HEAD_DIM
    same_head = row_head == col_head
    lane_head = lax.broadcasted_iota(jnp.int32, (1, rw), 1) // HEAD_DIM

    ii = lax.broadcasted_iota(jnp.int32, (c_len, c_len), 0)
    jj = lax.broadcasted_iota(jnp.int32, (c_len, c_len), 1)
    dij = (ii - jj).astype(F32)
    for h in range(n_heads):
        lf = lgf[:, h * HEAD_DIM:h * HEAD_DIM + 1]
        lb = lgb[:, h * HEAD_DIM:h * HEAD_DIM + 1]
        dm_ref[h] = jnp.exp(jnp.where(dij >= 0.0, lf * dij, -lb * dij))

    def contrib(c, carry):
        r0 = pl.multiple_of(c * c_len, c_len)
        k = ret_ref[pl.ds(r0, c_len), rw:2 * rw].astype(F32)
        v = ret_ref[pl.ds(r0, c_len), 2 * rw:3 * rw]
        tn = (((0,), (0,)), ((), ()))
        cf = lax.dot_general((k * kwf).astype(BF16), v, tn, preferred_element_type=F32)
        cb = lax.dot_general((k * kwb).astype(BF16), v, tn, preferred_element_type=F32)
        sf_ref[c] = jnp.where(same_head, cf, 0.0)
        sb_ref[c] = jnp.where(same_head, cb, 0.0)
        return carry

    lax.fori_loop(0, nc, contrib, 0)

    def chain(ref, decay, order):
        state = jnp.zeros((rw, rw), F32)
        for c in order:
            u = ref[c]
            ref[c] = state
            state = decay * state + u

    chain(sf_ref, cdf, list(range(ncl, nc)) + list(range(ncl)))
    chain(sb_ref, cdb, list(range(nc - 1, ncl - 1, -1)) + list(range(ncl - 1, -1, -1)))

    ones_blk = jnp.where(same_head, 1.0, 0.0).astype(BF16)

    def outputs(chunks):
        rows = [c * c_len if isinstance(c, int) else pl.multiple_of(c * c_len, c_len) for c in chunks]
        qs =[ret_ref[pl.ds(r0, c_len), 0:rw] for r0 in rows]
        ks = [ret_ref[pl.ds(r0, c_len), rw:2 * rw] for r0 in rows]
        vs = [ret_ref[pl.ds(r0, c_len), 2 * rw:3 * rw] for r0 in rows]
        os_, scs = [], []
        for c, q, k in zip(chunks, qs, ks):
            qf = q.astype(F32)
            qi = jnp.concatenate([(qf * qwf).astype(BF16), (qf * qwb).astype(BF16)], axis=1)
            st = jnp.concatenate([sf_ref[c], sb_ref[c]], axis=0).astype(BF16)
            os_.append(jnp.dot(qi, st, preferred_element_type=F32))
            scs.append([lax.dot_general(jnp.where(lane_head == h, q, jnp.zeros_like(q)), k,
                                        (((1,), (1,)), ((), ())), preferred_element_type=F32)
                        for h in range(n_heads)])
        for i, v in enumerate(vs):
            for h in range(n_heads):
                p = (scs[i][h] * dm_ref[h]).astype(BF16)
                os_[i] = os_[i] + jnp.where(lane_head == h, jnp.dot(p, v, preferred_element_type=F32), 0.0)
        mss = []
        for o in os_:
            o2 = o * o
            hi = o2.astype(BF16)
            lo = (o2 - hi.astype(F32)).astype(BF16)
            mss.append((jnp.dot(hi, ones_blk, preferred_element_type=F32)
                        + jnp.dot(lo, ones_blk, preferred_element_type=F32)) * (1.0 / HEAD_DIM))
        for r0, o, ms in zip(rows, os_, mss):
            g = g_ref[pl.ds(r0, c_len), :]
            o_ref[pl.ds(r0, c_len), :] = (o * lax.rsqrt(ms + EPS) * (g * jax.nn.sigmoid(g))).astype(BF16)

    def output_pair(i, carry):
        outputs([2 * i, 2 * i + 1])
        return carry

    lax.fori_loop(0, nc // 2, output_pair, 0)
    if nc % 2:
        outputs([nc - 1])


def _ret_call(ret_in, g_in, dl_lane, n_lat):
    b, s, rw = g_in.shape
    nc = s // CHUNK
    return pl.pallas_call(
        functools.partial(_ret_kernel, n_lat_chunks=n_lat // CHUNK),
        grid=(b,),
        in_specs=[
            pl.BlockSpec((None, s, 3 * rw), lambda i: (i, 0, 0)),
            pl.BlockSpec((None, s, rw), lambda i: (i, 0, 0)),
            pl.BlockSpec((2, rw), lambda i: (0, 0)),
        ],
        out_specs=pl.BlockSpec((None, s, rw), lambda i: (i, 0, 0)),
        out_shape=jax.ShapeDtypeStruct((b, s, rw), BF16),
        scratch_shapes=[
            pltpu.VMEM((nc, rw, rw), F32),
            pltpu.VMEM((nc, rw, rw), F32),
            pltpu.VMEM((rw // HEAD_DIM, CHUNK, CHUNK), F32),
        ],
        compiler_params=_cparams(("parallel",)),
    )(ret_in, g_in, dl_lane)


def _attn_kernel(qt_ref, k_ref, vt_ref, kn_ref, *rest, group):
    o_ref, rhs_ref, sa_ref, sb_ref, pc_ref, m_ref, acc_ref, out_ref = rest[-8:]
    n_q_heads = qt_ref.shape[0] // HEAD_DIM
    n_chunks = vt_ref.shape[0]
    kc_len = vt_ref.shape[2]
    vrows = vt_ref.shape[1] // (n_q_heads // group)

    def scores(c, h):
        c0 = c * kc_len
        if not isinstance(c0, int):
            c0 = pl.multiple_of(c0, kc_len)
        return jnp.dot(k_ref[pl.ds(c0, kc_len), :], rhs_ref[h], preferred_element_type=F32)

    def p_times_v(c, h, p):
        kv = h // group
        return jnp.dot(vt_ref[c, kv * vrows:(kv + 1) * vrows, :], p, preferred_element_type=F32)

    kmax2 = jnp.max(kn_ref[...], axis=1, keepdims=True)
    for h in range(n_q_heads):
        qt = qt_ref[h * HEAD_DIM:(h + 1) * HEAD_DIM, :]
        zero = jnp.zeros_like(qt)
        rhs_ref[h] = jnp.concatenate([qt, zero] if h // group == 0 else [zero, qt], axis=0)
        sa_ref[h] = scores(0, h)
        qf = qt.astype(F32)
        qn2 = jnp.sum(qf * qf, axis=0, keepdims=True)
        m_ref[h:h + 1, :] = jnp.sqrt(qn2 * kmax2[h // group:h // group + 1, :]) * SCORE_BOUND_MARGIN
    acc_ref[...] = jnp.zeros(acc_ref.shape, F32)
    bounded = jnp.max(m_ref[...]) <= SCORE_BOUND_CAP

    def run_chunks(step):
        def pair(i, carry):
            step(2 * i, sa_ref, sb_ref)
            step(2 * i + 1, sb_ref, sa_ref)
            return carry

        lax.fori_loop(0, (n_chunks - 1) // 2, pair, 0)
        if (n_chunks - 1) % 2 == 1:
            step(n_chunks - 2, sa_ref, sb_ref)
            step(n_chunks - 1, sb_ref, None)
        else:
            step(n_chunks - 1, sa_ref, None)

    @pl.when(bounded)
    def _():
        lag = PV_LAG
        pc_ref[...] = jnp.zeros(pc_ref.shape, BF16)

        def step(c, cur_ref, nxt_ref):
            ps = [pc_ref[i] for i in range(lag)]
            c_prev = max(c - 1, 0) if isinstance(c, int) else jnp.maximum(c - 1, 0)
            for h in range(n_q_heads):
                if nxt_ref is not None:
                    nxt_ref[h] = scores(c + 1, h)
                if h < lag:
                    hp = n_q_heads - lag + h
                    acc_ref[hp] += p_times_v(c_prev, hp, ps[h])
                else:
                    acc_ref[h - lag] += p_times_v(c, h - lag, ps[h])
                ps.append(jnp.exp2(cur_ref[h] - m_ref[h:h + 1, :]).astype(BF16))
            for i in range(lag):
                pc_ref[i] = ps[n_q_heads + i]

        run_chunks(step)
        for i in range(lag):
            hp = n_q_heads - lag + i
            acc_ref[hp] += p_times_v(n_chunks - 1, hp, pc_ref[i])

    @pl.when(jnp.logical_not(bounded))
    def _():
        m_ref[...] = jnp.full(m_ref.shape, -jnp.inf, F32)

        def step(c, cur_ref, nxt_ref):
            for h in range(n_q_heads):
                if nxt_ref is not None:
                    nxt_ref[h] = scores(c + 1, h)
                s = cur_ref[h]
                m_old = m_ref[h:h + 1, :]
                m_new = jnp.maximum(m_old, jnp.max(s, axis=0, keepdims=True))
                alpha = jnp.exp2(m_old - m_new)
                m_ref[h:h + 1, :] = m_new
                p = jnp.exp2(s - m_new).astype(BF16)
                acc_ref[h] = alpha * acc_ref[h] + p_times_v(c, h, p)

        run_chunks(step)

    for h in range(n_q_heads):
        a = acc_ref[h]
        out_ref[h * HEAD_DIM:(h + 1) * HEAD_DIM, :] = a[:HEAD_DIM] / a[HEAD_DIM:HEAD_DIM + 1]
    o_ref[...] = out_ref[...].T.astype(BF16)


def _attn_call(qt, k, vt, kn, n_lat, with_ctx, tq_lat=512, tq_ctx=256):
    b, qw, s = qt.shape
    kw = k.shape[2]
    vw, kc_len = vt.shape[2], vt.shape[3]
    n_ctx = s - n_lat
    n_heads = qw // HEAD_DIM
    n_kv = kw // HEAD_DIM
    assert n_kv == 2 and n_lat % n_ctx == 0 and n_ctx % kc_len == 0
    tq_lat = min(tq_lat, n_lat)
    tq_ctx = min(tq_ctx, n_ctx)
    assert n_lat % tq_lat == 0 and n_ctx % tq_ctx == 0
    body = functools.partial(_attn_kernel, group=n_heads // n_kv)
    out_sds = jax.ShapeDtypeStruct((b, s, qw), BF16)

    def scratch(tq):
        return [pltpu.VMEM((n_heads, kw, tq), BF16),
                pltpu.VMEM((n_heads, kc_len, tq), F32), pltpu.VMEM((n_heads, kc_len, tq), F32),
                pltpu.VMEM((PV_LAG, kc_len, tq), BF16),
                pltpu.VMEM((n_heads, tq), F32), pltpu.VMEM((n_heads, vw // n_kv, tq), F32),
                pltpu.VMEM((qw, tq), F32)]

    att = pl.pallas_call(
        body,
        grid=(b, n_lat // tq_lat),
        in_specs=[
            pl.BlockSpec((None, qw, tq_lat), lambda i, j: (i, 0, j)),
            pl.BlockSpec((None, s, kw), lambda i, j: (i, 0, 0)),
            pl.BlockSpec((None, s // kc_len, vw, kc_len), lambda i, j: (i, 0, 0, 0)),
            pl.BlockSpec((None, n_kv, s), lambda i, j: (i, 0, 0)),
        ],
        out_specs=pl.BlockSpec((None, tq_lat, qw), lambda i, j: (i, j, 0)),
        out_shape=out_sds,
        scratch_shapes=scratch(tq_lat),
        compiler_params=_cparams(("parallel", "arbitrary")),
    )(qt, k, vt, kn)
    if not with_ctx:
        return att
    lat_tiles, lat_ctx = n_lat // tq_ctx, n_lat // n_ctx
    return pl.pallas_call(
        body,
        grid=(b, n_ctx // tq_ctx),
        in_specs=[
            pl.BlockSpec((None, qw, tq_ctx), lambda i, j: (i, 0, lat_tiles + j)),
            pl.BlockSpec((None, n_ctx, kw), lambda i, j: (i, lat_ctx, 0)),
            pl.BlockSpec((None, n_ctx // kc_len, vw, kc_len), lambda i, j: (i, lat_ctx, 0, 0)),
            pl.BlockSpec((None, n_kv, n_ctx), lambda i, j: (i, 0, lat_ctx)),
            pl.BlockSpec(memory_space=pl.ANY),
        ],
        out_specs=pl.BlockSpec((None, tq_ctx, qw), lambda i, j: (i, lat_tiles + j, 0)),
        out_shape=out_sds,
        input_output_aliases={4: 0},
        scratch_shapes=scratch(tq_ctx),
        compiler_params=_cparams(("parallel", "arbitrary")),
    )(qt, k, vt, kn, att)


def _lru_pre_kernel(cur_ref, prev_ref, next_ref, cw_ref, cb_ref, wg_ref, bg_ref, lam_ref,
                    af_ref, df_ref, ab_ref, db_ref, xs_ref, *, n_lat_chunks, n_chunks):
    c = pl.program_id(0)
    n_b, c_len, w = cur_ref.shape
    first = jnp.logical_or(c == 0, c == n_lat_chunks)
    last = jnp.logical_or(c == n_lat_chunks - 1, c == n_chunks - 1)
    xs_ref[0:8] = jnp.where(first, 0.0, jnp.swapaxes(prev_ref[...], 0, 1))
    xs_ref[8:8 + c_len] = jnp.swapaxes(cur_ref[...], 0, 1)
    xs_ref[8 + c_len:] = jnp.where(last, 0.0, jnp.swapaxes(next_ref[...], 0, 1))
    half_c_lam = (0.5 * LRU_C) * _log_sigmoid(lam_ref[...])
    ts = LRU_SUB_TOKENS

    def sub_chunk(i, carry):
        t0 = pl.multiple_of(i * ts, ts)

        def shifted(off):
            return xs_ref[pl.ds(8 + t0 + off, ts)]

        xr = (shifted(-2) * cw_ref[0:1, :] + shifted(-1) * cw_ref[1:2, :] + shifted(0) * cw_ref[2:3, :]
              + shifted(1) * cw_ref[3:4, :] + cb_ref[...]).reshape(ts * n_b, w)
        half_gates = jnp.dot(xr.astype(BF16), wg_ref[...], preferred_element_type=F32) + bg_ref[...]
        half_xr = 0.5 * xr
        for d, (a_ref, d_ref) in enumerate(((af_ref, df_ref), (ab_ref, db_ref))):
            t_r = jnp.tanh(half_gates[:, 2 * d * w:(2 * d + 1) * w])
            t_i = jnp.tanh(half_gates[:, (2 * d + 1) * w:(2 * d + 2) * w])
            half_c = half_c_lam[d:d + 1, :]
            log_a = half_c * t_r + half_c
            a = jnp.exp(log_a)
            om = (1.0 + a * a) * jnp.tanh(-log_a)
            drive = om * lax.rsqrt(jnp.maximum(om, 1e-30)) * (half_xr * t_i + half_xr)
            a_ref[pl.ds(t0, ts)] = a.reshape(ts, n_b, w)
            d_ref[pl.ds(t0, ts)] = drive.reshape(ts, n_b, w)
        return carry

    lax.fori_loop(0, c_len // ts, sub_chunk, 0)


def _lru_pre_call(lru_in, conv_w, conv_b, wg, bg, lam, n_lat):
    b, s, w2 = lru_in.shape
    w = w2 // 2
    nc = s // CHUNK
    hb = CHUNK // 8
    const = lambda shape: pl.BlockSpec(shape, lambda j: tuple(0 for _ in shape))
    out_spec = pl.BlockSpec((CHUNK, b, w), lambda j: (j, 0, 0))
    out_sds = jax.ShapeDtypeStruct((s, b, w), F32)
    return pl.pallas_call(
        functools.partial(_lru_pre_kernel, n_lat_chunks=n_lat // CHUNK, n_chunks=nc),
        grid=(nc,),
        in_specs=[
            pl.BlockSpec((b, CHUNK, w), lambda j: (0, j, 0)),
            pl.BlockSpec((b, 8, w), lambda j: (0, jnp.maximum(j * hb - 1, 0), 0)),
            pl.BlockSpec((b, 8, w), lambda j: (0, jnp.minimum((j + 1) * hb, s // 8 - 1), 0)),
            const(conv_w.shape), const(conv_b.shape), const(wg.shape), const(bg.shape),
            const(lam.shape),
        ],
        out_specs=[out_spec] * 4,
        out_shape=[out_sds] * 4,
        scratch_shapes=[pltpu.VMEM((CHUNK + 16, b, w), F32)],
        compiler_params=_cparams(("parallel",)),
    )(lru_in, lru_in, lru_in, conv_w, conv_b, wg, bg, lam)


def _scan_kernel(af_ref, df_ref, ab_ref, db_ref, hf_ref, hb_ref, sf_ref, sb_ref, tf_ref, tb_ref):
    tt = af_ref.shape[0]

    @pl.when(pl.program_id(0) == 0)
    def _():
        sf_ref[...] = jnp.zeros_like(sf_ref)
        sb_ref[...] = jnp.zeros_like(sb_ref)

    def step(t, carry):
        hf, hb = carry
        hf = af_ref[t] * hf + df_ref[t]
        tf_ref[t] = hf
        tb = tt - 1 - t
        hb = ab_ref[tb] * hb + db_ref[tb]
        tb_ref[tb] = hb
        return hf, hb

    hf, hb = lax.fori_loop(0, tt, step, (sf_ref[...], sb_ref[...]), unroll=8)
    sf_ref[...] = hf
    sb_ref[...] = hb
    hf_ref[...] = jnp.swapaxes(tf_ref[...], 0, 1)
    hb_ref[...] = jnp.swapaxes(tb_ref[...], 0, 1)


def _scan_call(af, df, ab, db, n_lat):
    s, b, w = af.shape
    nt = s // SCAN_TILE
    nlt = n_lat // SCAN_TILE
    fwd = pl.BlockSpec((SCAN_TILE, b, w), lambda i: ((i + nlt) % nt, 0, 0))
    bwd = pl.BlockSpec((SCAN_TILE, b, w), lambda i: (nt - 1 - i, 0, 0))
    fwd_out = pl.BlockSpec((b, SCAN_TILE, w), lambda i: (0, (i + nlt) % nt, 0))
    bwd_out = pl.BlockSpec((b, SCAN_TILE, w), lambda i: (0, nt - 1 - i, 0))
    sds = jax.ShapeDtypeStruct((b, s, w), F32)
    return pl.pallas_call(
        _scan_kernel,
        grid=(nt,),
        in_specs=[fwd, fwd, bwd, bwd],
        out_specs=[fwd_out, bwd_out],
        out_shape=[sds, sds],
        scratch_shapes=[pltpu.VMEM((b, w), F32), pltpu.VMEM((b, w), F32),
                        pltpu.VMEM((SCAN_TILE, b, w), F32), pltpu.VMEM((SCAN_TILE, b, w), F32)],
        compiler_params=_cparams(("arbitrary",)),
    )(af, df, ab, db)


def _row_splits(tm, n_sub):
    units = tm // 16
    assert tm % 16 == 0 and units >= n_sub
    cuts = [16 * ((units * i) // n_sub) for i in range(n_sub + 1)]
    return list(zip(cuts[:-1], cuts[1:]))


def _tail_kernel(x_ref, ret_ref, att_ref, hf_ref, hb_ref, lg_ref, mod_ref, wo_ref, g1_ref, b1_ref,
                 w1_ref, w2_ref, g2_ref, b2_ref, o_ref, x1_ref, u_ref, acc_ref, *, n_lat, alpha):
    tm = x_ref.shape[0]
    kf = pl.program_id(2)
    nk = pl.num_programs(2)
    subs = _row_splits(tm, TAIL_SUB_BLOCKS)
    rw = ret_ref.shape[1]
    aw = att_ref.shape[1]

    def ctx_rows(r0, r1):
        rows = pl.program_id(1) * tm + r0 + lax.broadcasted_iota(jnp.int32, (r1 - r0, 1), 0)
        return rows >= n_lat

    def mlp_part(r0, r1):
        h = jnp.maximum(jnp.dot(u_ref[r0:r1, :], w1_ref[...], preferred_element_type=F32), 0.0)
        return jnp.dot((h * h).astype(BF16), w2_ref[...], preferred_element_type=F32)

    @pl.when(kf == 0)
    def _():
        for r0, r1 in subs:
            is_ctx = ctx_rows(r0, r1)
            lru = ((hf_ref[r0:r1, :] + hb_ref[r0:r1, :]) * jax.nn.gelu(lg_ref[r0:r1, :])).astype(BF16)
            y = jnp.dot(ret_ref[r0:r1, :], wo_ref[0:rw, :], preferred_element_type=F32)
            y = y + jnp.dot(att_ref[r0:r1, :], wo_ref[rw:rw + aw, :], preferred_element_type=F32)
            y = y + jnp.dot(lru, wo_ref[rw + aw:, :], preferred_element_type=F32)
            x1 = _layer_norm(alpha * x_ref[r0:r1, :] + _row_mod(mod_ref, 2, is_ctx) * y,
                             g1_ref[...], b1_ref[...])
            x1_ref[r0:r1, :] = x1
            u_ref[r0:r1, :] = (x1 * (1.0 + _row_mod(mod_ref, 4, is_ctx))
                               + _row_mod(mod_ref, 3, is_ctx)).astype(BF16)
        for r0, r1 in subs:
            acc_ref[r0:r1, :] = mlp_part(r0, r1)

    @pl.when(jnp.logical_and(kf > 0, kf < nk - 1))
    def _():
        acc_ref[...] += mlp_part(0, tm)

    @pl.when(kf == nk - 1)
    def _():
        for r0, r1 in subs:
            a = acc_ref[r0:r1, :] + mlp_part(r0, r1)
            z = alpha * x1_ref[r0:r1, :] + _row_mod(mod_ref, 5, ctx_rows(r0, r1)) * a
            o_ref[r0:r1, :] = _layer_norm(z, g2_ref[...], b2_ref[...])


def _tail_call(x_all, ret, att, hf, hb, lru_in, mod, w_out, ln1_g, ln1_b, w1, w2, ln2_g, ln2_b,
               n_lat, n_rows, alpha, tm, tf=2048):
    b, _, d = x_all.shape
    rw, aw = ret.shape[2], att.shape[2]
    f = w1.shape[1]
    assert f // tf >= 2
    tok = lambda last: pl.BlockSpec((None, tm, last), lambda i, j, k: (i, j, 0))
    const = lambda shape: pl.BlockSpec(shape, lambda i, j, k: tuple(0 for _ in shape))
    return pl.pallas_call(
        functools.partial(_tail_kernel, n_lat=n_lat, alpha=alpha),
        grid=(b, n_rows // tm, f // tf),
        in_specs=[
            tok(d), tok(rw), tok(aw), tok(rw), tok(rw),
            pl.BlockSpec((None, tm, rw), lambda i, j, k: (i, j, 1)),
            pl.BlockSpec((None, 2, 6, d), lambda i, j, k: (i, 0, 0, 0)),
            const(w_out.shape), const(ln1_g.shape), const(ln1_b.shape),
            pl.BlockSpec((d, tf), lambda i, j, k: (0, k)),
            pl.BlockSpec((tf, d), lambda i, j, k: (k, 0)),
            const(ln2_g.shape), const(ln2_b.shape),
        ],
        out_specs=tok(d),
        out_shape=jax.ShapeDtypeStruct((b, n_rows, d), F32),
        scratch_shapes=[pltpu.VMEM((tm, d), F32), pltpu.VMEM((tm, d), BF16), pltpu.VMEM((tm, d), F32)],
        compiler_params=_cparams(("parallel", "parallel", "arbitrary")),
    )(x_all, ret, att, hf, hb, lru_in, mod, w_out, ln1_g, ln1_b, w1, w2, ln2_g, ln2_b)


def _rope_tables(n_lat, n_ctx):
    rows = n_lat // GRID_W
    row = jnp.repeat(jnp.arange(rows, dtype=F32), GRID_W)
    col = jnp.tile(jnp.arange(GRID_W, dtype=F32), rows)
    n_freq = HEAD_DIM // 4
    inv = ROPE_THETA ** (-jnp.arange(n_freq, dtype=F32) / n_freq)
    ang = jnp.concatenate([row[:, None] * inv, col[:, None] * inv], axis=-1)
    cos = jnp.concatenate([jnp.cos(ang), jnp.ones((n_ctx, HEAD_DIM // 2), F32)], axis=0)
    sin = jnp.concatenate([jnp.sin(ang), jnp.zeros((n_ctx, HEAD_DIM // 2), F32)], axis=0)
    return cos.T, sin.T


def _block_diag(w):
    k, c = w.shape[-3], w.shape[-2]
    eye = jnp.eye(k, dtype=w.dtype)
    bd = jnp.einsum('...kce,kj->...kcje', w, eye)
    return bd.reshape(*w.shape[:-3], k * c, k * c)


def kernel(x, c, ctx, c_ctx, w_ada, b_ada, w_in, ret_decay_logit, attn_q_gain, attn_k_gain,
           lru_conv_w, lru_conv_b, lru_w_a, lru_b_a, lru_w_x, lru_b_x, lru_lambda,
           w_out, ln1_g, ln1_b, w_ff1, w_ff2, ln2_g, ln2_b):
    b, n_lat, d = x.shape
    n_ctx = ctx.shape[1]
    depth = w_in.shape[0]
    s = n_lat + n_ctx
    rw, aw, kw = d // 4, d // 2, d // 8
    alpha = (2.0 * depth) ** 0.25
    assert n_lat % CHUNK == 0 and n_ctx % CHUNK == 0 and d == 16 * HEAD_DIM

    pad = (-(b + 1)) % 8
    s_in = jnp.concatenate([c, c_ctx[None, :], jnp.zeros((pad, d), F32)], axis=0)
    mods = _ada_call(s_in, w_ada, b_ada)
    mod_lat = mods[:, :b].reshape(depth, b, 1, 6, d)
    mod_ctx = jnp.broadcast_to(mods[:, b].reshape(depth, 1, 1, 6, d), (depth, b, 1, 6, d))
    mod_all = jnp.concatenate([mod_lat, mod_ctx], axis=2)

    o_aq = 4 * rw
    o_lx = o_aq + aw + 2 * kw
    wn = jnp.concatenate([w_in[:, :, :o_aq], w_in[:, :, o_lx:]], axis=2).astype(BF16)
    wt = jnp.swapaxes(w_in[:, :, o_aq:o_lx], 1, 2).astype(BF16)
    wg = (0.5 * jnp.concatenate([_block_diag(lru_w_a[:, 0]), _block_diag(lru_w_x[:, 0]),
                                 _block_diag(lru_w_a[:, 1]), _block_diag(lru_w_x[:, 1])],
                                axis=-1)).astype(BF16)
    bg = 0.5 * jnp.concatenate([lru_b_a[:, 0], lru_b_x[:, 0], lru_b_a[:, 1], lru_b_x[:, 1]],
                               axis=-1)[:, None, :]
    w_out_b = w_out.astype(BF16)
    w1_b = w_ff1.astype(BF16)
    w2_b = w_ff2.astype(BF16)
    dl_lane = jnp.repeat(ret_decay_logit, HEAD_DIM, axis=-1)
    cos_t, sin_t = _rope_tables(n_lat, n_ctx)

    def tail_tile(rows):
        return rows // 8 if rows % 64 == 0 else CHUNK

    xa = (x, ctx)
    for l in range(depth):
        need_ctx = l < depth - 1
        mod = mod_all[l]
        outs = _inproj_call(xa, mod, wn[l], wt[l], cos_t, sin_t,
                            attn_q_gain[l][:, None], attn_k_gain[l][:, None], n_lat)
        ret_in, g_in, lru_in, qt, k, vt, kn = outs[:7]
        if l == 0:
            xa = outs[7]
        ret = _ret_call(ret_in, g_in, dl_lane[l], n_lat)
        att = _attn_call(qt, k, vt, kn, n_lat, need_ctx)
        af, df, ab, db = _lru_pre_call(lru_in, lru_conv_w[l], lru_conv_b[l][None, :], wg[l], bg[l],
                                       lru_lambda[l], n_lat)
        hf, hb = _scan_call(af, df, ab, db, n_lat)
        n_rows = s if need_ctx else n_lat
        xa = _tail_call(xa, ret, att, hf, hb, lru_in, mod,
                        w_out_b[l], ln1_g[l][None, :], ln1_b[l][None, :], w1_b[l], w2_b[l],
                        ln2_g[l][None, :], ln2_b[l][None, :], n_lat, n_rows, alpha, tail_tile(n_rows))
    return xa
```

```python
import functools
import math

import jax
import jax.numpy as jnp
from jax import lax
from jax.experimental import pallas as pl
from jax.experimental.pallas import tpu as pltpu

F32 = jnp.float32
BF16 = jnp.bfloat16

HEAD_DIM = 64
GRID_W = 64
ROPE_THETA = 10000.0
LRU_C = 8.0
LRU_BLOCKS = 4
EPS = 1e-6
LOG2E = 1.4426950408889634

ONES_ROWS = 16
SCORE_BOUND_MARGIN = 1.0 + 2.0 ** -7
SCORE_BOUND_CAP = 60.0
PV_LAG = 2
TAIL_SUB_BLOCKS = 2
CHUNK = 256
RET_GROUP = 4
SCAN_TILE = 128
LRU_SUB_TOKENS = 32
VMEM_LIMIT = 56 * 1024 * 1024


def _cparams(sem):
    return pltpu.CompilerParams(dimension_semantics=sem, vmem_limit_bytes=VMEM_LIMIT)


def _log_sigmoid(x):
    return jnp.minimum(x, 0.0) - jnp.log1p(jnp.exp(-jnp.abs(x)))


def _layer_norm(z, g, b):
    mu = jnp.mean(z, axis=-1, keepdims=True)
    zc = z - mu
    var = jnp.mean(zc * zc, axis=-1, keepdims=True)
    return zc * lax.rsqrt(var + EPS) * g + b


def _row_mod(mod_ref, idx, is_ctx):
    return jnp.where(is_ctx, mod_ref[1, idx:idx + 1, :], mod_ref[0, idx:idx + 1, :])


def _ctx_rows(tile_idx, tm, n_lat):
    rows = tile_idx * tm + lax.broadcasted_iota(jnp.int32, (tm, 1), 0)
    return rows >= n_lat


def _ada_kernel(s_ref, w_ref, b_ref, o_ref):
    s = s_ref[...]
    s = s * jax.nn.sigmoid(s)
    o_ref[...] = jnp.dot(s.astype(BF16), w_ref[...].astype(BF16),
                         preferred_element_type=F32) + b_ref[...]


def _ada_call(s_in, w_ada, b_ada):
    depth, d, d6 = w_ada.shape
    rows = s_in.shape[0]
    tn = d6 // 4
    return pl.pallas_call(
        _ada_kernel,
        grid=(depth, d6 // tn),
        in_specs=[
            pl.BlockSpec((rows, d), lambda l, j: (0, 0)),
            pl.BlockSpec((None, d, tn), lambda l, j: (l, 0, j)),
            pl.BlockSpec((None, 1, tn), lambda l, j: (l, 0, j)),
        ],
        out_specs=pl.BlockSpec((None, rows, tn), lambda l, j: (l, 0, j)),
        out_shape=jax.ShapeDtypeStruct((depth, rows, d6), F32),
        compiler_params=_cparams(("parallel", "parallel")),
    )(s_in, w_ada, b_ada.reshape(depth, 1, d6))


def _inproj_kernel(x_ref, *refs, n_lat, joins_streams):
    if joins_streams:
        ctx_ref, refs, xa_ref = refs[0], refs[1:-1], refs[-1]
    (mod_ref, wn_ref, wt_ref, cos_ref, sin_ref, qg_ref, kg_ref,
     ret_ref, g_ref, lru_ref, qt_ref, k_ref, vt_ref, kn_ref) = refs
    tm = x_ref.shape[0]
    is_ctx = _ctx_rows(pl.program_id(1), tm, n_lat)
    if joins_streams:
        x = jnp.where(pl.program_id(1) * tm >= n_lat, ctx_ref[...], x_ref[...])
        xa_ref[...] = x
    else:
        x = x_ref[...]
    u = (x * (1.0 + _row_mod(mod_ref, 1, is_ctx)) + _row_mod(mod_ref, 0, is_ctx)).astype(BF16)

    pt = lax.dot_general(wt_ref[...], u, (((1,), (1,)), ((), ())), preferred_element_type=F32)

    pn = jnp.dot(u, wn_ref[...], preferred_element_type=F32)
    rw = g_ref.shape[1]
    ret_ref[:, 0:rw] = pn[:, 0:rw].astype(BF16)
    ret_ref[:, rw:2 * rw] = (pn[:, rw:2 * rw] * HEAD_DIM ** -0.5).astype(BF16)
    ret_ref[:, 2 * rw:3 * rw] = pn[:, 2 * rw:3 * rw].astype(BF16)
    g_ref[...] = pn[:, 3 * rw:4 * rw]
    lru_ref[...] = pn[:, 4 * rw:]

    cos = cos_ref[...]
    sin = sin_ref[...]
    half = HEAD_DIM // 2

    def norm_rope(t, gain):
        ms = jnp.mean(t * t, axis=0, keepdims=True)
        t = t * lax.rsqrt(ms + EPS) * gain
        x1, x2 = t[:half], t[half:]
        return jnp.concatenate([x1 * cos - x2 * sin, x1 * sin + x2 * cos], axis=0)

    qw = qt_ref.shape[0]
    kw = k_ref.shape[1]
    qscale = HEAD_DIM ** -0.5 * LOG2E
    for h in range(qw // HEAD_DIM):
        r = h * HEAD_DIM
        qt_ref[r:r + HEAD_DIM, :] = (norm_rope(pt[r:r + HEAD_DIM], qg_ref[...]) * qscale).astype(BF16)
    kt = jnp.concatenate(
        [norm_rope(pt[qw + h * HEAD_DIM:qw + (h + 1) * HEAD_DIM], kg_ref[...])
         for h in range(kw // HEAD_DIM)], axis=0)
    k_ref[...] = kt.T.astype(BF16)
    kf = kt.astype(BF16).astype(F32)
    for h in range(kw // HEAD_DIM):
        kh = kf[h * HEAD_DIM:(h + 1) * HEAD_DIM]
        kn_ref[h:h + 1, :] = jnp.sum(kh * kh, axis=0, keepdims=True)
    ones = jnp.ones((ONES_ROWS, tm), BF16)
    vrows = HEAD_DIM + ONES_ROWS
    for h in range(kw // HEAD_DIM):
        v0 = qw + kw + h * HEAD_DIM
        vt_ref[h * vrows:h * vrows + HEAD_DIM, :] = pt[v0:v0 + HEAD_DIM].astype(BF16)
        vt_ref[h * vrows + HEAD_DIM:(h + 1) * vrows, :] = ones


def _inproj_call(streams, mod, wn, wt, cos_t, sin_t, qg, kg, n_lat, tm=256):
    joins = isinstance(streams, tuple)
    b, _, d = streams[0].shape if joins else streams.shape
    s = n_lat + streams[1].shape[1] if joins else streams.shape[1]
    assert n_lat % tm == 0 and s % tm == 0
    nlt = n_lat // tm
    rw, qw, kw = d // 4, d // 2, d // 8
    vw = (kw // HEAD_DIM) * (HEAD_DIM + ONES_ROWS)
    grid = (b, s // tm)
    tok = lambda shape_last: pl.BlockSpec((None, tm, shape_last), lambda i, j: (i, j, 0))
    tr = lambda rows: pl.BlockSpec((None, rows, tm), lambda i, j: (i, 0, j))
    const = lambda shape: pl.BlockSpec(shape, lambda i, j: tuple(0 for _ in shape))
    if joins:
        x_specs = [pl.BlockSpec((None, tm, d), lambda i, j: (i, jnp.minimum(j, nlt - 1), 0)),
                   pl.BlockSpec((None, tm, d), lambda i, j: (i, jnp.maximum(j - nlt, 0), 0))]
        x_args = list(streams)
    else:
        x_specs, x_args = [tok(d)], [streams]
    return pl.pallas_call(
        functools.partial(_inproj_kernel, n_lat=n_lat, joins_streams=joins),
        grid=grid,
        in_specs=x_specs + [
            pl.BlockSpec((None, 2, 6, d), lambda i, j: (i, 0, 0, 0)),
            const(wn.shape), const(wt.shape),
            pl.BlockSpec((HEAD_DIM // 2, tm), lambda i, j: (0, j)),
            pl.BlockSpec((HEAD_DIM // 2, tm), lambda i, j: (0, j)),
            const(qg.shape), const(kg.shape),
        ],
        out_specs=[tok(3 * rw), tok(rw), tok(2 * rw), tr(qw), tok(kw),
                   pl.BlockSpec((None, None, vw, tm), lambda i, j: (i, j, 0, 0)),
                   tr(kw // HEAD_DIM)] + ([tok(d)] if joins else []),
        out_shape=[
            jax.ShapeDtypeStruct((b, s, 3 * rw), BF16),
            jax.ShapeDtypeStruct((b, s, rw), F32),
            jax.ShapeDtypeStruct((b, s, 2 * rw), F32),
            jax.ShapeDtypeStruct((b, qw, s), BF16),
            jax.ShapeDtypeStruct((b, s, kw), BF16),
            jax.ShapeDtypeStruct((b, s // tm, vw, tm), BF16),
            jax.ShapeDtypeStruct((b, kw // HEAD_DIM, s), F32),
        ] + ([jax.ShapeDtypeStruct((b, s, d), F32)] if joins else []),
        compiler_params=_cparams(("parallel", "parallel")),
    )(*x_args, mod, wn, wt, cos_t, sin_t, qg, kg)


def _ret_kernel(ret_ref, g_ref, dl_ref, o_ref, sf_ref, sb_ref, dm_ref, *, n_lat_chunks):
    c_len = CHUNK
    s_len, rw = g_ref.shape
    nc = s_len // c_len
    ncl = n_lat_chunks
    n_heads = rw // HEAD_DIM

    lg = _log_sigmoid(dl_ref[...])
    lgf, lgb = lg[0:1], lg[1:2]
    pos = lax.broadcasted_iota(jnp.int32, (c_len, 1), 0).astype(F32)
    kwf = jnp.exp(lgf * (c_len - 1.0 - pos))
    kwb = jnp.exp(lgb * pos)
    qwf = jnp.exp(lgf * (pos + 1.0))
    qwb = jnp.exp(lgb * (c_len - pos))
    cdf = jnp.exp(lgf * c_len)
    cdb = jnp.exp(lgb * c_len)
    row_head = lax.broadcasted_iota(jnp.int32, (rw, rw), 0) // HEAD_DIM
    col_head = lax.broadcasted_iota(jnp.int32, (rw, rw), 1) // HEAD_DIM
    same_head = row_head == col_head
    lane_head = lax.broadcasted_iota(jnp.int32, (1, rw), 1) // HEAD_DIM

    ii = lax.broadcasted_iota(jnp.int32, (c_len, c_len), 0)
    jj = lax.broadcasted_iota(jnp.int32, (c_len, c_len), 1)
    dij = (ii - jj).astype(F32)
    for h in range(n_heads):
        lf = lgf[:, h * HEAD_DIM:h * HEAD_DIM + 1]
        lb = lgb[:, h * HEAD_DIM:h * HEAD_DIM + 1]
        dm_ref[h] = jnp.exp(jnp.where(dij >= 0.0, lf * dij, -lb * dij))

    def contrib(chunks):
        tn = (((0,), (0,)), ((), ()))
        prods = []
        for c in chunks:
            r0 = c * c_len if isinstance(c, int) else pl.multiple_of(c * c_len, c_len)
            k = ret_ref[pl.ds(r0, c_len), rw:2 * rw].astype(F32)
            v = ret_ref[pl.ds(r0, c_len), 2 * rw:3 * rw]
            prods.append((lax.dot_general((k * kwf).astype(BF16), v, tn, preferred_element_type=F32),
                          lax.dot_general((k * kwb).astype(BF16), v, tn, preferred_element_type=F32)))
        for c, (cf, cb) in zip(chunks, prods):
            sf_ref[c] = jnp.where(same_head, cf, 0.0)
            sb_ref[c] = jnp.where(same_head, cb, 0.0)

    def contrib_group(i, carry):
        contrib([RET_GROUP * i + j for j in range(RET_GROUP)])
        return carry

    lax.fori_loop(0, nc // RET_GROUP, contrib_group, 0)
    if nc % RET_GROUP:
        contrib(list(range(nc - nc % RET_GROUP, nc)))

    def chain(ref, decay, order):
        state = jnp.zeros((rw, rw), F32)
        for c in order:
            u = ref[c]
            ref[c] = state
            state = decay * state + u

    chain(sf_ref, cdf, list(range(ncl, nc)) + list(range(ncl)))
    chain(sb_ref, cdb, list(range(nc - 1, ncl - 1, -1)) + list(range(ncl - 1, -1, -1)))

    ones_blk = jnp.where(same_head, 1.0, 0.0).astype(BF16)

    def outputs(chunks):
        rows = [c * c_len if isinstance(c, int) else pl.multiple_of(c * c_len, c_len) for c in chunks]
        qs =[ret_ref[pl.ds(r0, c_len), 0:rw] for r0 in rows]
        ks = [ret_ref[pl.ds(r0, c_len), rw:2 * rw] for r0 in rows]
        vs = [ret_ref[pl.ds(r0, c_len), 2 * rw:3 * rw] for r0 in rows]
        os_, scs = [], []
        for c, q, k in zip(chunks, qs, ks):
            qf = q.astype(F32)
            qi = jnp.concatenate([(qf * qwf).astype(BF16), (qf * qwb).astype(BF16)], axis=1)
            st = jnp.concatenate([sf_ref[c], sb_ref[c]], axis=0).astype(BF16)
            os_.append(jnp.dot(qi, st, preferred_element_type=F32))
            scs.append([lax.dot_general(jnp.where(lane_head == h, q, jnp.zeros_like(q)), k,
                                        (((1,), (1,)), ((), ())), preferred_element_type=F32)
                        for h in range(n_heads)])
        for i, v in enumerate(vs):
            for h in range(n_heads):
                p = (scs[i][h] * dm_ref[h]).astype(BF16)
                os_[i] = os_[i] + jnp.where(lane_head == h, jnp.dot(p, v, preferred_element_type=F32), 0.0)
        mss = []
        for o in os_:
            o2 = o * o
            hi = o2.astype(BF16)
            lo = (o2 - hi.astype(F32)).astype(BF16)
            mss.append((jnp.dot(hi, ones_blk, preferred_element_type=F32)
                        + jnp.dot(lo, ones_blk, preferred_element_type=F32)) * (1.0 / HEAD_DIM))
        for r0, o, ms in zip(rows, os_, mss):
            g = g_ref[pl.ds(r0, c_len), :]
            o_ref[pl.ds(r0, c_len), :] = (o * lax.rsqrt(ms + EPS) * (g * jax.nn.sigmoid(g))).astype(BF16)

    def output_group(i, carry):
        outputs([RET_GROUP * i + j for j in range(RET_GROUP)])
        return carry

    lax.fori_loop(0, nc // RET_GROUP, output_group, 0)
    if nc % RET_GROUP:
        outputs(list(range(nc - nc % RET_GROUP, nc)))


def _ret_call(ret_in, g_in, dl_lane, n_lat):
    b, s, rw = g_in.shape
    nc = s // CHUNK
    return pl.pallas_call(
        functools.partial(_ret_kernel, n_lat_chunks=n_lat // CHUNK),
        grid=(b,),
        in_specs=[
            pl.BlockSpec((None, s, 3 * rw), lambda i: (i, 0, 0)),
            pl.BlockSpec((None, s, rw), lambda i: (i, 0, 0)),
            pl.BlockSpec((2, rw), lambda i: (0, 0)),
        ],
        out_specs=pl.BlockSpec((None, s, rw), lambda i: (i, 0, 0)),
        out_shape=jax.ShapeDtypeStruct((b, s, rw), BF16),
        scratch_shapes=[
            pltpu.VMEM((nc, rw, rw), F32),
            pltpu.VMEM((nc, rw, rw), F32),
            pltpu.VMEM((rw // HEAD_DIM, CHUNK, CHUNK), F32),
        ],
        compiler_params=_cparams(("parallel",)),
    )(ret_in, g_in, dl_lane)


def _attn_kernel(qt_ref, k_ref, vt_ref, kn_ref, *rest, group):
    o_ref, rhs_ref, sa_ref, sb_ref, pc_ref, m_ref, acc_ref, out_ref = rest[-8:]
    n_q_heads = qt_ref.shape[0] // HEAD_DIM
    n_chunks = vt_ref.shape[0]
    kc_len = vt_ref.shape[2]
    vrows = vt_ref.shape[1] // (n_q_heads // group)

    def scores(c, h):
        c0 = c * kc_len
        if not isinstance(c0, int):
            c0 = pl.multiple_of(c0, kc_len)
        return jnp.dot(k_ref[pl.ds(c0, kc_len), :], rhs_ref[h], preferred_element_type=F32)

    def p_times_v(c, h, p):
        kv = h // group
        return jnp.dot(vt_ref[c, kv * vrows:(kv + 1) * vrows, :], p, preferred_element_type=F32)

    kmax2 = jnp.max(kn_ref[...], axis=1, keepdims=True)
    for h in range(n_q_heads):
        qt = qt_ref[h * HEAD_DIM:(h + 1) * HEAD_DIM, :]
        zero = jnp.zeros_like(qt)
        rhs_ref[h] = jnp.concatenate([qt, zero] if h // group == 0 else [zero, qt], axis=0)
        sa_ref[h] = scores(0, h)
        qf = qt.astype(F32)
        qn2 = jnp.sum(qf * qf, axis=0, keepdims=True)
        m_ref[h:h + 1, :] = jnp.sqrt(qn2 * kmax2[h // group:h // group + 1, :]) * SCORE_BOUND_MARGIN
    acc_ref[...] = jnp.zeros(acc_ref.shape, F32)
    bounded = jnp.max(m_ref[...]) <= SCORE_BOUND_CAP

    def run_chunks(step):
        def pair(i, carry):
            step(2 * i, sa_ref, sb_ref)
            step(2 * i + 1, sb_ref, sa_ref)
            return carry

        lax.fori_loop(0, (n_chunks - 1) // 2, pair, 0)
        if (n_chunks - 1) % 2 == 1:
            step(n_chunks - 2, sa_ref, sb_ref)
            step(n_chunks - 1, sb_ref, None)
        else:
            step(n_chunks - 1, sa_ref, None)

    @pl.when(bounded)
    def _():
        lag = PV_LAG
        pc_ref[...] = jnp.zeros(pc_ref.shape, BF16)

        def step(c, cur_ref, nxt_ref):
            ps = [pc_ref[i] for i in range(lag)]
            c_prev = max(c - 1, 0) if isinstance(c, int) else jnp.maximum(c - 1, 0)
            for h in range(n_q_heads):
                if nxt_ref is not None:
                    nxt_ref[h] = scores(c + 1, h)
                if h < lag:
                    hp = n_q_heads - lag + h
                    acc_ref[hp] += p_times_v(c_prev, hp, ps[h])
                else:
                    acc_ref[h - lag] += p_times_v(c, h - lag, ps[h])
                ps.append(jnp.exp2(cur_ref[h] - m_ref[h:h + 1, :]).astype(BF16))
            for i in range(lag):
                pc_ref[i] = ps[n_q_heads + i]

        run_chunks(step)
        for i in range(lag):
            hp = n_q_heads - lag + i
            acc_ref[hp] += p_times_v(n_chunks - 1, hp, pc_ref[i])

    @pl.when(jnp.logical_not(bounded))
    def _():
        m_ref[...] = jnp.full(m_ref.shape, -jnp.inf, F32)

        def step(c, cur_ref, nxt_ref):
            for h in range(n_q_heads):
                if nxt_ref is not None:
                    nxt_ref[h] = scores(c + 1, h)
                s = cur_ref[h]
                m_old = m_ref[h:h + 1, :]
                m_new = jnp.maximum(m_old, jnp.max(s, axis=0, keepdims=True))
                alpha = jnp.exp2(m_old - m_new)
                m_ref[h:h + 1, :] = m_new
                p = jnp.exp2(s - m_new).astype(BF16)
                acc_ref[h] = alpha * acc_ref[h] + p_times_v(c, h, p)

        run_chunks(step)

    for h in range(n_q_heads):
        a = acc_ref[h]
        out_ref[h * HEAD_DIM:(h + 1) * HEAD_DIM, :] = a[:HEAD_DIM] / a[HEAD_DIM:HEAD_DIM + 1]
    o_ref[...] = out_ref[...].T.astype(BF16)


def _attn_call(qt, k, vt, kn, n_lat, with_ctx, tq_lat=512, tq_ctx=256):
    b, qw, s = qt.shape
    kw = k.shape[2]
    vw, kc_len = vt.shape[2], vt.shape[3]
    n_ctx = s - n_lat
    n_heads = qw // HEAD_DIM
    n_kv = kw // HEAD_DIM
    assert n_kv == 2 and n_lat % n_ctx == 0 and n_ctx % kc_len == 0
    tq_lat = min(tq_lat, n_lat)
    tq_ctx = min(tq_ctx, n_ctx)
    assert n_lat % tq_lat == 0 and n_ctx % tq_ctx == 0
    body = functools.partial(_attn_kernel, group=n_heads // n_kv)
    out_sds = jax.ShapeDtypeStruct((b, s, qw), BF16)

    def scratch(tq):
        return [pltpu.VMEM((n_heads, kw, tq), BF16),
                pltpu.VMEM((n_heads, kc_len, tq), F32), pltpu.VMEM((n_heads, kc_len, tq), F32),
                pltpu.VMEM((PV_LAG, kc_len, tq), BF16),
                pltpu.VMEM((n_heads, tq), F32), pltpu.VMEM((n_heads, vw // n_kv, tq), F32),
                pltpu.VMEM((qw, tq), F32)]

    att = pl.pallas_call(
        body,
        grid=(b, n_lat // tq_lat),
        in_specs=[
            pl.BlockSpec((None, qw, tq_lat), lambda i, j: (i, 0, j)),
            pl.BlockSpec((None, s, kw), lambda i, j: (i, 0, 0)),
            pl.BlockSpec((None, s // kc_len, vw, kc_len), lambda i, j: (i, 0, 0, 0)),
            pl.BlockSpec((None, n_kv, s), lambda i, j: (i, 0, 0)),
        ],
        out_specs=pl.BlockSpec((None, tq_lat, qw), lambda i, j: (i, j, 0)),
        out_shape=out_sds,
        scratch_shapes=scratch(tq_lat),
        compiler_params=_cparams(("parallel", "arbitrary")),
    )(qt, k, vt, kn)
    if not with_ctx:
        return att
    lat_tiles, lat_ctx = n_lat // tq_ctx, n_lat // n_ctx
    return pl.pallas_call(
        body,
        grid=(b, n_ctx // tq_ctx),
        in_specs=[
            pl.BlockSpec((None, qw, tq_ctx), lambda i, j: (i, 0, lat_tiles + j)),
            pl.BlockSpec((None, n_ctx, kw), lambda i, j: (i, lat_ctx, 0)),
            pl.BlockSpec((None, n_ctx // kc_len, vw, kc_len), lambda i, j: (i, lat_ctx, 0, 0)),
            pl.BlockSpec((None, n_kv, n_ctx), lambda i, j: (i, 0, lat_ctx)),
            pl.BlockSpec(memory_space=pl.ANY),
        ],
        out_specs=pl.BlockSpec((None, tq_ctx, qw), lambda i, j: (i, lat_tiles + j, 0)),
        out_shape=out_sds,
        input_output_aliases={4: 0},
        scratch_shapes=scratch(tq_ctx),
        compiler_params=_cparams(("parallel", "arbitrary")),
    )(qt, k, vt, kn, att)


def _lru_pre_kernel(cur_ref, prev_ref, next_ref, cw_ref, cb_ref, wg_ref, bg_ref, lam_ref,
                    af_ref, df_ref, ab_ref, db_ref, xs_ref, *, n_lat_chunks, n_chunks):
    c = pl.program_id(0)
    n_b, c_len, w = cur_ref.shape
    first = jnp.logical_or(c == 0, c == n_lat_chunks)
    last = jnp.logical_or(c == n_lat_chunks - 1, c == n_chunks - 1)
    xs_ref[0:8] = jnp.where(first, 0.0, jnp.swapaxes(prev_ref[...], 0, 1))
    xs_ref[8:8 + c_len] = jnp.swapaxes(cur_ref[...], 0, 1)
    xs_ref[8 + c_len:] = jnp.where(last, 0.0, jnp.swapaxes(next_ref[...], 0, 1))
    half_c_lam = (0.5 * LRU_C) * _log_sigmoid(lam_ref[...])
    ts = LRU_SUB_TOKENS

    def sub_chunk(i, carry):
        t0 = pl.multiple_of(i * ts, ts)

        def shifted(off):
            return xs_ref[pl.ds(8 + t0 + off, ts)]

        xr = (shifted(-2) * cw_ref[0:1, :] + shifted(-1) * cw_ref[1:2, :] + shifted(0) * cw_ref[2:3, :]
              + shifted(1) * cw_ref[3:4, :] + cb_ref[...]).reshape(ts * n_b, w)
        half_gates = jnp.dot(xr.astype(BF16), wg_ref[...], preferred_element_type=F32) + bg_ref[...]
        half_xr = 0.5 * xr
        for d, (a_ref, d_ref) in enumerate(((af_ref, df_ref), (ab_ref, db_ref))):
            t_r = jnp.tanh(half_gates[:, 2 * d * w:(2 * d + 1) * w])
            t_i = jnp.tanh(half_gates[:, (2 * d + 1) * w:(2 * d + 2) * w])
            half_c = half_c_lam[d:d + 1, :]
            log_a = half_c * t_r + half_c
            a = jnp.exp(log_a)
            om = (1.0 + a * a) * jnp.tanh(-log_a)
            drive = om * lax.rsqrt(jnp.maximum(om, 1e-30)) * (half_xr * t_i + half_xr)
            a_ref[pl.ds(t0, ts)] = a.reshape(ts, n_b, w)
            d_ref[pl.ds(t0, ts)] = drive.reshape(ts, n_b, w)
        return carry

    lax.fori_loop(0, c_len // ts, sub_chunk, 0)


def _lru_pre_call(lru_in, conv_w, conv_b, wg, bg, lam, n_lat):
    b, s, w2 = lru_in.shape
    w = w2 // 2
    nc = s // CHUNK
    hb = CHUNK // 8
    const = lambda shape: pl.BlockSpec(shape, lambda j: tuple(0 for _ in shape))
    out_spec = pl.BlockSpec((CHUNK, b, w), lambda j: (j, 0, 0))
    out_sds = jax.ShapeDtypeStruct((s, b, w), F32)
    return pl.pallas_call(
        functools.partial(_lru_pre_kernel, n_lat_chunks=n_lat // CHUNK, n_chunks=nc),
        grid=(nc,),
        in_specs=[
            pl.BlockSpec((b, CHUNK, w), lambda j: (0, j, 0)),
            pl.BlockSpec((b, 8, w), lambda j: (0, jnp.maximum(j * hb - 1, 0), 0)),
            pl.BlockSpec((b, 8, w), lambda j: (0, jnp.minimum((j + 1) * hb, s // 8 - 1), 0)),
            const(conv_w.shape), const(conv_b.shape), const(wg.shape), const(bg.shape),
            const(lam.shape),
        ],
        out_specs=[out_spec] * 4,
        out_shape=[out_sds] * 4,
        scratch_shapes=[pltpu.VMEM((CHUNK + 16, b, w), F32)],
        compiler_params=_cparams(("parallel",)),
    )(lru_in, lru_in, lru_in, conv_w, conv_b, wg, bg, lam)


def _scan_kernel(af_ref, df_ref, ab_ref, db_ref, hf_ref, hb_ref, sf_ref, sb_ref, tf_ref, tb_ref):
    tt = af_ref.shape[0]

    @pl.when(pl.program_id(0) == 0)
    def _():
        sf_ref[...] = jnp.zeros_like(sf_ref)
        sb_ref[...] = jnp.zeros_like(sb_ref)

    def step(t, carry):
        hf, hb = carry
        hf = af_ref[t] * hf + df_ref[t]
        tf_ref[t] = hf
        tb = tt - 1 - t
        hb = ab_ref[tb] * hb + db_ref[tb]
        tb_ref[tb] = hb
        return hf, hb

    hf, hb = lax.fori_loop(0, tt, step, (sf_ref[...], sb_ref[...]), unroll=8)
    sf_ref[...] = hf
    sb_ref[...] = hb
    hf_ref[...] = jnp.swapaxes(tf_ref[...], 0, 1)
    hb_ref[...] = jnp.swapaxes(tb_ref[...], 0, 1)


def _scan_call(af, df, ab, db, n_lat):
    s, b, w = af.shape
    nt = s // SCAN_TILE
    nlt = n_lat // SCAN_TILE
    fwd = pl.BlockSpec((SCAN_TILE, b, w), lambda i: ((i + nlt) % nt, 0, 0))
    bwd = pl.BlockSpec((SCAN_TILE, b, w), lambda i: (nt - 1 - i, 0, 0))
    fwd_out = pl.BlockSpec((b, SCAN_TILE, w), lambda i: (0, (i + nlt) % nt, 0))
    bwd_out = pl.BlockSpec((b, SCAN_TILE, w), lambda i: (0, nt - 1 - i, 0))
    sds = jax.ShapeDtypeStruct((b, s, w), F32)
    return pl.pallas_call(
        _scan_kernel,
        grid=(nt,),
        in_specs=[fwd, fwd, bwd, bwd],
        out_specs=[fwd_out, bwd_out],
        out_shape=[sds, sds],
        scratch_shapes=[pltpu.VMEM((b, w), F32), pltpu.VMEM((b, w), F32),
                        pltpu.VMEM((SCAN_TILE, b, w), F32), pltpu.VMEM((SCAN_TILE, b, w), F32)],
        compiler_params=_cparams(("arbitrary",)),
    )(af, df, ab, db)


def _row_splits(tm, n_sub):
    units = tm // 16
    assert tm % 16 == 0 and units >= n_sub
    cuts = [16 * ((units * i) // n_sub) for i in range(n_sub + 1)]
    return list(zip(cuts[:-1], cuts[1:]))


def _tail_kernel(x_ref, ret_ref, att_ref, hf_ref, hb_ref, lg_ref, mod_ref, wo_ref, g1_ref, b1_ref,
                 w1_ref, w2_ref, g2_ref, b2_ref, o_ref, x1_ref, u_ref, acc_ref, *, n_lat, alpha):
    tm = x_ref.shape[0]
    kf = pl.program_id(2)
    nk = pl.num_programs(2)
    subs = _row_splits(tm, TAIL_SUB_BLOCKS)
    rw = ret_ref.shape[1]
    aw = att_ref.shape[1]

    def ctx_rows(r0, r1):
        rows = pl.program_id(1) * tm + r0 + lax.broadcasted_iota(jnp.int32, (r1 - r0, 1), 0)
        return rows >= n_lat

    def mlp_part(r0, r1):
        h = jnp.maximum(jnp.dot(u_ref[r0:r1, :], w1_ref[...], preferred_element_type=F32), 0.0)
        return jnp.dot((h * h).astype(BF16), w2_ref[...], preferred_element_type=F32)

    @pl.when(kf == 0)
    def _():
        for r0, r1 in subs:
            is_ctx = ctx_rows(r0, r1)
            lru = ((hf_ref[r0:r1, :] + hb_ref[r0:r1, :]) * jax.nn.gelu(lg_ref[r0:r1, :])).astype(BF16)
            y = jnp.dot(ret_ref[r0:r1, :], wo_ref[0:rw, :], preferred_element_type=F32)
            y = y + jnp.dot(att_ref[r0:r1, :], wo_ref[rw:rw + aw, :], preferred_element_type=F32)
            y = y + jnp.dot(lru, wo_ref[rw + aw:, :], preferred_element_type=F32)
            x1 = _layer_norm(alpha * x_ref[r0:r1, :] + _row_mod(mod_ref, 2, is_ctx) * y,
                             g1_ref[...], b1_ref[...])
            x1_ref[r0:r1, :] = x1
            u_ref[r0:r1, :] = (x1 * (1.0 + _row_mod(mod_ref, 4, is_ctx))
                               + _row_mod(mod_ref, 3, is_ctx)).astype(BF16)
        for r0, r1 in subs:
            acc_ref[r0:r1, :] = mlp_part(r0, r1)

    @pl.when(jnp.logical_and(kf > 0, kf < nk - 1))
    def _():
        acc_ref[...] += mlp_part(0, tm)

    @pl.when(kf == nk - 1)
    def _():
        for r0, r1 in subs:
            a = acc_ref[r0:r1, :] + mlp_part(r0, r1)
            z = alpha * x1_ref[r0:r1, :] + _row_mod(mod_ref, 5, ctx_rows(r0, r1)) * a
            o_ref[r0:r1, :] = _layer_norm(z, g2_ref[...], b2_ref[...])


def _tail_call(x_all, ret, att, hf, hb, lru_in, mod, w_out, ln1_g, ln1_b, w1, w2, ln2_g, ln2_b,
               n_lat, n_rows, alpha, tm, tf=2048):
    b, _, d = x_all.shape
    rw, aw = ret.shape[2], att.shape[2]
    f = w1.shape[1]
    assert f // tf >= 2
    tok = lambda last: pl.BlockSpec((None, tm, last), lambda i, j, k: (i, j, 0))
    const = lambda shape: pl.BlockSpec(shape, lambda i, j, k: tuple(0 for _ in shape))
    return pl.pallas_call(
        functools.partial(_tail_kernel, n_lat=n_lat, alpha=alpha),
        grid=(b, n_rows // tm, f // tf),
        in_specs=[
            tok(d), tok(rw), tok(aw), tok(rw), tok(rw),
            pl.BlockSpec((None, tm, rw), lambda i, j, k: (i, j, 1)),
            pl.BlockSpec((None, 2, 6, d), lambda i, j, k: (i, 0, 0, 0)),
            const(w_out.shape), const(ln1_g.shape), const(ln1_b.shape),
            pl.BlockSpec((d, tf), lambda i, j, k: (0, k)),
            pl.BlockSpec((tf, d), lambda i, j, k: (k, 0)),
            const(ln2_g.shape), const(ln2_b.shape),
        ],
        out_specs=tok(d),
        out_shape=jax.ShapeDtypeStruct((b, n_rows, d), F32),
        scratch_shapes=[pltpu.VMEM((tm, d), F32), pltpu.VMEM((tm, d), BF16), pltpu.VMEM((tm, d), F32)],
        compiler_params=_cparams(("parallel", "parallel", "arbitrary")),
    )(x_all, ret, att, hf, hb, lru_in, mod, w_out, ln1_g, ln1_b, w1, w2, ln2_g, ln2_b)


def _rope_tables(n_lat, n_ctx):
    rows = n_lat // GRID_W
    row = jnp.repeat(jnp.arange(rows, dtype=F32), GRID_W)
    col = jnp.tile(jnp.arange(GRID_W, dtype=F32), rows)
    n_freq = HEAD_DIM // 4
    inv = ROPE_THETA ** (-jnp.arange(n_freq, dtype=F32) / n_freq)
    ang = jnp.concatenate([row[:, None] * inv, col[:, None] * inv], axis=-1)
    cos = jnp.concatenate([jnp.cos(ang), jnp.ones((n_ctx, HEAD_DIM // 2), F32)], axis=0)
    sin = jnp.concatenate([jnp.sin(ang), jnp.zeros((n_ctx, HEAD_DIM // 2), F32)], axis=0)
    return cos.T, sin.T


def _block_diag(w):
    k, c = w.shape[-3], w.shape[-2]
    eye = jnp.eye(k, dtype=w.dtype)
    bd = jnp.einsum('...kce,kj->...kcje', w, eye)
    return bd.reshape(*w.shape[:-3], k * c, k * c)


def kernel(x, c, ctx, c_ctx, w_ada, b_ada, w_in, ret_decay_logit, attn_q_gain, attn_k_gain,
           lru_conv_w, lru_conv_b, lru_w_a, lru_b_a, lru_w_x, lru_b_x, lru_lambda,
           w_out, ln1_g, ln1_b, w_ff1, w_ff2, ln2_g, ln2_b):
    b, n_lat, d = x.shape
    n_ctx = ctx.shape[1]
    depth = w_in.shape[0]
    s = n_lat + n_ctx
    rw, aw, kw = d // 4, d // 2, d // 8
    alpha = (2.0 * depth) ** 0.25
    assert n_lat % CHUNK == 0 and n_ctx % CHUNK == 0 and d == 16 * HEAD_DIM

    pad = (-(b + 1)) % 8
    s_in = jnp.concatenate([c, c_ctx[None, :], jnp.zeros((pad, d), F32)], axis=0)
    mods = _ada_call(s_in, w_ada, b_ada)
    mod_lat = mods[:, :b].reshape(depth, b, 1, 6, d)
    mod_ctx = jnp.broadcast_to(mods[:, b].reshape(depth, 1, 1, 6, d), (depth, b, 1, 6, d))
    mod_all = jnp.concatenate([mod_lat, mod_ctx], axis=2)

    o_aq = 4 * rw
    o_lx = o_aq + aw + 2 * kw
    wn = jnp.concatenate([w_in[:, :, :o_aq], w_in[:, :, o_lx:]], axis=2).astype(BF16)
    wt = jnp.swapaxes(w_in[:, :, o_aq:o_lx], 1, 2).astype(BF16)
    wg = (0.5 * jnp.concatenate([_block_diag(lru_w_a[:, 0]), _block_diag(lru_w_x[:, 0]),
                                 _block_diag(lru_w_a[:, 1]), _block_diag(lru_w_x[:, 1])],
                                axis=-1)).astype(BF16)
    bg = 0.5 * jnp.concatenate([lru_b_a[:, 0], lru_b_x[:, 0], lru_b_a[:, 1], lru_b_x[:, 1]],
                               axis=-1)[:, None, :]
    w_out_b = w_out.astype(BF16)
    w1_b = w_ff1.astype(BF16)
    w2_b = w_ff2.astype(BF16)
    dl_lane = jnp.repeat(ret_decay_logit, HEAD_DIM, axis=-1)
    cos_t, sin_t = _rope_tables(n_lat, n_ctx)

    def tail_tile(rows):
        return rows // 8 if rows % 64 == 0 else CHUNK

    xa = (x, ctx)
    for l in range(depth):
        need_ctx = l < depth - 1
        mod = mod_all[l]
        outs = _inproj_call(xa, mod, wn[l], wt[l], cos_t, sin_t,
                            attn_q_gain[l][:, None], attn_k_gain[l][:, None], n_lat)
        ret_in, g_in, lru_in, qt, k, vt, kn = outs[:7]
        if l == 0:
            xa = outs[7]
        ret = _ret_call(ret_in, g_in, dl_lane[l], n_lat)
        att = _attn_call(qt, k, vt, kn, n_lat, need_ctx)
        af, df, ab, db = _lru_pre_call(lru_in, lru_conv_w[l], lru_conv_b[l][None, :], wg[l], bg[l],
                                       lru_lambda[l], n_lat)
        hf, hb = _scan_call(af, df, ab, db, n_lat)
        n_rows = s if need_ctx else n_lat
        xa = _tail_call(xa, ret, att, hf, hb, lru_in, mod,
                        w_out_b[l], ln1_g[l][None, :], ln1_b[l][None, :], w1_b[l], w2_b[l],
                        ln2_g[l][None, :], ln2_b[l][None, :], n_lat, n_rows, alpha, tail_tile(n_rows))
    return xa
```

```python
import functools

import jax
import jax.numpy as jnp
from jax import lax
from jax.experimental import pallas as pl
from jax.experimental.pallas import tpu as pltpu

F32 = jnp.float32
BF16 = jnp.bfloat16

HEAD_DIM = 64
GRID_W = 64
ROPE_THETA = 10000.0
LRU_C = 8.0
EPS = 1e-6
LOG2E = 1.4426950408889634

SUBLANES = 8
BF16_ROWS = 16
ONES_ROWS = BF16_ROWS
SCORE_BOUND_MARGIN = 1.0 + 2.0 ** -7
SCORE_BOUND_CAP = 60.0
PV_LAG = 2
TAIL_TILES = 8
TAIL_SUB_BLOCKS = 2
CHUNK = 256
RET_GROUP = 8
SCAN_TILE = 128
LRU_SUB_TOKENS = 64
VMEM_LIMIT = 56 * 1024 * 1024


def _cparams(sem):
    return pltpu.CompilerParams(dimension_semantics=sem, vmem_limit_bytes=VMEM_LIMIT)


def _log_sigmoid(x):
    return jnp.minimum(x, 0.0) - jnp.log1p(jnp.exp(-jnp.abs(x)))


def _layer_norm(z, g, b):
    mu = jnp.mean(z, axis=-1, keepdims=True)
    zc = z - mu
    var = jnp.mean(zc * zc, axis=-1, keepdims=True)
    return zc * lax.rsqrt(var + EPS) * g + b


def _row_mod(mod_ref, idx, is_ctx):
    return jnp.where(is_ctx, mod_ref[1, idx:idx + 1, :], mod_ref[0, idx:idx + 1, :])


def _ctx_rows(tile_idx, tm, n_lat):
    rows = tile_idx * tm + lax.broadcasted_iota(jnp.int32, (tm, 1), 0)
    return rows >= n_lat


def _ada_kernel(s_ref, w_ref, b_ref, o_ref):
    s = s_ref[...]
    s = s * jax.nn.sigmoid(s)
    o_ref[...] = jnp.dot(s.astype(BF16), w_ref[...].astype(BF16),
                         preferred_element_type=F32) + b_ref[...]


def _ada_call(s_in, w_ada, b_ada):
    depth, d, d6 = w_ada.shape
    rows = s_in.shape[0]
    tn = d6 // 4
    return pl.pallas_call(
        _ada_kernel,
        grid=(depth, d6 // tn),
        in_specs=[
            pl.BlockSpec((rows, d), lambda l, j: (0, 0)),
            pl.BlockSpec((None, d, tn), lambda l, j: (l, 0, j)),
            pl.BlockSpec((None, 1, tn), lambda l, j: (l, 0, j)),
        ],
        out_specs=pl.BlockSpec((None, rows, tn), lambda l, j: (l, 0, j)),
        out_shape=jax.ShapeDtypeStruct((depth, rows, d6), F32),
        compiler_params=_cparams(("parallel", "parallel")),
    )(s_in, w_ada, b_ada.reshape(depth, 1, d6))


def _inproj_kernel(x_ref, *refs, n_lat, joins_streams):
    if joins_streams:
        ctx_ref, refs, xa_ref = refs[0], refs[1:-1], refs[-1]
    (mod_ref, wn_ref, wt_ref, cos_ref, sin_ref, qg_ref, kg_ref,
     ret_ref, g_ref, lru_ref, qt_ref, k_ref, vt_ref, kn_ref) = refs
    tm = x_ref.shape[0]
    is_ctx = _ctx_rows(pl.program_id(1), tm, n_lat)
    if joins_streams:
        x = jnp.where(pl.program_id(1) * tm >= n_lat, ctx_ref[...], x_ref[...])
        xa_ref[...] = x
    else:
        x = x_ref[...]
    u = (x * (1.0 + _row_mod(mod_ref, 1, is_ctx)) + _row_mod(mod_ref, 0, is_ctx)).astype(BF16)

    pt = lax.dot_general(wt_ref[...], u, (((1,), (1,)), ((), ())), preferred_element_type=F32)

    pn = jnp.dot(u, wn_ref[...], preferred_element_type=F32)
    rw = g_ref.shape[1]
    ret_ref[:, 0:rw] = pn[:, 0:rw].astype(BF16)
    ret_ref[:, rw:2 * rw] = (pn[:, rw:2 * rw] * HEAD_DIM ** -0.5).astype(BF16)
    ret_ref[:, 2 * rw:3 * rw] = pn[:, 2 * rw:3 * rw].astype(BF16)
    g_ref[...] = pn[:, 3 * rw:4 * rw]
    lru_ref[...] = pn[:, 4 * rw:]

    cos = cos_ref[...]
    sin = sin_ref[...]
    half = HEAD_DIM // 2

    def norm_rope(t, gain):
        ms = jnp.mean(t * t, axis=0, keepdims=True)
        t = t * lax.rsqrt(ms + EPS) * gain
        x1, x2 = t[:half], t[half:]
        return jnp.concatenate([x1 * cos - x2 * sin, x1 * sin + x2 * cos], axis=0)

    qw = qt_ref.shape[0]
    kw = k_ref.shape[1]
    qscale = HEAD_DIM ** -0.5 * LOG2E
    for h in range(qw // HEAD_DIM):
        r = h * HEAD_DIM
        qt_ref[r:r + HEAD_DIM, :] = (norm_rope(pt[r:r + HEAD_DIM], qg_ref[...]) * qscale).astype(BF16)
    kt = jnp.concatenate(
        [norm_rope(pt[qw + h * HEAD_DIM:qw + (h + 1) * HEAD_DIM], kg_ref[...])
         for h in range(kw // HEAD_DIM)], axis=0)
    k_ref[...] = kt.T.astype(BF16)
    kf = kt.astype(BF16).astype(F32)
    for h in range(kw // HEAD_DIM):
        kh = kf[h * HEAD_DIM:(h + 1) * HEAD_DIM]
        kn_ref[h:h + 1, :] = jnp.sum(kh * kh, axis=0, keepdims=True)
    ones = jnp.ones((ONES_ROWS, tm), BF16)
    vrows = HEAD_DIM + ONES_ROWS
    for h in range(kw // HEAD_DIM):
        v0 = qw + kw + h * HEAD_DIM
        vt_ref[h * vrows:h * vrows + HEAD_DIM, :] = pt[v0:v0 + HEAD_DIM].astype(BF16)
        vt_ref[h * vrows + HEAD_DIM:(h + 1) * vrows, :] = ones


def _inproj_call(streams, mod, wn, wt, cos_t, sin_t, qg, kg, n_lat, tm=256):
    joins = isinstance(streams, tuple)
    b, _, d = streams[0].shape if joins else streams.shape
    s = n_lat + streams[1].shape[1] if joins else streams.shape[1]
    assert n_lat % tm == 0 and s % tm == 0
    nlt = n_lat // tm
    rw, qw, kw = d // 4, d // 2, d // 8
    vw = (kw // HEAD_DIM) * (HEAD_DIM + ONES_ROWS)
    grid = (b, s // tm)
    tok = lambda shape_last: pl.BlockSpec((None, tm, shape_last), lambda i, j: (i, j, 0))
    tr = lambda rows: pl.BlockSpec((None, rows, tm), lambda i, j: (i, 0, j))
    const = lambda shape: pl.BlockSpec(shape, lambda i, j: tuple(0 for _ in shape))
    if joins:
        x_specs = [pl.BlockSpec((None, tm, d), lambda i, j: (i, jnp.minimum(j, nlt - 1), 0)),
                   pl.BlockSpec((None, tm, d), lambda i, j: (i, jnp.maximum(j - nlt, 0), 0))]
        x_args = list(streams)
    else:
        x_specs, x_args = [tok(d)], [streams]
    return pl.pallas_call(
        functools.partial(_inproj_kernel, n_lat=n_lat, joins_streams=joins),
        grid=grid,
        in_specs=x_specs + [
            pl.BlockSpec((None, 2, 6, d), lambda i, j: (i, 0, 0, 0)),
            const(wn.shape), const(wt.shape),
            pl.BlockSpec((HEAD_DIM // 2, tm), lambda i, j: (0, j)),
            pl.BlockSpec((HEAD_DIM // 2, tm), lambda i, j: (0, j)),
            const(qg.shape), const(kg.shape),
        ],
        out_specs=[tok(3 * rw), tok(rw), tok(2 * rw), tr(qw), tok(kw),
                   pl.BlockSpec((None, None, vw, tm), lambda i, j: (i, j, 0, 0)),
                   tr(kw // HEAD_DIM)] + ([tok(d)] if joins else []),
        out_shape=[
            jax.ShapeDtypeStruct((b, s, 3 * rw), BF16),
            jax.ShapeDtypeStruct((b, s, rw), F32),
            jax.ShapeDtypeStruct((b, s, 2 * rw), F32),
            jax.ShapeDtypeStruct((b, qw, s), BF16),
            jax.ShapeDtypeStruct((b, s, kw), BF16),
            jax.ShapeDtypeStruct((b, s // tm, vw, tm), BF16),
            jax.ShapeDtypeStruct((b, kw // HEAD_DIM, s), F32),
        ] + ([jax.ShapeDtypeStruct((b, s, d), F32)] if joins else []),
        compiler_params=_cparams(("parallel", "parallel")),
    )(*x_args, mod, wn, wt, cos_t, sin_t, qg, kg)


def _ret_kernel(ret_ref, g_ref, dl_ref, o_ref, sf_ref, sb_ref, dm_ref, *, n_lat_chunks):
    c_len = CHUNK
    s_len, rw = g_ref.shape
    nc = s_len // c_len
    ncl = n_lat_chunks
    n_heads = rw // HEAD_DIM

    lg = _log_sigmoid(dl_ref[...])
    lgf, lgb = lg[0:1], lg[1:2]
    pos = lax.broadcasted_iota(jnp.int32, (c_len, 1), 0).astype(F32)
    kwf = jnp.exp(lgf * (c_len - 1.0 - pos))
    kwb = jnp.exp(lgb * pos)
    qwf = jnp.exp(lgf * (pos + 1.0))
    qwb = jnp.exp(lgb * (c_len - pos))
    cdf = jnp.exp(lgf * c_len)
    cdb = jnp.exp(lgb * c_len)
    row_head = lax.broadcasted_iota(jnp.int32, (rw, rw), 0) // HEAD_DIM
    col_head = lax.broadcasted_iota(jnp.int32, (rw, rw), 1) // HEAD_DIM
    same_head = row_head == col_head
    lane_head = lax.broadcasted_iota(jnp.int32, (1, rw), 1) // HEAD_DIM

    ii = lax.broadcasted_iota(jnp.int32, (c_len, c_len), 0)
    jj = lax.broadcasted_iota(jnp.int32, (c_len, c_len), 1)
    dij = (ii - jj).astype(F32)
    for h in range(n_heads):
        lf = lgf[:, h * HEAD_DIM:h * HEAD_DIM + 1]
        lb = lgb[:, h * HEAD_DIM:h * HEAD_DIM + 1]
        dm_ref[h] = jnp.exp(jnp.where(dij >= 0.0, lf * dij, -lb * dij))

    def contrib(chunks):
        tn = (((0,), (0,)), ((), ()))
        prods = []
        for c in chunks:
            r0 = c * c_len if isinstance(c, int) else pl.multiple_of(c * c_len, c_len)
            k = ret_ref[pl.ds(r0, c_len), rw:2 * rw].astype(F32)
            v = ret_ref[pl.ds(r0, c_len), 2 * rw:3 * rw]
            prods.append((lax.dot_general((k * kwf).astype(BF16), v, tn, preferred_element_type=F32),
                          lax.dot_general((k * kwb).astype(BF16), v, tn, preferred_element_type=F32)))
        for c, (cf, cb) in zip(chunks, prods):
            sf_ref[c] = jnp.where(same_head, cf, 0.0)
            sb_ref[c] = jnp.where(same_head, cb, 0.0)

    def contrib_group(i, carry):
        contrib([RET_GROUP * i + j for j in range(RET_GROUP)])
        return carry

    lax.fori_loop(0, nc // RET_GROUP, contrib_group, 0)
    if nc % RET_GROUP:
        contrib(list(range(nc - nc % RET_GROUP, nc)))

    def chain(ref, decay, order):
        state = jnp.zeros((rw, rw), F32)
        for c in order:
            u = ref[c]
            ref[c] = state
            state = decay * state + u

    chain(sf_ref, cdf, list(range(ncl, nc)) + list(range(ncl)))
    chain(sb_ref, cdb, list(range(nc - 1, ncl - 1, -1)) + list(range(ncl - 1, -1, -1)))

    ones_blk = jnp.where(same_head, 1.0, 0.0).astype(BF16)

    def outputs(chunks):
        rows = [c * c_len if isinstance(c, int) else pl.multiple_of(c * c_len, c_len) for c in chunks]
        qs = [ret_ref[pl.ds(r0, c_len), 0:rw] for r0 in rows]
        ks = [ret_ref[pl.ds(r0, c_len), rw:2 * rw] for r0 in rows]
        vs = [ret_ref[pl.ds(r0, c_len), 2 * rw:3 * rw] for r0 in rows]
        os_, scs = [], []
        for c, q, k in zip(chunks, qs, ks):
            qf = q.astype(F32)
            qi = jnp.concatenate([(qf * qwf).astype(BF16), (qf * qwb).astype(BF16)], axis=1)
            st = jnp.concatenate([sf_ref[c], sb_ref[c]], axis=0).astype(BF16)
            os_.append(jnp.dot(qi, st, preferred_element_type=F32))
            scs.append([lax.dot_general(jnp.where(lane_head == h, q, jnp.zeros_like(q)), k,
                                        (((1,), (1,)), ((), ())), preferred_element_type=F32)
                        for h in range(n_heads)])
        for i, v in enumerate(vs):
            for h in range(n_heads):
                p = (scs[i][h] * dm_ref[h]).astype(BF16)
                os_[i] = os_[i] + jnp.where(lane_head == h, jnp.dot(p, v, preferred_element_type=F32), 0.0)
        mss = []
        for o in os_:
            o2 = o * o
            hi = o2.astype(BF16)
            lo = (o2 - hi.astype(F32)).astype(BF16)
            mss.append((jnp.dot(hi, ones_blk, preferred_element_type=F32)
                        + jnp.dot(lo, ones_blk, preferred_element_type=F32)) * (1.0 / HEAD_DIM))
        for r0, o, ms in zip(rows, os_, mss):
            g = g_ref[pl.ds(r0, c_len), :]
            o_ref[pl.ds(r0, c_len), :] = (o * lax.rsqrt(ms + EPS) * (g * jax.nn.sigmoid(g))).astype(BF16)

    def output_group(i, carry):
        outputs([RET_GROUP * i + j for j in range(RET_GROUP)])
        return carry

    lax.fori_loop(0, nc // RET_GROUP, output_group, 0)
    if nc % RET_GROUP:
        outputs(list(range(nc - nc % RET_GROUP, nc)))


def _ret_call(ret_in, g_in, dl_lane, n_lat):
    b, s, rw = g_in.shape
    nc = s // CHUNK
    return pl.pallas_call(
        functools.partial(_ret_kernel, n_lat_chunks=n_lat // CHUNK),
        grid=(b,),
        in_specs=[
            pl.BlockSpec((None, s, 3 * rw), lambda i: (i, 0, 0)),
            pl.BlockSpec((None, s, rw), lambda i: (i, 0, 0)),
            pl.BlockSpec((2, rw), lambda i: (0, 0)),
        ],
        out_specs=pl.BlockSpec((None, s, rw), lambda i: (i, 0, 0)),
        out_shape=jax.ShapeDtypeStruct((b, s, rw), BF16),
        scratch_shapes=[
            pltpu.VMEM((nc, rw, rw), F32),
            pltpu.VMEM((nc, rw, rw), F32),
            pltpu.VMEM((rw // HEAD_DIM, CHUNK, CHUNK), F32),
        ],
        compiler_params=_cparams(("parallel",)),
    )(ret_in, g_in, dl_lane)


def _attn_kernel(qt_ref, k_ref, vt_ref, kn_ref, *rest, group):
    o_ref, rhs_ref, sa_ref, sb_ref, pc_ref, m_ref, acc_ref, out_ref = rest[-8:]
    n_q_heads = qt_ref.shape[0] // HEAD_DIM
    n_chunks = vt_ref.shape[0]
    kc_len = vt_ref.shape[2]
    vrows = vt_ref.shape[1] // (n_q_heads // group)

    def scores(c, h):
        c0 = c * kc_len
        if not isinstance(c0, int):
            c0 = pl.multiple_of(c0, kc_len)
        return jnp.dot(k_ref[pl.ds(c0, kc_len), :], rhs_ref[h], preferred_element_type=F32)

    def p_times_v(c, h, p):
        kv = h // group
        return jnp.dot(vt_ref[c, kv * vrows:(kv + 1) * vrows, :], p, preferred_element_type=F32)

    kmax2 = jnp.max(kn_ref[...], axis=1, keepdims=True)
    for h in range(n_q_heads):
        qt = qt_ref[h * HEAD_DIM:(h + 1) * HEAD_DIM, :]
        zero = jnp.zeros_like(qt)
        rhs_ref[h] = jnp.concatenate([qt, zero] if h // group == 0 else [zero, qt], axis=0)
        sa_ref[h] = scores(0, h)
        qf = qt.astype(F32)
        qn2 = jnp.sum(qf * qf, axis=0, keepdims=True)
        m_ref[h:h + 1, :] = jnp.sqrt(qn2 * kmax2[h // group:h // group + 1, :]) * SCORE_BOUND_MARGIN
    acc_ref[...] = jnp.zeros(acc_ref.shape, F32)
    bounded = jnp.max(m_ref[...]) <= SCORE_BOUND_CAP

    def run_chunks(step):
        def pair(i, carry):
            step(2 * i, sa_ref, sb_ref)
            step(2 * i + 1, sb_ref, sa_ref)
            return carry

        lax.fori_loop(0, (n_chunks - 1) // 2, pair, 0)
        if (n_chunks - 1) % 2 == 1:
            step(n_chunks - 2, sa_ref, sb_ref)
            step(n_chunks - 1, sb_ref, None)
        else:
            step(n_chunks - 1, sa_ref, None)

    @pl.when(bounded)
    def _():
        lag = PV_LAG
        pc_ref[...] = jnp.zeros(pc_ref.shape, BF16)

        def step(c, cur_ref, nxt_ref):
            ps = [pc_ref[i] for i in range(lag)]
            c_prev = max(c - 1, 0) if isinstance(c, int) else jnp.maximum(c - 1, 0)
            for h in range(n_q_heads):
                if nxt_ref is not None:
                    nxt_ref[h] = scores(c + 1, h)
                if h < lag:
                    hp = n_q_heads - lag + h
                    acc_ref[hp] += p_times_v(c_prev, hp, ps[h])
                else:
                    acc_ref[h - lag] += p_times_v(c, h - lag, ps[h])
                ps.append(jnp.exp2(cur_ref[h] - m_ref[h:h + 1, :]).astype(BF16))
            for i in range(lag):
                pc_ref[i] = ps[n_q_heads + i]

        run_chunks(step)
        for i in range(lag):
            hp = n_q_heads - lag + i
            acc_ref[hp] += p_times_v(n_chunks - 1, hp, pc_ref[i])

    @pl.when(jnp.logical_not(bounded))
    def _():
        m_ref[...] = jnp.full(m_ref.shape, -jnp.inf, F32)

        def step(c, cur_ref, nxt_ref):
            for h in range(n_q_heads):
                if nxt_ref is not None:
                    nxt_ref[h] = scores(c + 1, h)
                s = cur_ref[h]
                m_old = m_ref[h:h + 1, :]
                m_new = jnp.maximum(m_old, jnp.max(s, axis=0, keepdims=True))
                alpha = jnp.exp2(m_old - m_new)
                m_ref[h:h + 1, :] = m_new
                p = jnp.exp2(s - m_new).astype(BF16)
                acc_ref[h] = alpha * acc_ref[h] + p_times_v(c, h, p)

        run_chunks(step)

    for h in range(n_q_heads):
        a = acc_ref[h]
        out_ref[h * HEAD_DIM:(h + 1) * HEAD_DIM, :] = a[:HEAD_DIM] / a[HEAD_DIM:HEAD_DIM + 1]
    o_ref[...] = out_ref[...].T.astype(BF16)


def _attn_call(qt, k, vt, kn, n_lat, with_ctx, tq_lat=512, tq_ctx=256):
    b, qw, s = qt.shape
    kw = k.shape[2]
    vw, kc_len = vt.shape[2], vt.shape[3]
    n_ctx = s - n_lat
    n_heads = qw // HEAD_DIM
    n_kv = kw // HEAD_DIM
    assert n_kv == 2 and n_lat % n_ctx == 0 and n_ctx % kc_len == 0
    tq_lat = min(tq_lat, n_lat)
    tq_ctx = min(tq_ctx, n_ctx)
    assert n_lat % tq_lat == 0 and n_ctx % tq_ctx == 0
    body = functools.partial(_attn_kernel, group=n_heads // n_kv)
    out_sds = jax.ShapeDtypeStruct((b, s, qw), BF16)

    def scratch(tq):
        return [pltpu.VMEM((n_heads, kw, tq), BF16),
                pltpu.VMEM((n_heads, kc_len, tq), F32), pltpu.VMEM((n_heads, kc_len, tq), F32),
                pltpu.VMEM((PV_LAG, kc_len, tq), BF16),
                pltpu.VMEM((n_heads, tq), F32), pltpu.VMEM((n_heads, vw // n_kv, tq), F32),
                pltpu.VMEM((qw, tq), F32)]

    att = pl.pallas_call(
        body,
        grid=(b, n_lat // tq_lat),
        in_specs=[
            pl.BlockSpec((None, qw, tq_lat), lambda i, j: (i, 0, j)),
            pl.BlockSpec((None, s, kw), lambda i, j: (i, 0, 0)),
            pl.BlockSpec((None, s // kc_len, vw, kc_len), lambda i, j: (i, 0, 0, 0)),
            pl.BlockSpec((None, n_kv, s), lambda i, j: (i, 0, 0)),
        ],
        out_specs=pl.BlockSpec((None, tq_lat, qw), lambda i, j: (i, j, 0)),
        out_shape=out_sds,
        scratch_shapes=scratch(tq_lat),
        compiler_params=_cparams(("parallel", "arbitrary")),
    )(qt, k, vt, kn)
    if not with_ctx:
        return att
    lat_tiles, lat_ctx = n_lat // tq_ctx, n_lat // n_ctx
    return pl.pallas_call(
        body,
        grid=(b, n_ctx // tq_ctx),
        in_specs=[
            pl.BlockSpec((None, qw, tq_ctx), lambda i, j: (i, 0, lat_tiles + j)),
            pl.BlockSpec((None, n_ctx, kw), lambda i, j: (i, lat_ctx, 0)),
            pl.BlockSpec((None, n_ctx // kc_len, vw, kc_len), lambda i, j: (i, lat_ctx, 0, 0)),
            pl.BlockSpec((None, n_kv, n_ctx), lambda i, j: (i, 0, lat_ctx)),
            pl.BlockSpec(memory_space=pl.ANY),
        ],
        out_specs=pl.BlockSpec((None, tq_ctx, qw), lambda i, j: (i, lat_tiles + j, 0)),
        out_shape=out_sds,
        input_output_aliases={4: 0},
        scratch_shapes=scratch(tq_ctx),
        compiler_params=_cparams(("parallel", "arbitrary")),
    )(qt, k, vt, kn, att)


def _lru_pre_kernel(cur_ref, prev_ref, next_ref, cw_ref, cb_ref, wg_ref, bg_ref, lam_ref,
                    af_ref, df_ref, ab_ref, db_ref, xs_ref, *, n_lat_chunks, n_chunks):
    c = pl.program_id(0)
    n_b, c_len, w = cur_ref.shape
    first = jnp.logical_or(c == 0, c == n_lat_chunks)
    last = jnp.logical_or(c == n_lat_chunks - 1, c == n_chunks - 1)
    halo = prev_ref.shape[1]
    xs_ref[0:halo] = jnp.where(first, 0.0, jnp.swapaxes(prev_ref[...], 0, 1))
    xs_ref[halo:halo + c_len] = jnp.swapaxes(cur_ref[...], 0, 1)
    xs_ref[halo + c_len:] = jnp.where(last, 0.0, jnp.swapaxes(next_ref[...], 0, 1))
    half_c_lam = (0.5 * LRU_C) * _log_sigmoid(lam_ref[...])
    ts = LRU_SUB_TOKENS

    def sub_chunk(i, carry):
        t0 = pl.multiple_of(i * ts, ts)

        def shifted(off):
            return xs_ref[pl.ds(halo + t0 + off, ts)]

        xr = (shifted(-2) * cw_ref[0:1, :] + shifted(-1) * cw_ref[1:2, :] + shifted(0) * cw_ref[2:3, :]
              + shifted(1) * cw_ref[3:4, :] + cb_ref[...]).reshape(ts * n_b, w)
        half_gates = jnp.dot(xr.astype(BF16), wg_ref[...], preferred_element_type=F32) + bg_ref[...]
        half_xr = 0.5 * xr
        for d, (a_ref, d_ref) in enumerate(((af_ref, df_ref), (ab_ref, db_ref))):
            t_r = jnp.tanh(half_gates[:, 2 * d * w:(2 * d + 1) * w])
            t_i = jnp.tanh(half_gates[:, (2 * d + 1) * w:(2 * d + 2) * w])
            half_c = half_c_lam[d:d + 1, :]
            log_a = half_c * t_r + half_c
            a = jnp.exp(log_a)
            om = (1.0 + a * a) * jnp.tanh(-log_a)
            drive = om * lax.rsqrt(jnp.maximum(om, 1e-30)) * (half_xr * t_i + half_xr)
            a_ref[pl.ds(t0, ts)] = a.reshape(ts, n_b, w)
            d_ref[pl.ds(t0, ts)] = drive.reshape(ts, n_b, w)
        return carry

    lax.fori_loop(0, c_len // ts, sub_chunk, 0)


def _lru_pre_call(lru_in, conv_w, conv_b, wg, bg, lam, n_lat):
    b, s, w2 = lru_in.shape
    w = w2 // 2
    nc = s // CHUNK
    halo = SUBLANES
    hb = CHUNK // halo
    const = lambda shape: pl.BlockSpec(shape, lambda j: tuple(0 for _ in shape))
    out_spec = pl.BlockSpec((CHUNK, b, w), lambda j: (j, 0, 0))
    out_sds = jax.ShapeDtypeStruct((s, b, w), F32)
    return pl.pallas_call(
        functools.partial(_lru_pre_kernel, n_lat_chunks=n_lat // CHUNK, n_chunks=nc),
        grid=(nc,),
        in_specs=[
            pl.BlockSpec((b, CHUNK, w), lambda j: (0, j, 0)),
            pl.BlockSpec((b, halo, w), lambda j: (0, jnp.maximum(j * hb - 1, 0), 0)),
            pl.BlockSpec((b, halo, w), lambda j: (0, jnp.minimum((j + 1) * hb, s // halo - 1), 0)),
            const(conv_w.shape), const(conv_b.shape), const(wg.shape), const(bg.shape),
            const(lam.shape),
        ],
        out_specs=[out_spec] * 4,
        out_shape=[out_sds] * 4,
        scratch_shapes=[pltpu.VMEM((CHUNK + 2 * halo, b, w), F32)],
        compiler_params=_cparams(("parallel",)),
    )(lru_in, lru_in, lru_in, conv_w, conv_b, wg, bg, lam)


def _scan_kernel(af_ref, df_ref, ab_ref, db_ref, hf_ref, hb_ref, sf_ref, sb_ref, tf_ref, tb_ref):
    tt = af_ref.shape[0]

    @pl.when(pl.program_id(0) == 0)
    def _():
        sf_ref[...] = jnp.zeros_like(sf_ref)
        sb_ref[...] = jnp.zeros_like(sb_ref)

    def step(t, carry):
        hf, hb = carry
        hf = af_ref[t] * hf + df_ref[t]
        tf_ref[t] = hf
        tb = tt - 1 - t
        hb = ab_ref[tb] * hb + db_ref[tb]
        tb_ref[tb] = hb
        return hf, hb

    hf, hb = lax.fori_loop(0, tt, step, (sf_ref[...], sb_ref[...]), unroll=8)
    sf_ref[...] = hf
    sb_ref[...] = hb
    hf_ref[...] = jnp.swapaxes(tf_ref[...], 0, 1)
    hb_ref[...] = jnp.swapaxes(tb_ref[...], 0, 1)


def _scan_call(af, df, ab, db, n_lat):
    s, b, w = af.shape
    nt = s // SCAN_TILE
    nlt = n_lat // SCAN_TILE
    fwd = pl.BlockSpec((SCAN_TILE, b, w), lambda i: ((i + nlt) % nt, 0, 0))
    bwd = pl.BlockSpec((SCAN_TILE, b, w), lambda i: (nt - 1 - i, 0, 0))
    fwd_out = pl.BlockSpec((b, SCAN_TILE, w), lambda i: (0, (i + nlt) % nt, 0))
    bwd_out = pl.BlockSpec((b, SCAN_TILE, w), lambda i: (0, nt - 1 - i, 0))
    sds = jax.ShapeDtypeStruct((b, s, w), F32)
    return pl.pallas_call(
        _scan_kernel,
        grid=(nt,),
        in_specs=[fwd, fwd, bwd, bwd],
        out_specs=[fwd_out, bwd_out],
        out_shape=[sds, sds],
        scratch_shapes=[pltpu.VMEM((b, w), F32), pltpu.VMEM((b, w), F32),
                        pltpu.VMEM((SCAN_TILE, b, w), F32), pltpu.VMEM((SCAN_TILE, b, w), F32)],
        compiler_params=_cparams(("arbitrary",)),
    )(af, df, ab, db)


def _row_splits(tm, n_sub):
    units = tm // BF16_ROWS
    assert tm % BF16_ROWS == 0 and units >= n_sub
    cuts = [BF16_ROWS * ((units * i) // n_sub) for i in range(n_sub + 1)]
    return list(zip(cuts[:-1], cuts[1:]))


def _tail_kernel(x_ref, ret_ref, att_ref, hf_ref, hb_ref, lg_ref, mod_ref, wo_ref, g1_ref, b1_ref,
                 w1_ref, w2_ref, g2_ref, b2_ref, o_ref, x1_ref, u_ref, acc_ref, *, n_lat, alpha):
    tm = x_ref.shape[0]
    kf = pl.program_id(2)
    nk = pl.num_programs(2)
    subs = _row_splits(tm, TAIL_SUB_BLOCKS)
    rw = ret_ref.shape[1]
    aw = att_ref.shape[1]

    def ctx_rows(r0, r1):
        rows = pl.program_id(1) * tm + r0 + lax.broadcasted_iota(jnp.int32, (r1 - r0, 1), 0)
        return rows >= n_lat

    def mlp_part(r0, r1):
        h = jnp.maximum(jnp.dot(u_ref[r0:r1, :], w1_ref[...], preferred_element_type=F32), 0.0)
        return jnp.dot((h * h).astype(BF16), w2_ref[...], preferred_element_type=F32)

    @pl.when(kf == 0)
    def _():
        for r0, r1 in subs:
            is_ctx = ctx_rows(r0, r1)
            lru = ((hf_ref[r0:r1, :] + hb_ref[r0:r1, :]) * jax.nn.gelu(lg_ref[r0:r1, :])).astype(BF16)
            y = jnp.dot(ret_ref[r0:r1, :], wo_ref[0:rw, :], preferred_element_type=F32)
            y = y + jnp.dot(att_ref[r0:r1, :], wo_ref[rw:rw + aw, :], preferred_element_type=F32)
            y = y + jnp.dot(lru, wo_ref[rw + aw:, :], preferred_element_type=F32)
            x1 = _layer_norm(alpha * x_ref[r0:r1, :] + _row_mod(mod_ref, 2, is_ctx) * y,
                             g1_ref[...], b1_ref[...])
            x1_ref[r0:r1, :] = x1
            u_ref[r0:r1, :] = (x1 * (1.0 + _row_mod(mod_ref, 4, is_ctx))
                               + _row_mod(mod_ref, 3, is_ctx)).astype(BF16)
        for r0, r1 in subs:
            acc_ref[r0:r1, :] = mlp_part(r0, r1)

    @pl.when(jnp.logical_and(kf > 0, kf < nk - 1))
    def _():
        acc_ref[...] += mlp_part(0, tm)

    @pl.when(kf == nk - 1)
    def _():
        for r0, r1 in subs:
            a = acc_ref[r0:r1, :] + mlp_part(r0, r1)
            z = alpha * x1_ref[r0:r1, :] + _row_mod(mod_ref, 5, ctx_rows(r0, r1)) * a
            o_ref[r0:r1, :] = _layer_norm(z, g2_ref[...], b2_ref[...])


def _tail_call(x_all, ret, att, hf, hb, lru_in, mod, w_out, ln1_g, ln1_b, w1, w2, ln2_g, ln2_b,
               n_lat, n_rows, alpha, tm, tf=2048):
    b, _, d = x_all.shape
    rw, aw = ret.shape[2], att.shape[2]
    f = w1.shape[1]
    assert f // tf >= 2
    tok = lambda last: pl.BlockSpec((None, tm, last), lambda i, j, k: (i, j, 0))
    const = lambda shape: pl.BlockSpec(shape, lambda i, j, k: tuple(0 for _ in shape))
    return pl.pallas_call(
        functools.partial(_tail_kernel, n_lat=n_lat, alpha=alpha),
        grid=(b, n_rows // tm, f // tf),
        in_specs=[
            tok(d), tok(rw), tok(aw), tok(rw), tok(rw),
            pl.BlockSpec((None, tm, rw), lambda i, j, k: (i, j, 1)),
            pl.BlockSpec((None, 2, 6, d), lambda i, j, k: (i, 0, 0, 0)),
            const(w_out.shape), const(ln1_g.shape), const(ln1_b.shape),
            pl.BlockSpec((d, tf), lambda i, j, k: (0, k)),
            pl.BlockSpec((tf, d), lambda i, j, k: (k, 0)),
            const(ln2_g.shape), const(ln2_b.shape),
        ],
        out_specs=tok(d),
        out_shape=jax.ShapeDtypeStruct((b, n_rows, d), F32),
        scratch_shapes=[pltpu.VMEM((tm, d), F32), pltpu.VMEM((tm, d), BF16), pltpu.VMEM((tm, d), F32)],
        compiler_params=_cparams(("parallel", "parallel", "arbitrary")),
    )(x_all, ret, att, hf, hb, lru_in, mod, w_out, ln1_g, ln1_b, w1, w2, ln2_g, ln2_b)


def _rope_tables(n_lat, n_ctx):
    rows = n_lat // GRID_W
    row = jnp.repeat(jnp.arange(rows, dtype=F32), GRID_W)
    col = jnp.tile(jnp.arange(GRID_W, dtype=F32), rows)
    n_freq = HEAD_DIM // 4
    inv = ROPE_THETA ** (-jnp.arange(n_freq, dtype=F32) / n_freq)
    ang = jnp.concatenate([row[:, None] * inv, col[:, None] * inv], axis=-1)
    cos = jnp.concatenate([jnp.cos(ang), jnp.ones((n_ctx, HEAD_DIM // 2), F32)], axis=0)
    sin = jnp.concatenate([jnp.sin(ang), jnp.zeros((n_ctx, HEAD_DIM // 2), F32)], axis=0)
    return cos.T, sin.T


def _block_diag(w):
    k, c = w.shape[-3], w.shape[-2]
    eye = jnp.eye(k, dtype=w.dtype)
    bd = jnp.einsum('...kce,kj->...kcje', w, eye)
    return bd.reshape(*w.shape[:-3], k * c, k * c)


def kernel(x, c, ctx, c_ctx, w_ada, b_ada, w_in, ret_decay_logit, attn_q_gain, attn_k_gain,
           lru_conv_w, lru_conv_b, lru_w_a, lru_b_a, lru_w_x, lru_b_x, lru_lambda,
           w_out, ln1_g, ln1_b, w_ff1, w_ff2, ln2_g, ln2_b):
    b, n_lat, d = x.shape
    n_ctx = ctx.shape[1]
    depth = w_in.shape[0]
    s = n_lat + n_ctx
    rw, aw, kw = d // 4, d // 2, d // 8
    alpha = (2.0 * depth) ** 0.25
    assert n_lat % CHUNK == 0 and n_ctx % CHUNK == 0 and d == 16 * HEAD_DIM

    pad = (-(b + 1)) % SUBLANES
    s_in = jnp.concatenate([c, c_ctx[None, :], jnp.zeros((pad, d), F32)], axis=0)
    mods = _ada_call(s_in, w_ada, b_ada)
    mod_lat = mods[:, :b].reshape(depth, b, 1, 6, d)
    mod_ctx = jnp.broadcast_to(mods[:, b].reshape(depth, 1, 1, 6, d), (depth, b, 1, 6, d))
    mod_all = jnp.concatenate([mod_lat, mod_ctx], axis=2)

    o_aq = 4 * rw
    o_lx = o_aq + aw + 2 * kw
    wn = jnp.concatenate([w_in[:, :, :o_aq], w_in[:, :, o_lx:]], axis=2).astype(BF16)
    wt = jnp.swapaxes(w_in[:, :, o_aq:o_lx], 1, 2).astype(BF16)
    wg = (0.5 * jnp.concatenate([_block_diag(lru_w_a[:, 0]), _block_diag(lru_w_x[:, 0]),
                                 _block_diag(lru_w_a[:, 1]), _block_diag(lru_w_x[:, 1])],
                                axis=-1)).astype(BF16)
    bg = 0.5 * jnp.concatenate([lru_b_a[:, 0], lru_b_x[:, 0], lru_b_a[:, 1], lru_b_x[:, 1]],
                               axis=-1)[:, None, :]
    w_out_b = w_out.astype(BF16)
    w1_b = w_ff1.astype(BF16)
    w2_b = w_ff2.astype(BF16)
    dl_lane = jnp.repeat(ret_decay_logit, HEAD_DIM, axis=-1)
    cos_t, sin_t = _rope_tables(n_lat, n_ctx)

    def tail_tile(rows):
        return rows // TAIL_TILES if rows % (TAIL_TILES * BF16_ROWS) == 0 else CHUNK

    xa = (x, ctx)
    for l in range(depth):
        need_ctx = l < depth - 1
        mod = mod_all[l]
        outs = _inproj_call(xa, mod, wn[l], wt[l], cos_t, sin_t,
                            attn_q_gain[l][:, None], attn_k_gain[l][:, None], n_lat)
        ret_in, g_in, lru_in, qt, k, vt, kn = outs[:7]
        if l == 0:
            xa = outs[7]
        ret = _ret_call(ret_in, g_in, dl_lane[l], n_lat)
        att = _attn_call(qt, k, vt, kn, n_lat, need_ctx)
        af, df, ab, db = _lru_pre_call(lru_in, lru_conv_w[l], lru_conv_b[l][None, :], wg[l], bg[l],
                                       lru_lambda[l], n_lat)
        hf, hb = _scan_call(af, df, ab, db, n_lat)
        n_rows = s if need_ctx else n_lat
        xa = _tail_call(xa, ret, att, hf, hb, lru_in, mod,
                        w_out_b[l], ln1_g[l][None, :], ln1_b[l][None, :], w1_b[l], w2_b[l],
                        ln2_g[l][None, :], ln2_b[l][None, :], n_lat, n_rows, alpha, tail_tile(n_rows))
    return xa
```

```python
import functools

import jax
import jax.numpy as jnp
from jax import lax
from jax.experimental import pallas as pl
from jax.experimental.pallas import tpu as pltpu

F32 = jnp.float32
BF16 = jnp.bfloat16

HEAD_DIM = 64
GRID_W = 64
ROPE_THETA = 10000.0
LRU_C = 8.0
EPS = 1e-6
LOG2E = 1.4426950408889634

SUBLANES = 8
BF16_ROWS = 16
ONES_ROWS = BF16_ROWS
SCORE_BOUND_MARGIN = 1.0 + 2.0 ** -7
SCORE_BOUND_CAP = 60.0
PV_LAG = 2
TAIL_TILES = 8
TAIL_SUB_BLOCKS = 2
CHUNK = 256
RET_GROUP = 8
SCAN_TILE = 256
LRU_SUB_TOKENS = 128
VMEM_LIMIT = 56 * 1024 * 1024


def _cparams(sem):
    return pltpu.CompilerParams(dimension_semantics=sem, vmem_limit_bytes=VMEM_LIMIT)


def _log_sigmoid(x):
    return jnp.minimum(x, 0.0) - jnp.log1p(jnp.exp(-jnp.abs(x)))


def _layer_norm(z, g, b):
    mu = jnp.mean(z, axis=-1, keepdims=True)
    zc = z - mu
    var = jnp.mean(zc * zc, axis=-1, keepdims=True)
    return zc * lax.rsqrt(var + EPS) * g + b


def _row_mod(mod_ref, idx, is_ctx):
    return jnp.where(is_ctx, mod_ref[1, idx:idx + 1, :], mod_ref[0, idx:idx + 1, :])


def _ctx_rows(tile_idx, tm, n_lat):
    rows = tile_idx * tm + lax.broadcasted_iota(jnp.int32, (tm, 1), 0)
    return rows >= n_lat


def _ada_kernel(s_ref, w_ref, b_ref, o_ref):
    s = s_ref[...]
    s = s * jax.nn.sigmoid(s)
    o_ref[...] = jnp.dot(s.astype(BF16), w_ref[...].astype(BF16),
                         preferred_element_type=F32) + b_ref[...]


def _ada_call(s_in, w_ada, b_ada):
    depth, d, d6 = w_ada.shape
    rows = s_in.shape[0]
    tn = d6 // 4
    return pl.pallas_call(
        _ada_kernel,
        grid=(depth, d6 // tn),
        in_specs=[
            pl.BlockSpec((rows, d), lambda l, j: (0, 0)),
            pl.BlockSpec((None, d, tn), lambda l, j: (l, 0, j)),
            pl.BlockSpec((None, 1, tn), lambda l, j: (l, 0, j)),
        ],
        out_specs=pl.BlockSpec((None, rows, tn), lambda l, j: (l, 0, j)),
        out_shape=jax.ShapeDtypeStruct((depth, rows, d6), F32),
        compiler_params=_cparams(("parallel", "parallel")),
    )(s_in, w_ada, b_ada.reshape(depth, 1, d6))


def _inproj_kernel(x_ref, *refs, n_lat, joins_streams):
    if joins_streams:
        ctx_ref, refs, xa_ref = refs[0], refs[1:-1], refs[-1]
    (mod_ref, wn_ref, wt_ref, cos_ref, sin_ref, qg_ref, kg_ref,
     ret_ref, g_ref, lru_ref, qt_ref, k_ref, vt_ref, kn_ref) = refs
    tm = x_ref.shape[0]
    is_ctx = _ctx_rows(pl.program_id(1), tm, n_lat)
    if joins_streams:
        x = jnp.where(pl.program_id(1) * tm >= n_lat, ctx_ref[...], x_ref[...])
        xa_ref[...] = x
    else:
        x = x_ref[...]
    u = (x * (1.0 + _row_mod(mod_ref, 1, is_ctx)) + _row_mod(mod_ref, 0, is_ctx)).astype(BF16)

    pt = lax.dot_general(wt_ref[...], u, (((1,), (1,)), ((), ())), preferred_element_type=F32)

    pn = jnp.dot(u, wn_ref[...], preferred_element_type=F32)
    rw = g_ref.shape[1]
    ret_ref[:, 0:rw] = pn[:, 0:rw].astype(BF16)
    ret_ref[:, rw:2 * rw] = (pn[:, rw:2 * rw] * HEAD_DIM ** -0.5).astype(BF16)
    ret_ref[:, 2 * rw:3 * rw] = pn[:, 2 * rw:3 * rw].astype(BF16)
    g_ref[...] = pn[:, 3 * rw:4 * rw]
    lru_ref[...] = pn[:, 4 * rw:]

    cos = cos_ref[...]
    sin = sin_ref[...]
    half = HEAD_DIM // 2

    def norm_rope(t, gain):
        ms = jnp.mean(t * t, axis=0, keepdims=True)
        t = t * lax.rsqrt(ms + EPS) * gain
        x1, x2 = t[:half], t[half:]
        return jnp.concatenate([x1 * cos - x2 * sin, x1 * sin + x2 * cos], axis=0)

    qw = qt_ref.shape[0]
    kw = k_ref.shape[1]
    qscale = HEAD_DIM ** -0.5 * LOG2E
    for h in range(qw // HEAD_DIM):
        r = h * HEAD_DIM
        qt_ref[r:r + HEAD_DIM, :] = (norm_rope(pt[r:r + HEAD_DIM], qg_ref[...]) * qscale).astype(BF16)
    kt = jnp.concatenate(
        [norm_rope(pt[qw + h * HEAD_DIM:qw + (h + 1) * HEAD_DIM], kg_ref[...])
         for h in range(kw // HEAD_DIM)], axis=0)
    k_ref[...] = kt.T.astype(BF16)
    kf = kt.astype(BF16).astype(F32)
    for h in range(kw // HEAD_DIM):
        kh = kf[h * HEAD_DIM:(h + 1) * HEAD_DIM]
        kn_ref[h:h + 1, :] = jnp.sum(kh * kh, axis=0, keepdims=True)
    ones = jnp.ones((ONES_ROWS, tm), BF16)
    vrows = HEAD_DIM + ONES_ROWS
    for h in range(kw // HEAD_DIM):
        v0 = qw + kw + h * HEAD_DIM
        vt_ref[h * vrows:h * vrows + HEAD_DIM, :] = pt[v0:v0 + HEAD_DIM].astype(BF16)
        vt_ref[h * vrows + HEAD_DIM:(h + 1) * vrows, :] = ones


def _inproj_call(streams, mod, wn, wt, cos_t, sin_t, qg, kg, n_lat, tm=256):
    joins = isinstance(streams, tuple)
    b, _, d = streams[0].shape if joins else streams.shape
    s = n_lat + streams[1].shape[1] if joins else streams.shape[1]
    assert n_lat % tm == 0 and s % tm == 0
    nlt = n_lat // tm
    rw, qw, kw = d // 4, d // 2, d // 8
    vw = (kw // HEAD_DIM) * (HEAD_DIM + ONES_ROWS)
    grid = (b, s // tm)
    tok = lambda shape_last: pl.BlockSpec((None, tm, shape_last), lambda i, j: (i, j, 0))
    tr = lambda rows: pl.BlockSpec((None, rows, tm), lambda i, j: (i, 0, j))
    const = lambda shape: pl.BlockSpec(shape, lambda i, j: tuple(0 for _ in shape))
    if joins:
        x_specs = [pl.BlockSpec((None, tm, d), lambda i, j: (i, jnp.minimum(j, nlt - 1), 0)),
                   pl.BlockSpec((None, tm, d), lambda i, j: (i, jnp.maximum(j - nlt, 0), 0))]
        x_args = list(streams)
    else:
        x_specs, x_args = [tok(d)], [streams]
    return pl.pallas_call(
        functools.partial(_inproj_kernel, n_lat=n_lat, joins_streams=joins),
        grid=grid,
        in_specs=x_specs + [
            pl.BlockSpec((None, 2, 6, d), lambda i, j: (i, 0, 0, 0)),
            const(wn.shape), const(wt.shape),
            pl.BlockSpec((HEAD_DIM // 2, tm), lambda i, j: (0, j)),
            pl.BlockSpec((HEAD_DIM // 2, tm), lambda i, j: (0, j)),
            const(qg.shape), const(kg.shape),
        ],
        out_specs=[tok(3 * rw), tok(rw), tok(2 * rw), tr(qw), tok(kw),
                   pl.BlockSpec((None, None, vw, tm), lambda i, j: (i, j, 0, 0)),
                   tr(kw // HEAD_DIM)] + ([tok(d)] if joins else []),
        out_shape=[
            jax.ShapeDtypeStruct((b, s, 3 * rw), BF16),
            jax.ShapeDtypeStruct((b, s, rw), F32),
            jax.ShapeDtypeStruct((b, s, 2 * rw), F32),
            jax.ShapeDtypeStruct((b, qw, s), BF16),
            jax.ShapeDtypeStruct((b, s, kw), BF16),
            jax.ShapeDtypeStruct((b, s // tm, vw, tm), BF16),
            jax.ShapeDtypeStruct((b, kw // HEAD_DIM, s), F32),
        ] + ([jax.ShapeDtypeStruct((b, s, d), F32)] if joins else []),
        compiler_params=_cparams(("parallel", "parallel")),
    )(*x_args, mod, wn, wt, cos_t, sin_t, qg, kg)


def _ret_kernel(ret_ref, g_ref, dl_ref, o_ref, sf_ref, sb_ref, dm_ref, *, n_lat_chunks):
    c_len = CHUNK
    s_len, rw = g_ref.shape
    nc = s_len // c_len
    ncl = n_lat_chunks
    n_heads = rw // HEAD_DIM

    lg = _log_sigmoid(dl_ref[...])
    lgf, lgb = lg[0:1], lg[1:2]
    pos = lax.broadcasted_iota(jnp.int32, (c_len, 1), 0).astype(F32)
    kwf = jnp.exp(lgf * (c_len - 1.0 - pos))
    kwb = jnp.exp(lgb * pos)
    qwf = jnp.exp(lgf * (pos + 1.0))
    qwb = jnp.exp(lgb * (c_len - pos))
    cdf = jnp.exp(lgf * c_len)
    cdb = jnp.exp(lgb * c_len)
    row_head = lax.broadcasted_iota(jnp.int32, (rw, rw), 0) // HEAD_DIM
    col_head = lax.broadcasted_iota(jnp.int32, (rw, rw), 1) // HEAD_DIM
    same_head = row_head == col_head
    lane_head = lax.broadcasted_iota(jnp.int32, (1, rw), 1) // HEAD_DIM

    ii = lax.broadcasted_iota(jnp.int32, (c_len, c_len), 0)
    jj = lax.broadcasted_iota(jnp.int32, (c_len, c_len), 1)
    dij = (ii - jj).astype(F32)
    for h in range(n_heads):
        lf = lgf[:, h * HEAD_DIM:h * HEAD_DIM + 1]
        lb = lgb[:, h * HEAD_DIM:h * HEAD_DIM + 1]
        dm_ref[h] = jnp.exp(jnp.where(dij >= 0.0, lf * dij, -lb * dij))

    def contrib(chunks):
        tn = (((0,), (0,)), ((), ()))
        prods = []
        for c in chunks:
            r0 = c * c_len if isinstance(c, int) else pl.multiple_of(c * c_len, c_len)
            k = ret_ref[pl.ds(r0, c_len), rw:2 * rw].astype(F32)
            v = ret_ref[pl.ds(r0, c_len), 2 * rw:3 * rw]
            prods.append((lax.dot_general((k * kwf).astype(BF16), v, tn, preferred_element_type=F32),
                          lax.dot_general((k * kwb).astype(BF16), v, tn, preferred_element_type=F32)))
        for c, (cf, cb) in zip(chunks, prods):
            sf_ref[c] = jnp.where(same_head, cf, 0.0)
            sb_ref[c] = jnp.where(same_head, cb, 0.0)

    def contrib_group(i, carry):
        contrib([RET_GROUP * i + j for j in range(RET_GROUP)])
        return carry

    lax.fori_loop(0, nc // RET_GROUP, contrib_group, 0)
    if nc % RET_GROUP:
        contrib(list(range(nc - nc % RET_GROUP, nc)))

    def chain(ref, decay, order):
        state = jnp.zeros((rw, rw), F32)
        for c in order:
            u = ref[c]
            ref[c] = state
            state = decay * state + u

    chain(sf_ref, cdf, list(range(ncl, nc)) + list(range(ncl)))
    chain(sb_ref, cdb, list(range(nc - 1, ncl - 1, -1)) + list(range(ncl - 1, -1, -1)))

    ones_blk = jnp.where(same_head, 1.0, 0.0).astype(BF16)

    def outputs(chunks):
        rows = [c * c_len if isinstance(c, int) else pl.multiple_of(c * c_len, c_len) for c in chunks]
        qs = [ret_ref[pl.ds(r0, c_len), 0:rw] for r0 in rows]
        ks = [ret_ref[pl.ds(r0, c_len), rw:2 * rw] for r0 in rows]
        vs = [ret_ref[pl.ds(r0, c_len), 2 * rw:3 * rw] for r0 in rows]
        os_, scs = [], []
        for c, q, k in zip(chunks, qs, ks):
            qf = q.astype(F32)
            qi = jnp.concatenate([(qf * qwf).astype(BF16), (qf * qwb).astype(BF16)], axis=1)
            st = jnp.concatenate([sf_ref[c], sb_ref[c]], axis=0).astype(BF16)
            os_.append(jnp.dot(qi, st, preferred_element_type=F32))
            scs.append([lax.dot_general(jnp.where(lane_head == h, q, jnp.zeros_like(q)), k,
                                        (((1,), (1,)), ((), ())), preferred_element_type=F32)
                        for h in range(n_heads)])
        for i, v in enumerate(vs):
            for h in range(n_heads):
                p = (scs[i][h] * dm_ref[h]).astype(BF16)
                os_[i] = os_[i] + jnp.where(lane_head == h, jnp.dot(p, v, preferred_element_type=F32), 0.0)
        mss = []
        for o in os_:
            o2 = o * o
            hi = o2.astype(BF16)
            lo = (o2 - hi.astype(F32)).astype(BF16)
            mss.append((jnp.dot(hi, ones_blk, preferred_element_type=F32)
                        + jnp.dot(lo, ones_blk, preferred_element_type=F32)) * (1.0 / HEAD_DIM))
        for r0, o, ms in zip(rows, os_, mss):
            g = g_ref[pl.ds(r0, c_len), :]
            o_ref[pl.ds(r0, c_len), :] = (o * lax.rsqrt(ms + EPS) * (g * jax.nn.sigmoid(g))).astype(BF16)

    def output_group(i, carry):
        outputs([RET_GROUP * i + j for j in range(RET_GROUP)])
        return carry

    lax.fori_loop(0, nc // RET_GROUP, output_group, 0)
    if nc % RET_GROUP:
        outputs(list(range(nc - nc % RET_GROUP, nc)))


def _ret_call(ret_in, g_in, dl_lane, n_lat):
    b, s, rw = g_in.shape
    nc = s // CHUNK
    return pl.pallas_call(
        functools.partial(_ret_kernel, n_lat_chunks=n_lat // CHUNK),
        grid=(b,),
        in_specs=[
            pl.BlockSpec((None, s, 3 * rw), lambda i: (i, 0, 0)),
            pl.BlockSpec((None, s, rw), lambda i: (i, 0, 0)),
            pl.BlockSpec((2, rw), lambda i: (0, 0)),
        ],
        out_specs=pl.BlockSpec((None, s, rw), lambda i: (i, 0, 0)),
        out_shape=jax.ShapeDtypeStruct((b, s, rw), BF16),
        scratch_shapes=[
            pltpu.VMEM((nc, rw, rw), F32),
            pltpu.VMEM((nc, rw, rw), F32),
            pltpu.VMEM((rw // HEAD_DIM, CHUNK, CHUNK), F32),
        ],
        compiler_params=_cparams(("parallel",)),
    )(ret_in, g_in, dl_lane)


def _attn_kernel(qt_ref, k_ref, vt_ref, kn_ref, *rest, group):
    o_ref, rhs_ref, sa_ref, sb_ref, pc_ref, m_ref, acc_ref, out_ref = rest[-8:]
    n_q_heads = qt_ref.shape[0] // HEAD_DIM
    n_chunks = vt_ref.shape[0]
    kc_len = vt_ref.shape[2]
    vrows = vt_ref.shape[1] // (n_q_heads // group)

    def scores(c, h):
        c0 = c * kc_len
        if not isinstance(c0, int):
            c0 = pl.multiple_of(c0, kc_len)
        return jnp.dot(k_ref[pl.ds(c0, kc_len), :], rhs_ref[h], preferred_element_type=F32)

    def p_times_v(c, h, p):
        kv = h // group
        return jnp.dot(vt_ref[c, kv * vrows:(kv + 1) * vrows, :], p, preferred_element_type=F32)

    kmax2 = jnp.max(kn_ref[...], axis=1, keepdims=True)
    for h in range(n_q_heads):
        qt = qt_ref[h * HEAD_DIM:(h + 1) * HEAD_DIM, :]
        zero = jnp.zeros_like(qt)
        rhs_ref[h] = jnp.concatenate([qt, zero] if h // group == 0 else [zero, qt], axis=0)
        sa_ref[h] = scores(0, h)
        qf = qt.astype(F32)
        qn2 = jnp.sum(qf * qf, axis=0, keepdims=True)
        m_ref[h:h + 1, :] = jnp.sqrt(qn2 * kmax2[h // group:h // group + 1, :]) * SCORE_BOUND_MARGIN
    acc_ref[...] = jnp.zeros(acc_ref.shape, F32)
    bounded = jnp.max(m_ref[...]) <= SCORE_BOUND_CAP

    def run_chunks(step):
        def pair(i, carry):
            step(2 * i, sa_ref, sb_ref)
            step(2 * i + 1, sb_ref, sa_ref)
            return carry

        lax.fori_loop(0, (n_chunks - 1) // 2, pair, 0)
        if (n_chunks - 1) % 2 == 1:
            step(n_chunks - 2, sa_ref, sb_ref)
            step(n_chunks - 1, sb_ref, None)
        else:
            step(n_chunks - 1, sa_ref, None)

    @pl.when(bounded)
    def _():
        lag = PV_LAG
        pc_ref[...] = jnp.zeros(pc_ref.shape, BF16)

        def step(c, cur_ref, nxt_ref):
            ps = [pc_ref[i] for i in range(lag)]
            c_prev = max(c - 1, 0) if isinstance(c, int) else jnp.maximum(c - 1, 0)
            for h in range(n_q_heads):
                if nxt_ref is not None:
                    nxt_ref[h] = scores(c + 1, h)
                if h < lag:
                    hp = n_q_heads - lag + h
                    acc_ref[hp] += p_times_v(c_prev, hp, ps[h])
                else:
                    acc_ref[h - lag] += p_times_v(c, h - lag, ps[h])
                ps.append(jnp.exp2(cur_ref[h] - m_ref[h:h + 1, :]).astype(BF16))
            for i in range(lag):
                pc_ref[i] = ps[n_q_heads + i]

        run_chunks(step)
        for i in range(lag):
            hp = n_q_heads - lag + i
            acc_ref[hp] += p_times_v(n_chunks - 1, hp, pc_ref[i])

    @pl.when(jnp.logical_not(bounded))
    def _():
        m_ref[...] = jnp.full(m_ref.shape, -jnp.inf, F32)

        def step(c, cur_ref, nxt_ref):
            for h in range(n_q_heads):
                if nxt_ref is not None:
                    nxt_ref[h] = scores(c + 1, h)
                s = cur_ref[h]
                m_old = m_ref[h:h + 1, :]
                m_new = jnp.maximum(m_old, jnp.max(s, axis=0, keepdims=True))
                alpha = jnp.exp2(m_old - m_new)
                m_ref[h:h + 1, :] = m_new
                p = jnp.exp2(s - m_new).astype(BF16)
                acc_ref[h] = alpha * acc_ref[h] + p_times_v(c, h, p)

        run_chunks(step)

    for h in range(n_q_heads):
        a = acc_ref[h]
        out_ref[h * HEAD_DIM:(h + 1) * HEAD_DIM, :] = a[:HEAD_DIM] / a[HEAD_DIM:HEAD_DIM + 1]
    o_ref[...] = out_ref[...].T.astype(BF16)


def _attn_call(qt, k, vt, kn, n_lat, with_ctx, tq_lat=512, tq_ctx=256):
    b, qw, s = qt.shape
    kw = k.shape[2]
    vw, kc_len = vt.shape[2], vt.shape[3]
    n_ctx = s - n_lat
    n_heads = qw // HEAD_DIM
    n_kv = kw // HEAD_DIM
    assert n_kv == 2 and n_lat % n_ctx == 0 and n_ctx % kc_len == 0
    tq_lat = min(tq_lat, n_lat)
    tq_ctx = min(tq_ctx, n_ctx)
    assert n_lat % tq_lat == 0 and n_ctx % tq_ctx == 0
    body = functools.partial(_attn_kernel, group=n_heads // n_kv)
    out_sds = jax.ShapeDtypeStruct((b, s, qw), BF16)

    def scratch(tq):
        return [pltpu.VMEM((n_heads, kw, tq), BF16),
                pltpu.VMEM((n_heads, kc_len, tq), F32), pltpu.VMEM((n_heads, kc_len, tq), F32),
                pltpu.VMEM((PV_LAG, kc_len, tq), BF16),
                pltpu.VMEM((n_heads, tq), F32), pltpu.VMEM((n_heads, vw // n_kv, tq), F32),
                pltpu.VMEM((qw, tq), F32)]

    att = pl.pallas_call(
        body,
        grid=(b, n_lat // tq_lat),
        in_specs=[
            pl.BlockSpec((None, qw, tq_lat), lambda i, j: (i, 0, j)),
            pl.BlockSpec((None, s, kw), lambda i, j: (i, 0, 0)),
            pl.BlockSpec((None, s // kc_len, vw, kc_len), lambda i, j: (i, 0, 0, 0)),
            pl.BlockSpec((None, n_kv, s), lambda i, j: (i, 0, 0)),
        ],
        out_specs=pl.BlockSpec((None, tq_lat, qw), lambda i, j: (i, j, 0)),
        out_shape=out_sds,
        scratch_shapes=scratch(tq_lat),
        compiler_params=_cparams(("parallel", "arbitrary")),
    )(qt, k, vt, kn)
    if not with_ctx:
        return att
    lat_tiles, lat_ctx = n_lat // tq_ctx, n_lat // n_ctx
    return pl.pallas_call(
        body,
        grid=(b, n_ctx // tq_ctx),
        in_specs=[
            pl.BlockSpec((None, qw, tq_ctx), lambda i, j: (i, 0, lat_tiles + j)),
            pl.BlockSpec((None, n_ctx, kw), lambda i, j: (i, lat_ctx, 0)),
            pl.BlockSpec((None, n_ctx // kc_len, vw, kc_len), lambda i, j: (i, lat_ctx, 0, 0)),
            pl.BlockSpec((None, n_kv, n_ctx), lambda i, j: (i, 0, lat_ctx)),
            pl.BlockSpec(memory_space=pl.ANY),
        ],
        out_specs=pl.BlockSpec((None, tq_ctx, qw), lambda i, j: (i, lat_tiles + j, 0)),
        out_shape=out_sds,
        input_output_aliases={4: 0},
        scratch_shapes=scratch(tq_ctx),
        compiler_params=_cparams(("parallel", "arbitrary")),
    )(qt, k, vt, kn, att)


def _lru_pre_kernel(cur_ref, prev_ref, next_ref, cw_ref, cb_ref, wg_ref, bg_ref, lam_ref,
                    af_ref, df_ref, ab_ref, db_ref, xs_ref, *, n_lat_chunks, n_chunks):
    c = pl.program_id(0)
    n_b, c_len, w = cur_ref.shape
    first = jnp.logical_or(c == 0, c == n_lat_chunks)
    last = jnp.logical_or(c == n_lat_chunks - 1, c == n_chunks - 1)
    halo = prev_ref.shape[1]
    xs_ref[0:halo] = jnp.where(first, 0.0, jnp.swapaxes(prev_ref[...], 0, 1))
    xs_ref[halo:halo + c_len] = jnp.swapaxes(cur_ref[...], 0, 1)
    xs_ref[halo + c_len:] = jnp.where(last, 0.0, jnp.swapaxes(next_ref[...], 0, 1))
    half_c_lam = (0.5 * LRU_C) * _log_sigmoid(lam_ref[...])
    ts = LRU_SUB_TOKENS

    def sub_chunk(i, carry):
        t0 = pl.multiple_of(i * ts, ts)

        def shifted(off):
            return xs_ref[pl.ds(halo + t0 + off, ts)]

        xr = (shifted(-2) * cw_ref[0:1, :] + shifted(-1) * cw_ref[1:2, :] + shifted(0) * cw_ref[2:3, :]
              + shifted(1) * cw_ref[3:4, :] + cb_ref[...]).reshape(ts * n_b, w)
        half_gates = jnp.dot(xr.astype(BF16), wg_ref[...], preferred_element_type=F32) + bg_ref[...]
        half_xr = 0.5 * xr
        for d, (a_ref, d_ref) in enumerate(((af_ref, df_ref), (ab_ref, db_ref))):
            t_r = jnp.tanh(half_gates[:, 2 * d * w:(2 * d + 1) * w])
            t_i = jnp.tanh(half_gates[:, (2 * d + 1) * w:(2 * d + 2) * w])
            half_c = half_c_lam[d:d + 1, :]
            log_a = half_c * t_r + half_c
            a = jnp.exp(log_a)
            om = (1.0 + a * a) * jnp.tanh(-log_a)
            drive = om * lax.rsqrt(jnp.maximum(om, 1e-30)) * (half_xr * t_i + half_xr)
            a_ref[pl.ds(t0, ts)] = a.reshape(ts, n_b, w)
            d_ref[pl.ds(t0, ts)] = drive.reshape(ts, n_b, w)
        return carry

    lax.fori_loop(0, c_len // ts, sub_chunk, 0)


def _lru_pre_call(lru_in, conv_w, conv_b, wg, bg, lam, n_lat):
    b, s, w2 = lru_in.shape
    w = w2 // 2
    nc = s // CHUNK
    halo = SUBLANES
    hb = CHUNK // halo
    const = lambda shape: pl.BlockSpec(shape, lambda j: tuple(0 for _ in shape))
    out_spec = pl.BlockSpec((CHUNK, b, w), lambda j: (j, 0, 0))
    out_sds = jax.ShapeDtypeStruct((s, b, w), F32)
    return pl.pallas_call(
        functools.partial(_lru_pre_kernel, n_lat_chunks=n_lat // CHUNK, n_chunks=nc),
        grid=(nc,),
        in_specs=[
            pl.BlockSpec((b, CHUNK, w), lambda j: (0, j, 0)),
            pl.BlockSpec((b, halo, w), lambda j: (0, jnp.maximum(j * hb - 1, 0), 0)),
            pl.BlockSpec((b, halo, w), lambda j: (0, jnp.minimum((j + 1) * hb, s // halo - 1), 0)),
            const(conv_w.shape), const(conv_b.shape), const(wg.shape), const(bg.shape),
            const(lam.shape),
        ],
        out_specs=[out_spec] * 4,
        out_shape=[out_sds] * 4,
        scratch_shapes=[pltpu.VMEM((CHUNK + 2 * halo, b, w), F32)],
        compiler_params=_cparams(("parallel",)),
    )(lru_in, lru_in, lru_in, conv_w, conv_b, wg, bg, lam)


def _scan_kernel(af_ref, df_ref, ab_ref, db_ref, hf_ref, hb_ref, sf_ref, sb_ref, tf_ref, tb_ref):
    tt = af_ref.shape[0]

    @pl.when(pl.program_id(0) == 0)
    def _():
        sf_ref[...] = jnp.zeros_like(sf_ref)
        sb_ref[...] = jnp.zeros_like(sb_ref)

    def step(t, carry):
        hf, hb = carry
        hf = af_ref[t] * hf + df_ref[t]
        tf_ref[t] = hf
        tb = tt - 1 - t
        hb = ab_ref[tb] * hb + db_ref[tb]
        tb_ref[tb] = hb
        return hf, hb

    hf, hb = lax.fori_loop(0, tt, step, (sf_ref[...], sb_ref[...]), unroll=8)
    sf_ref[...] = hf
    sb_ref[...] = hb
    hf_ref[...] = jnp.swapaxes(tf_ref[...], 0, 1)
    hb_ref[...] = jnp.swapaxes(tb_ref[...], 0, 1)


def _scan_call(af, df, ab, db, n_lat):
    s, b, w = af.shape
    nt = s // SCAN_TILE
    nlt = n_lat // SCAN_TILE
    fwd = pl.BlockSpec((SCAN_TILE, b, w), lambda i: ((i + nlt) % nt, 0, 0))
    bwd = pl.BlockSpec((SCAN_TILE, b, w), lambda i: (nt - 1 - i, 0, 0))
    fwd_out = pl.BlockSpec((b, SCAN_TILE, w), lambda i: (0, (i + nlt) % nt, 0))
    bwd_out = pl.BlockSpec((b, SCAN_TILE, w), lambda i: (0, nt - 1 - i, 0))
    sds = jax.ShapeDtypeStruct((b, s, w), F32)
    return pl.pallas_call(
        _scan_kernel,
        grid=(nt,),
        in_specs=[fwd, fwd, bwd, bwd],
        out_specs=[fwd_out, bwd_out],
        out_shape=[sds, sds],
        scratch_shapes=[pltpu.VMEM((b, w), F32), pltpu.VMEM((b, w), F32),
                        pltpu.VMEM((SCAN_TILE, b, w), F32), pltpu.VMEM((SCAN_TILE, b, w), F32)],
        compiler_params=_cparams(("arbitrary",)),
    )(af, df, ab, db)


def _row_splits(tm, n_sub):
    units = tm // BF16_ROWS
    assert tm % BF16_ROWS == 0 and units >= n_sub
    cuts = [BF16_ROWS * ((units * i) // n_sub) for i in range(n_sub + 1)]
    return list(zip(cuts[:-1], cuts[1:]))


def _tail_kernel(x_ref, ret_ref, att_ref, hf_ref, hb_ref, lg_ref, mod_ref, wo_ref, g1_ref, b1_ref,
                 w1_ref, w2_ref, g2_ref, b2_ref, o_ref, x1_ref, u_ref, acc_ref, *, n_lat, alpha):
    tm = x_ref.shape[0]
    kf = pl.program_id(2)
    nk = pl.num_programs(2)
    subs = _row_splits(tm, TAIL_SUB_BLOCKS)
    rw = ret_ref.shape[1]
    aw = att_ref.shape[1]

    def ctx_rows(r0, r1):
        rows = pl.program_id(1) * tm + r0 + lax.broadcasted_iota(jnp.int32, (r1 - r0, 1), 0)
        return rows >= n_lat

    def mlp_part(r0, r1):
        h = jnp.maximum(jnp.dot(u_ref[r0:r1, :], w1_ref[...], preferred_element_type=F32), 0.0)
        return jnp.dot((h * h).astype(BF16), w2_ref[...], preferred_element_type=F32)

    @pl.when(kf == 0)
    def _():
        for r0, r1 in subs:
            is_ctx = ctx_rows(r0, r1)
            lru = ((hf_ref[r0:r1, :] + hb_ref[r0:r1, :]) * jax.nn.gelu(lg_ref[r0:r1, :])).astype(BF16)
            y = jnp.dot(ret_ref[r0:r1, :], wo_ref[0:rw, :], preferred_element_type=F32)
            y = y + jnp.dot(att_ref[r0:r1, :], wo_ref[rw:rw + aw, :], preferred_element_type=F32)
            y = y + jnp.dot(lru, wo_ref[rw + aw:, :], preferred_element_type=F32)
            x1 = _layer_norm(alpha * x_ref[r0:r1, :] + _row_mod(mod_ref, 2, is_ctx) * y,
                             g1_ref[...], b1_ref[...])
            x1_ref[r0:r1, :] = x1
            u_ref[r0:r1, :] = (x1 * (1.0 + _row_mod(mod_ref, 4, is_ctx))
                               + _row_mod(mod_ref, 3, is_ctx)).astype(BF16)
        for r0, r1 in subs:
            acc_ref[r0:r1, :] = mlp_part(r0, r1)

    @pl.when(jnp.logical_and(kf > 0, kf < nk - 1))
    def _():
        acc_ref[...] += mlp_part(0, tm)

    @pl.when(kf == nk - 1)
    def _():
        for r0, r1 in subs:
            a = acc_ref[r0:r1, :] + mlp_part(r0, r1)
            z = alpha * x1_ref[r0:r1, :] + _row_mod(mod_ref, 5, ctx_rows(r0, r1)) * a
            o_ref[r0:r1, :] = _layer_norm(z, g2_ref[...], b2_ref[...])


def _tail_call(x_all, ret, att, hf, hb, lru_in, mod, w_out, ln1_g, ln1_b, w1, w2, ln2_g, ln2_b,
               n_lat, n_rows, alpha, tm, tf=2048):
    b, _, d = x_all.shape
    rw, aw = ret.shape[2], att.shape[2]
    f = w1.shape[1]
    assert f // tf >= 2
    tok = lambda last: pl.BlockSpec((None, tm, last), lambda i, j, k: (i, j, 0))
    const = lambda shape: pl.BlockSpec(shape, lambda i, j, k: tuple(0 for _ in shape))
    return pl.pallas_call(
        functools.partial(_tail_kernel, n_lat=n_lat, alpha=alpha),
        grid=(b, n_rows // tm, f // tf),
        in_specs=[
            tok(d), tok(rw), tok(aw), tok(rw), tok(rw),
            pl.BlockSpec((None, tm, rw), lambda i, j, k: (i, j, 1)),
            pl.BlockSpec((None, 2, 6, d), lambda i, j, k: (i, 0, 0, 0)),
            const(w_out.shape), const(ln1_g.shape), const(ln1_b.shape),
            pl.BlockSpec((d, tf), lambda i, j, k: (0, k)),
            pl.BlockSpec((tf, d), lambda i, j, k: (k, 0)),
            const(ln2_g.shape), const(ln2_b.shape),
        ],
        out_specs=tok(d),
        out_shape=jax.ShapeDtypeStruct((b, n_rows, d), F32),
        scratch_shapes=[pltpu.VMEM((tm, d), F32), pltpu.VMEM((tm, d), BF16), pltpu.VMEM((tm, d), F32)],
        compiler_params=_cparams(("parallel", "parallel", "arbitrary")),
    )(x_all, ret, att, hf, hb, lru_in, mod, w_out, ln1_g, ln1_b, w1, w2, ln2_g, ln2_b)


def _rope_tables(n_lat, n_ctx):
    rows = n_lat // GRID_W
    row = jnp.repeat(jnp.arange(rows, dtype=F32), GRID_W)
    col = jnp.tile(jnp.arange(GRID_W, dtype=F32), rows)
    n_freq = HEAD_DIM // 4
    inv = ROPE_THETA ** (-jnp.arange(n_freq, dtype=F32) / n_freq)
    ang = jnp.concatenate([row[:, None] * inv, col[:, None] * inv], axis=-1)
    cos = jnp.concatenate([jnp.cos(ang), jnp.ones((n_ctx, HEAD_DIM // 2), F32)], axis=0)
    sin = jnp.concatenate([jnp.sin(ang), jnp.zeros((n_ctx, HEAD_DIM // 2), F32)], axis=0)
    return cos.T, sin.T


def _block_diag(w):
    k, c = w.shape[-3], w.shape[-2]
    eye = jnp.eye(k, dtype=w.dtype)
    bd = jnp.einsum('...kce,kj->...kcje', w, eye)
    return bd.reshape(*w.shape[:-3], k * c, k * c)


def kernel(x, c, ctx, c_ctx, w_ada, b_ada, w_in, ret_decay_logit, attn_q_gain, attn_k_gain,
           lru_conv_w, lru_conv_b, lru_w_a, lru_b_a, lru_w_x, lru_b_x, lru_lambda,
           w_out, ln1_g, ln1_b, w_ff1, w_ff2, ln2_g, ln2_b):
    b, n_lat, d = x.shape
    n_ctx = ctx.shape[1]
    depth = w_in.shape[0]
    s = n_lat + n_ctx
    rw, aw, kw = d // 4, d // 2, d // 8
    alpha = (2.0 * depth) ** 0.25
    assert n_lat % CHUNK == 0 and n_ctx % CHUNK == 0 and d == 16 * HEAD_DIM

    pad = (-(b + 1)) % SUBLANES
    s_in = jnp.concatenate([c, c_ctx[None, :], jnp.zeros((pad, d), F32)], axis=0)
    mods = _ada_call(s_in, w_ada, b_ada)
    mod_lat = mods[:, :b].reshape(depth, b, 1, 6, d)
    mod_ctx = jnp.broadcast_to(mods[:, b].reshape(depth, 1, 1, 6, d), (depth, b, 1, 6, d))
    mod_all = jnp.concatenate([mod_lat, mod_ctx], axis=2)

    o_aq = 4 * rw
    o_lx = o_aq + aw + 2 * kw
    wn = jnp.concatenate([w_in[:, :, :o_aq], w_in[:, :, o_lx:]], axis=2).astype(BF16)
    wt = jnp.swapaxes(w_in[:, :, o_aq:o_lx], 1, 2).astype(BF16)
    wg = (0.5 * jnp.concatenate([_block_diag(lru_w_a[:, 0]), _block_diag(lru_w_x[:, 0]),
                                 _block_diag(lru_w_a[:, 1]), _block_diag(lru_w_x[:, 1])],
                                axis=-1)).astype(BF16)
    bg = 0.5 * jnp.concatenate([lru_b_a[:, 0], lru_b_x[:, 0], lru_b_a[:, 1], lru_b_x[:, 1]],
                               axis=-1)[:, None, :]
    w_out_b = w_out.astype(BF16)
    w1_b = w_ff1.astype(BF16)
    w2_b = w_ff2.astype(BF16)
    dl_lane = jnp.repeat(ret_decay_logit, HEAD_DIM, axis=-1)
    cos_t, sin_t = _rope_tables(n_lat, n_ctx)

    def tail_tile(rows):
        return rows // TAIL_TILES if rows % (TAIL_TILES * BF16_ROWS) == 0 else CHUNK

    xa = (x, ctx)
    for l in range(depth):
        need_ctx = l < depth - 1
        mod = mod_all[l]
        outs = _inproj_call(xa, mod, wn[l], wt[l], cos_t, sin_t,
                            attn_q_gain[l][:, None], attn_k_gain[l][:, None], n_lat)
        ret_in, g_in, lru_in, qt, k, vt, kn = outs[:7]
        if l == 0:
            xa = outs[7]
        ret = _ret_call(ret_in, g_in, dl_lane[l], n_lat)
        att = _attn_call(qt, k, vt, kn, n_lat, need_ctx)
        af, df, ab, db = _lru_pre_call(lru_in, lru_conv_w[l], lru_conv_b[l][None, :], wg[l], bg[l],
                                       lru_lambda[l], n_lat)
        hf, hb = _scan_call(af, df, ab, db, n_lat)
        n_rows = s if need_ctx else n_lat
        xa = _tail_call(xa, ret, att, hf, hb, lru_in, mod,
                        w_out_b[l], ln1_g[l][None, :], ln1_b[l][None, :], w1_b[l], w2_b[l],
                        ln2_g[l][None, :], ln2_b[l][None, :], n_lat, n_rows, alpha, tail_tile(n_rows))
    return xa
```

```python
import functools

import jax
import jax.numpy as jnp
from jax import lax
from jax.experimental import pallas as pl
from jax.experimental.pallas import tpu as pltpu

F32 = jnp.float32
BF16 = jnp.bfloat16

HEAD_DIM = 64
GRID_W = 64
ROPE_THETA = 10000.0
LRU_C = 8.0
EPS = 1e-6
LOG2E = 1.4426950408889634

SUBLANES = 8
BF16_ROWS = 16
ONES_ROWS = BF16_ROWS
SCORE_BOUND_MARGIN = 1.0 + 2.0 ** -7
SCORE_BOUND_CAP = 60.0
PV_LAG = 2
TAIL_TILES = 8
TAIL_SUB_BLOCKS = 2
CHUNK = 256
RET_GROUP = 8
SCAN_TILE = 128
LRU_SUB_TOKENS = 64
VMEM_LIMIT = 56 * 1024 * 1024


def _cparams(sem):
    return pltpu.CompilerParams(dimension_semantics=sem, vmem_limit_bytes=VMEM_LIMIT)


def _log_sigmoid(x):
    return jnp.minimum(x, 0.0) - jnp.log1p(jnp.exp(-jnp.abs(x)))


def _layer_norm(z, g, b):
    mu = jnp.mean(z, axis=-1, keepdims=True)
    zc = z - mu
    var = jnp.mean(zc * zc, axis=-1, keepdims=True)
    return zc * lax.rsqrt(var + EPS) * g + b


def _row_mod(mod_ref, idx, is_ctx):
    return jnp.where(is_ctx, mod_ref[1, idx:idx + 1, :], mod_ref[0, idx:idx + 1, :])


def _ctx_rows(tile_idx, tm, n_lat):
    rows = tile_idx * tm + lax.broadcasted_iota(jnp.int32, (tm, 1), 0)
    return rows >= n_lat


def _ada_kernel(s_ref, w_ref, b_ref, o_ref):
    s = s_ref[...]
    s = s * jax.nn.sigmoid(s)
    o_ref[...] = jnp.dot(s.astype(BF16), w_ref[...].astype(BF16),
                         preferred_element_type=F32) + b_ref[...]


def _ada_call(s_in, w_ada, b_ada):
    depth, d, d6 = w_ada.shape
    rows = s_in.shape[0]
    tn = d6 // 4
    return pl.pallas_call(
        _ada_kernel,
        grid=(depth, d6 // tn),
        in_specs=[
            pl.BlockSpec((rows, d), lambda l, j: (0, 0)),
            pl.BlockSpec((None, d, tn), lambda l, j: (l, 0, j)),
            pl.BlockSpec((None, 1, tn), lambda l, j: (l, 0, j)),
        ],
        out_specs=pl.BlockSpec((None, rows, tn), lambda l, j: (l, 0, j)),
        out_shape=jax.ShapeDtypeStruct((depth, rows, d6), F32),
        compiler_params=_cparams(("parallel", "parallel")),
    )(s_in, w_ada, b_ada.reshape(depth, 1, d6))


def _inproj_kernel(x_ref, *refs, n_lat, joins_streams):
    if joins_streams:
        ctx_ref, refs, xa_ref = refs[0], refs[1:-1], refs[-1]
    (mod_ref, wn_ref, wt_ref, cos_ref, sin_ref, qg_ref, kg_ref,
     ret_ref, g_ref, lru_ref, qt_ref, k_ref, vt_ref, kn_ref) = refs
    tm = x_ref.shape[0]
    is_ctx = _ctx_rows(pl.program_id(1), tm, n_lat)
    if joins_streams:
        x = jnp.where(pl.program_id(1) * tm >= n_lat, ctx_ref[...], x_ref[...])
        xa_ref[...] = x
    else:
        x = x_ref[...]
    u = (x * (1.0 + _row_mod(mod_ref, 1, is_ctx)) + _row_mod(mod_ref, 0, is_ctx)).astype(BF16)

    pt = lax.dot_general(wt_ref[...], u, (((1,), (1,)), ((), ())), preferred_element_type=F32)

    pn = jnp.dot(u, wn_ref[...], preferred_element_type=F32)
    rw = g_ref.shape[1]
    ret_ref[:, 0:rw] = pn[:, 0:rw].astype(BF16)
    ret_ref[:, rw:2 * rw] = (pn[:, rw:2 * rw] * HEAD_DIM ** -0.5).astype(BF16)
    ret_ref[:, 2 * rw:3 * rw] = pn[:, 2 * rw:3 * rw].astype(BF16)
    g_ref[...] = pn[:, 3 * rw:4 * rw]
    lru_ref[...] = pn[:, 4 * rw:]

    cos = cos_ref[...]
    sin = sin_ref[...]
    half = HEAD_DIM // 2

    def norm_rope(t, gain):
        ms = jnp.mean(t * t, axis=0, keepdims=True)
        t = t * lax.rsqrt(ms + EPS) * gain
        x1, x2 = t[:half], t[half:]
        return jnp.concatenate([x1 * cos - x2 * sin, x1 * sin + x2 * cos], axis=0)

    qw = qt_ref.shape[0]
    kw = k_ref.shape[1]
    qscale = HEAD_DIM ** -0.5 * LOG2E
    for h in range(qw // HEAD_DIM):
        r = h * HEAD_DIM
        qt_ref[r:r + HEAD_DIM, :] = (norm_rope(pt[r:r + HEAD_DIM], qg_ref[...]) * qscale).astype(BF16)
    kt = jnp.concatenate(
        [norm_rope(pt[qw + h * HEAD_DIM:qw + (h + 1) * HEAD_DIM], kg_ref[...])
         for h in range(kw // HEAD_DIM)], axis=0)
    k_ref[...] = kt.T.astype(BF16)
    kf = kt.astype(BF16).astype(F32)
    for h in range(kw // HEAD_DIM):
        kh = kf[h * HEAD_DIM:(h + 1) * HEAD_DIM]
        kn_ref[h:h + 1, :] = jnp.sum(kh * kh, axis=0, keepdims=True)
    ones = jnp.ones((ONES_ROWS, tm), BF16)
    vrows = HEAD_DIM + ONES_ROWS
    for h in range(kw // HEAD_DIM):
        v0 = qw + kw + h * HEAD_DIM
        vt_ref[h * vrows:h * vrows + HEAD_DIM, :] = pt[v0:v0 + HEAD_DIM].astype(BF16)
        vt_ref[h * vrows + HEAD_DIM:(h + 1) * vrows, :] = ones


def _inproj_call(streams, mod, wn, wt, cos_t, sin_t, qg, kg, n_lat, tm=256):
    joins = isinstance(streams, tuple)
    b, _, d = streams[0].shape if joins else streams.shape
    s = n_lat + streams[1].shape[1] if joins else streams.shape[1]
    assert n_lat % tm == 0 and s % tm == 0
    nlt = n_lat // tm
    rw, qw, kw = d // 4, d // 2, d // 8
    vw = (kw // HEAD_DIM) * (HEAD_DIM + ONES_ROWS)
    grid = (b, s // tm)
    tok = lambda shape_last: pl.BlockSpec((None, tm, shape_last), lambda i, j: (i, j, 0))
    tr = lambda rows: pl.BlockSpec((None, rows, tm), lambda i, j: (i, 0, j))
    const = lambda shape: pl.BlockSpec(shape, lambda i, j: tuple(0 for _ in shape))
    if joins:
        x_specs = [pl.BlockSpec((None, tm, d), lambda i, j: (i, jnp.minimum(j, nlt - 1), 0)),
                   pl.BlockSpec((None, tm, d), lambda i, j: (i, jnp.maximum(j - nlt, 0), 0))]
        x_args = list(streams)
    else:
        x_specs, x_args = [tok(d)], [streams]
    return pl.pallas_call(
        functools.partial(_inproj_kernel, n_lat=n_lat, joins_streams=joins),
        grid=grid,
        in_specs=x_specs + [
            pl.BlockSpec((None, 2, 6, d), lambda i, j: (i, 0, 0, 0)),
            const(wn.shape), const(wt.shape),
            pl.BlockSpec((HEAD_DIM // 2, tm), lambda i, j: (0, j)),
            pl.BlockSpec((HEAD_DIM // 2, tm), lambda i, j: (0, j)),
            const(qg.shape), const(kg.shape),
        ],
        out_specs=[tok(3 * rw), tok(rw), tok(2 * rw), tr(qw), tok(kw),
                   pl.BlockSpec((None, None, vw, tm), lambda i, j: (i, j, 0, 0)),
                   tr(kw // HEAD_DIM)] + ([tok(d)] if joins else []),
        out_shape=[
            jax.ShapeDtypeStruct((b, s, 3 * rw), BF16),
            jax.ShapeDtypeStruct((b, s, rw), F32),
            jax.ShapeDtypeStruct((b, s, 2 * rw), F32),
            jax.ShapeDtypeStruct((b, qw, s), BF16),
            jax.ShapeDtypeStruct((b, s, kw), BF16),
            jax.ShapeDtypeStruct((b, s // tm, vw, tm), BF16),
            jax.ShapeDtypeStruct((b, kw // HEAD_DIM, s), F32),
        ] + ([jax.ShapeDtypeStruct((b, s, d), F32)] if joins else []),
        compiler_params=_cparams(("parallel", "parallel")),
    )(*x_args, mod, wn, wt, cos_t, sin_t, qg, kg)


def _ret_kernel(ret_ref, g_ref, dl_ref, o_ref, sf_ref, sb_ref, dm_ref, *, n_lat_chunks):
    c_len = CHUNK
    s_len, rw = g_ref.shape
    nc = s_len // c_len
    ncl = n_lat_chunks
    n_heads = rw // HEAD_DIM

    lg = _log_sigmoid(dl_ref[...])
    lgf, lgb = lg[0:1], lg[1:2]
    pos = lax.broadcasted_iota(jnp.int32, (c_len, 1), 0).astype(F32)
    kwf = jnp.exp(lgf * (c_len - 1.0 - pos))
    kwb = jnp.exp(lgb * pos)
    qwf = jnp.exp(lgf * (pos + 1.0))
    qwb = jnp.exp(lgb * (c_len - pos))
    cdf = jnp.exp(lgf * c_len)
    cdb = jnp.exp(lgb * c_len)
    row_head = lax.broadcasted_iota(jnp.int32, (rw, rw), 0) // HEAD_DIM
    col_head = lax.broadcasted_iota(jnp.int32, (rw, rw), 1) // HEAD_DIM
    same_head = row_head == col_head
    lane_head = lax.broadcasted_iota(jnp.int32, (1, rw), 1) // HEAD_DIM

    ii = lax.broadcasted_iota(jnp.int32, (c_len, c_len), 0)
    jj = lax.broadcasted_iota(jnp.int32, (c_len, c_len), 1)
    dij = (ii - jj).astype(F32)
    for h in range(n_heads):
        lf = lgf[:, h * HEAD_DIM:h * HEAD_DIM + 1]
        lb = lgb[:, h * HEAD_DIM:h * HEAD_DIM + 1]
        dm_ref[h] = jnp.exp(jnp.where(dij >= 0.0, lf * dij, -lb * dij))

    def contrib(chunks):
        tn = (((0,), (0,)), ((), ()))
        prods = []
        for c in chunks:
            r0 = c * c_len if isinstance(c, int) else pl.multiple_of(c * c_len, c_len)
            k = ret_ref[pl.ds(r0, c_len), rw:2 * rw].astype(F32)
            v = ret_ref[pl.ds(r0, c_len), 2 * rw:3 * rw]
            prods.append((lax.dot_general((k * kwf).astype(BF16), v, tn, preferred_element_type=F32),
                          lax.dot_general((k * kwb).astype(BF16), v, tn, preferred_element_type=F32)))
        for c, (cf, cb) in zip(chunks, prods):
            sf_ref[c] = jnp.where(same_head, cf, 0.0)
            sb_ref[c] = jnp.where(same_head, cb, 0.0)

    def contrib_group(i, carry):
        contrib([RET_GROUP * i + j for j in range(RET_GROUP)])
        return carry

    lax.fori_loop(0, nc // RET_GROUP, contrib_group, 0)
    if nc % RET_GROUP:
        contrib(list(range(nc - nc % RET_GROUP, nc)))

    def chain(ref, decay, order):
        state = jnp.zeros((rw, rw), F32)
        for c in order:
            u = ref[c]
            ref[c] = state
            state = decay * state + u

    chain(sf_ref, cdf, list(range(ncl, nc)) + list(range(ncl)))
    chain(sb_ref, cdb, list(range(nc - 1, ncl - 1, -1)) + list(range(ncl - 1, -1, -1)))

    ones_blk = jnp.where(same_head, 1.0, 0.0).astype(BF16)

    def outputs(chunks):
        rows = [c * c_len if isinstance(c, int) else pl.multiple_of(c * c_len, c_len) for c in chunks]
        qs = [ret_ref[pl.ds(r0, c_len), 0:rw] for r0 in rows]
        ks = [ret_ref[pl.ds(r0, c_len), rw:2 * rw] for r0 in rows]
        vs = [ret_ref[pl.ds(r0, c_len), 2 * rw:3 * rw] for r0 in rows]
        os_, scs = [], []
        for c, q, k in zip(chunks, qs, ks):
            qf = q.astype(F32)
            qi = jnp.concatenate([(qf * qwf).astype(BF16), (qf * qwb).astype(BF16)], axis=1)
            st = jnp.concatenate([sf_ref[c], sb_ref[c]], axis=0).astype(BF16)
            os_.append(jnp.dot(qi, st, preferred_element_type=F32))
            scs.append([lax.dot_general(jnp.where(lane_head == h, q, jnp.zeros_like(q)), k,
                                        (((1,), (1,)), ((), ())), preferred_element_type=F32)
                        for h in range(n_heads)])
        for i, v in enumerate(vs):
            for h in range(n_heads):
                p = (scs[i][h] * dm_ref[h]).astype(BF16)
                os_[i] = os_[i] + jnp.where(lane_head == h, jnp.dot(p, v, preferred_element_type=F32), 0.0)
        mss = []
        for o in os_:
            o2 = o * o
            hi = o2.astype(BF16)
            lo = (o2 - hi.astype(F32)).astype(BF16)
            mss.append((jnp.dot(hi, ones_blk, preferred_element_type=F32)
                        + jnp.dot(lo, ones_blk, preferred_element_type=F32)) * (1.0 / HEAD_DIM))
        for r0, o, ms in zip(rows, os_, mss):
            g = g_ref[pl.ds(r0, c_len), :]
            o_ref[pl.ds(r0, c_len), :] = (o * lax.rsqrt(ms + EPS) * (g * jax.nn.sigmoid(g))).astype(BF16)

    def output_group(i, carry):
        outputs([RET_GROUP * i + j for j in range(RET_GROUP)])
        return carry

    lax.fori_loop(0, nc // RET_GROUP, output_group, 0)
    if nc % RET_GROUP:
        outputs(list(range(nc - nc % RET_GROUP, nc)))


def _ret_call(ret_in, g_in, dl_lane, n_lat):
    b, s, rw = g_in.shape
    nc = s // CHUNK
    return pl.pallas_call(
        functools.partial(_ret_kernel, n_lat_chunks=n_lat // CHUNK),
        grid=(b,),
        in_specs=[
            pl.BlockSpec((None, s, 3 * rw), lambda i: (i, 0, 0)),
            pl.BlockSpec((None, s, rw), lambda i: (i, 0, 0)),
            pl.BlockSpec((2, rw), lambda i: (0, 0)),
        ],
        out_specs=pl.BlockSpec((None, s, rw), lambda i: (i, 0, 0)),
        out_shape=jax.ShapeDtypeStruct((b, s, rw), BF16),
        scratch_shapes=[
            pltpu.VMEM((nc, rw, rw), F32),
            pltpu.VMEM((nc, rw, rw), F32),
            pltpu.VMEM((rw // HEAD_DIM, CHUNK, CHUNK), F32),
        ],
        compiler_params=_cparams(("parallel",)),
    )(ret_in, g_in, dl_lane)


def _attn_kernel(qt_ref, k_ref, vt_ref, kn_ref, *rest, group):
    o_ref, rhs_ref, sa_ref, sb_ref, pc_ref, m_ref, acc_ref, out_ref = rest[-8:]
    n_q_heads = qt_ref.shape[0] // HEAD_DIM
    n_chunks = vt_ref.shape[0]
    kc_len = vt_ref.shape[2]
    vrows = vt_ref.shape[1] // (n_q_heads // group)

    def scores(c, h):
        c0 = c * kc_len
        if not isinstance(c0, int):
            c0 = pl.multiple_of(c0, kc_len)
        return jnp.dot(k_ref[pl.ds(c0, kc_len), :], rhs_ref[h], preferred_element_type=F32)

    def p_times_v(c, h, p):
        kv = h // group
        return jnp.dot(vt_ref[c, kv * vrows:(kv + 1) * vrows, :], p, preferred_element_type=F32)

    kmax2 = jnp.max(kn_ref[...], axis=1, keepdims=True)
    for h in range(n_q_heads):
        qt = qt_ref[h * HEAD_DIM:(h + 1) * HEAD_DIM, :]
        zero = jnp.zeros_like(qt)
        rhs_ref[h] = jnp.concatenate([qt, zero] if h // group == 0 else [zero, qt], axis=0)
        sa_ref[h] = scores(0, h)
        qf = qt.astype(F32)
        qn2 = jnp.sum(qf * qf, axis=0, keepdims=True)
        m_ref[h:h + 1, :] = jnp.sqrt(qn2 * kmax2[h // group:h // group + 1, :]) * SCORE_BOUND_MARGIN
    acc_ref[...] = jnp.zeros(acc_ref.shape, F32)
    bounded = jnp.max(m_ref[...]) <= SCORE_BOUND_CAP

    def run_chunks(step):
        def pair(i, carry):
            step(2 * i, sa_ref, sb_ref)
            step(2 * i + 1, sb_ref, sa_ref)
            return carry

        lax.fori_loop(0, (n_chunks - 1) // 2, pair, 0)
        if (n_chunks - 1) % 2 == 1:
            step(n_chunks - 2, sa_ref, sb_ref)
            step(n_chunks - 1, sb_ref, None)
        else:
            step(n_chunks - 1, sa_ref, None)

    @pl.when(bounded)
    def _():
        lag = PV_LAG
        pc_ref[...] = jnp.zeros(pc_ref.shape, BF16)

        def step(c, cur_ref, nxt_ref):
            ps = [pc_ref[i] for i in range(lag)]
            c_prev = max(c - 1, 0) if isinstance(c, int) else jnp.maximum(c - 1, 0)
            for h in range(n_q_heads):
                if nxt_ref is not None:
                    nxt_ref[h] = scores(c + 1, h)
                if h < lag:
                    hp = n_q_heads - lag + h
                    acc_ref[hp] += p_times_v(c_prev, hp, ps[h])
                else:
                    acc_ref[h - lag] += p_times_v(c, h - lag, ps[h])
                ps.append(jnp.exp2(cur_ref[h] - m_ref[h:h + 1, :]).astype(BF16))
            for i in range(lag):
                pc_ref[i] = ps[n_q_heads + i]

        run_chunks(step)
        for i in range(lag):
            hp = n_q_heads - lag + i
            acc_ref[hp] += p_times_v(n_chunks - 1, hp, pc_ref[i])

    @pl.when(jnp.logical_not(bounded))
    def _():
        m_ref[...] = jnp.full(m_ref.shape, -jnp.inf, F32)

        def step(c, cur_ref, nxt_ref):
            for h in range(n_q_heads):
                if nxt_ref is not None:
                    nxt_ref[h] = scores(c + 1, h)
                s = cur_ref[h]
                m_old = m_ref[h:h + 1, :]
                m_new = jnp.maximum(m_old, jnp.max(s, axis=0, keepdims=True))
                alpha = jnp.exp2(m_old - m_new)
                m_ref[h:h + 1, :] = m_new
                p = jnp.exp2(s - m_new).astype(BF16)
                acc_ref[h] = alpha * acc_ref[h] + p_times_v(c, h, p)

        run_chunks(step)

    for h in range(n_q_heads):
        a = acc_ref[h]
        out_ref[h * HEAD_DIM:(h + 1) * HEAD_DIM, :] = a[:HEAD_DIM] / a[HEAD_DIM:HEAD_DIM + 1]
    o_ref[...] = out_ref[...].T.astype(BF16)


def _attn_call(qt, k, vt, kn, n_lat, recycled, tq_lat=512, tq_ctx=256):
    b, qw, s = qt.shape
    kw = k.shape[2]
    vw, kc_len = vt.shape[2], vt.shape[3]
    n_ctx = s - n_lat
    n_heads = qw // HEAD_DIM
    n_kv = kw // HEAD_DIM
    assert n_kv == 2 and n_lat % n_ctx == 0 and n_ctx % kc_len == 0
    tq_lat = min(tq_lat, n_lat)
    tq_ctx = min(tq_ctx, n_ctx)
    assert n_lat % tq_lat == 0 and n_ctx % tq_ctx == 0
    body = functools.partial(_attn_kernel, group=n_heads // n_kv)
    out_sds = jax.ShapeDtypeStruct((b, s, qw), BF16)

    def scratch(tq):
        return [pltpu.VMEM((n_heads, kw, tq), BF16),
                pltpu.VMEM((n_heads, kc_len, tq), F32), pltpu.VMEM((n_heads, kc_len, tq), F32),
                pltpu.VMEM((PV_LAG, kc_len, tq), BF16),
                pltpu.VMEM((n_heads, tq), F32), pltpu.VMEM((n_heads, vw // n_kv, tq), F32),
                pltpu.VMEM((qw, tq), F32)]

    with_ctx = recycled is not None
    att = pl.pallas_call(
        body,
        grid=(b, n_lat // tq_lat),
        in_specs=[
            pl.BlockSpec((None, qw, tq_lat), lambda i, j: (i, 0, j)),
            pl.BlockSpec((None, s, kw), lambda i, j: (i, 0, 0)),
            pl.BlockSpec((None, s // kc_len, vw, kc_len), lambda i, j: (i, 0, 0, 0)),
            pl.BlockSpec((None, n_kv, s), lambda i, j: (i, 0, 0)),
        ] + ([pl.BlockSpec(memory_space=pl.ANY)] if with_ctx else []),
        out_specs=pl.BlockSpec((None, tq_lat, qw), lambda i, j: (i, j, 0)),
        out_shape=out_sds if with_ctx else jax.ShapeDtypeStruct((b, n_lat, qw), BF16),
        input_output_aliases={4: 0} if with_ctx else {},
        scratch_shapes=scratch(tq_lat),
        compiler_params=_cparams(("parallel", "arbitrary")),
    )(qt, k, vt, kn, *([recycled] if with_ctx else []))
    if not with_ctx:
        return att
    lat_tiles, lat_ctx = n_lat // tq_ctx, n_lat // n_ctx
    return pl.pallas_call(
        body,
        grid=(b, n_ctx // tq_ctx),
        in_specs=[
            pl.BlockSpec((None, qw, tq_ctx), lambda i, j: (i, 0, lat_tiles + j)),
            pl.BlockSpec((None, n_ctx, kw), lambda i, j: (i, lat_ctx, 0)),
            pl.BlockSpec((None, n_ctx // kc_len, vw, kc_len), lambda i, j: (i, lat_ctx, 0, 0)),
            pl.BlockSpec((None, n_kv, n_ctx), lambda i, j: (i, 0, lat_ctx)),
            pl.BlockSpec(memory_space=pl.ANY),
        ],
        out_specs=pl.BlockSpec((None, tq_ctx, qw), lambda i, j: (i, lat_tiles + j, 0)),
        out_shape=out_sds,
        input_output_aliases={4: 0},
        scratch_shapes=scratch(tq_ctx),
        compiler_params=_cparams(("parallel", "arbitrary")),
    )(qt, k, vt, kn, att)


def _lru_pre_kernel(cur_ref, prev_ref, next_ref, cw_ref, cb_ref, wg_ref, bg_ref, lam_ref,
                    af_ref, df_ref, ab_ref, db_ref, xs_ref, *, n_lat_chunks, n_chunks):
    c = pl.program_id(0)
    n_b, c_len, w = cur_ref.shape
    first = jnp.logical_or(c == 0, c == n_lat_chunks)
    last = jnp.logical_or(c == n_lat_chunks - 1, c == n_chunks - 1)
    halo = prev_ref.shape[1]
    xs_ref[0:halo] = jnp.where(first, 0.0, jnp.swapaxes(prev_ref[...], 0, 1))
    xs_ref[halo:halo + c_len] = jnp.swapaxes(cur_ref[...], 0, 1)
    xs_ref[halo + c_len:] = jnp.where(last, 0.0, jnp.swapaxes(next_ref[...], 0, 1))
    half_c_lam = (0.5 * LRU_C) * _log_sigmoid(lam_ref[...])
    ts = LRU_SUB_TOKENS

    def sub_chunk(i, carry):
        t0 = pl.multiple_of(i * ts, ts)

        def shifted(off):
            return xs_ref[pl.ds(halo + t0 + off, ts)]

        xr = (shifted(-2) * cw_ref[0:1, :] + shifted(-1) * cw_ref[1:2, :] + shifted(0) * cw_ref[2:3, :]
              + shifted(1) * cw_ref[3:4, :] + cb_ref[...]).reshape(ts * n_b, w)
        half_gates = jnp.dot(xr.astype(BF16), wg_ref[...], preferred_element_type=F32) + bg_ref[...]
        half_xr = 0.5 * xr
        for d, (a_ref, d_ref) in enumerate(((af_ref, df_ref), (ab_ref, db_ref))):
            t_r = jnp.tanh(half_gates[:, 2 * d * w:(2 * d + 1) * w])
            t_i = jnp.tanh(half_gates[:, (2 * d + 1) * w:(2 * d + 2) * w])
            half_c = half_c_lam[d:d + 1, :]
            log_a = half_c * t_r + half_c
            a = jnp.exp(log_a)
            om = (1.0 + a * a) * jnp.tanh(-log_a)
            drive = om * lax.rsqrt(jnp.maximum(om, 1e-30)) * (half_xr * t_i + half_xr)
            a_ref[pl.ds(t0, ts)] = a.reshape(ts, n_b, w)
            d_ref[pl.ds(t0, ts)] = drive.reshape(ts, n_b, w)
        return carry

    lax.fori_loop(0, c_len // ts, sub_chunk, 0)


def _lru_pre_call(lru_in, conv_w, conv_b, wg, bg, lam, n_lat):
    b, s, w2 = lru_in.shape
    w = w2 // 2
    nc = s // CHUNK
    halo = SUBLANES
    hb = CHUNK // halo
    const = lambda shape: pl.BlockSpec(shape, lambda j: tuple(0 for _ in shape))
    out_spec = pl.BlockSpec((CHUNK, b, w), lambda j: (j, 0, 0))
    out_sds = jax.ShapeDtypeStruct((s, b, w), F32)
    return pl.pallas_call(
        functools.partial(_lru_pre_kernel, n_lat_chunks=n_lat // CHUNK, n_chunks=nc),
        grid=(nc,),
        in_specs=[
            pl.BlockSpec((b, CHUNK, w), lambda j: (0, j, 0)),
            pl.BlockSpec((b, halo, w), lambda j: (0, jnp.maximum(j * hb - 1, 0), 0)),
            pl.BlockSpec((b, halo, w), lambda j: (0, jnp.minimum((j + 1) * hb, s // halo - 1), 0)),
            const(conv_w.shape), const(conv_b.shape), const(wg.shape), const(bg.shape),
            const(lam.shape),
        ],
        out_specs=[out_spec] * 4,
        out_shape=[out_sds] * 4,
        scratch_shapes=[pltpu.VMEM((CHUNK + 2 * halo, b, w), F32)],
        compiler_params=_cparams(("parallel",)),
    )(lru_in, lru_in, lru_in, conv_w, conv_b, wg, bg, lam)


def _scan_kernel(af_ref, df_ref, ab_ref, db_ref, hf_ref, hb_ref, sf_ref, sb_ref, tf_ref, tb_ref):
    tt = af_ref.shape[0]

    @pl.when(pl.program_id(0) == 0)
    def _():
        sf_ref[...] = jnp.zeros_like(sf_ref)
        sb_ref[...] = jnp.zeros_like(sb_ref)

    def step(t, carry):
        hf, hb = carry
        hf = af_ref[t] * hf + df_ref[t]
        tf_ref[t] = hf
        tb = tt - 1 - t
        hb = ab_ref[tb] * hb + db_ref[tb]
        tb_ref[tb] = hb
        return hf, hb

    hf, hb = lax.fori_loop(0, tt, step, (sf_ref[...], sb_ref[...]), unroll=8)
    sf_ref[...] = hf
    sb_ref[...] = hb
    hf_ref[...] = jnp.swapaxes(tf_ref[...], 0, 1)
    hb_ref[...] = jnp.swapaxes(tb_ref[...], 0, 1)


def _scan_call(af, df, ab, db, n_lat):
    s, b, w = af.shape
    nt = s // SCAN_TILE
    nlt = n_lat // SCAN_TILE
    fwd = pl.BlockSpec((SCAN_TILE, b, w), lambda i: ((i + nlt) % nt, 0, 0))
    bwd = pl.BlockSpec((SCAN_TILE, b, w), lambda i: (nt - 1 - i, 0, 0))
    fwd_out = pl.BlockSpec((b, SCAN_TILE, w), lambda i: (0, (i + nlt) % nt, 0))
    bwd_out = pl.BlockSpec((b, SCAN_TILE, w), lambda i: (0, nt - 1 - i, 0))
    sds = jax.ShapeDtypeStruct((b, s, w), F32)
    return pl.pallas_call(
        _scan_kernel,
        grid=(nt,),
        in_specs=[fwd, fwd, bwd, bwd],
        out_specs=[fwd_out, bwd_out],
        out_shape=[sds, sds],
        scratch_shapes=[pltpu.VMEM((b, w), F32), pltpu.VMEM((b, w), F32),
                        pltpu.VMEM((SCAN_TILE, b, w), F32), pltpu.VMEM((SCAN_TILE, b, w), F32)],
        compiler_params=_cparams(("arbitrary",)),
    )(af, df, ab, db)


def _row_splits(tm, n_sub):
    units = tm // BF16_ROWS
    assert tm % BF16_ROWS == 0 and units >= n_sub
    cuts = [BF16_ROWS * ((units * i) // n_sub) for i in range(n_sub + 1)]
    return list(zip(cuts[:-1], cuts[1:]))


def _tail_kernel(x_ref, ret_ref, att_ref, hf_ref, hb_ref, lg_ref, mod_ref, wo_ref, g1_ref, b1_ref,
                 w1_ref, w2_ref, g2_ref, b2_ref, o_ref, x1_ref, u_ref, acc_ref, *, n_lat, alpha):
    tm = x_ref.shape[0]
    kf = pl.program_id(2)
    nk = pl.num_programs(2)
    subs = _row_splits(tm, TAIL_SUB_BLOCKS)
    rw = ret_ref.shape[1]
    aw = att_ref.shape[1]

    def ctx_rows(r0, r1):
        rows = pl.program_id(1) * tm + r0 + lax.broadcasted_iota(jnp.int32, (r1 - r0, 1), 0)
        return rows >= n_lat

    def mlp_part(r0, r1):
        h = jnp.maximum(jnp.dot(u_ref[r0:r1, :], w1_ref[...], preferred_element_type=F32), 0.0)
        return jnp.dot((h * h).astype(BF16), w2_ref[...], preferred_element_type=F32)

    @pl.when(kf == 0)
    def _():
        for r0, r1 in subs:
            is_ctx = ctx_rows(r0, r1)
            lru = ((hf_ref[r0:r1, :] + hb_ref[r0:r1, :]) * jax.nn.gelu(lg_ref[r0:r1, :])).astype(BF16)
            y = jnp.dot(ret_ref[r0:r1, :], wo_ref[0:rw, :], preferred_element_type=F32)
            y = y + jnp.dot(att_ref[r0:r1, :], wo_ref[rw:rw + aw, :], preferred_element_type=F32)
            y = y + jnp.dot(lru, wo_ref[rw + aw:, :], preferred_element_type=F32)
            x1 = _layer_norm(alpha * x_ref[r0:r1, :] + _row_mod(mod_ref, 2, is_ctx) * y,
                             g1_ref[...], b1_ref[...])
            x1_ref[r0:r1, :] = x1
            u_ref[r0:r1, :] = (x1 * (1.0 + _row_mod(mod_ref, 4, is_ctx))
                               + _row_mod(mod_ref, 3, is_ctx)).astype(BF16)
        for r0, r1 in subs:
            acc_ref[r0:r1, :] = mlp_part(r0, r1)

    @pl.when(jnp.logical_and(kf > 0, kf < nk - 1))
    def _():
        acc_ref[...] += mlp_part(0, tm)

    @pl.when(kf == nk - 1)
    def _():
        for r0, r1 in subs:
            a = acc_ref[r0:r1, :] + mlp_part(r0, r1)
            z = alpha * x1_ref[r0:r1, :] + _row_mod(mod_ref, 5, ctx_rows(r0, r1)) * a
            o_ref[r0:r1, :] = _layer_norm(z, g2_ref[...], b2_ref[...])


def _tail_call(x_all, ret, att, hf, hb, lru_in, mod, w_out, ln1_g, ln1_b, w1, w2, ln2_g, ln2_b,
               n_lat, n_rows, alpha, tm, tf=2048):
    b, _, d = x_all.shape
    rw, aw = ret.shape[2], att.shape[2]
    f = w1.shape[1]
    assert f // tf >= 2
    tok = lambda last: pl.BlockSpec((None, tm, last), lambda i, j, k: (i, j, 0))
    const = lambda shape: pl.BlockSpec(shape, lambda i, j, k: tuple(0 for _ in shape))
    return pl.pallas_call(
        functools.partial(_tail_kernel, n_lat=n_lat, alpha=alpha),
        grid=(b, n_rows // tm, f // tf),
        in_specs=[
            tok(d), tok(rw), tok(aw), tok(rw), tok(rw),
            pl.BlockSpec((None, tm, rw), lambda i, j, k: (i, j, 1)),
            pl.BlockSpec((None, 2, 6, d), lambda i, j, k: (i, 0, 0, 0)),
            const(w_out.shape), const(ln1_g.shape), const(ln1_b.shape),
            pl.BlockSpec((d, tf), lambda i, j, k: (0, k)),
            pl.BlockSpec((tf, d), lambda i, j, k: (k, 0)),
            const(ln2_g.shape), const(ln2_b.shape),
        ],
        out_specs=tok(d),
        out_shape=jax.ShapeDtypeStruct((b, n_rows, d), F32),
        scratch_shapes=[pltpu.VMEM((tm, d), F32), pltpu.VMEM((tm, d), BF16), pltpu.VMEM((tm, d), F32)],
        compiler_params=_cparams(("parallel", "parallel", "arbitrary")),
    )(x_all, ret, att, hf, hb, lru_in, mod, w_out, ln1_g, ln1_b, w1, w2, ln2_g, ln2_b)


def _rope_tables(n_lat, n_ctx):
    rows = n_lat // GRID_W
    row = jnp.repeat(jnp.arange(rows, dtype=F32), GRID_W)
    col = jnp.tile(jnp.arange(GRID_W, dtype=F32), rows)
    n_freq = HEAD_DIM // 4
    inv = ROPE_THETA ** (-jnp.arange(n_freq, dtype=F32) / n_freq)
    ang = jnp.concatenate([row[:, None] * inv, col[:, None] * inv], axis=-1)
    cos = jnp.concatenate([jnp.cos(ang), jnp.ones((n_ctx, HEAD_DIM // 2), F32)], axis=0)
    sin = jnp.concatenate([jnp.sin(ang), jnp.zeros((n_ctx, HEAD_DIM // 2), F32)], axis=0)
    return cos.T, sin.T


def _block_diag(w):
    k, c = w.shape[-3], w.shape[-2]
    eye = jnp.eye(k, dtype=w.dtype)
    bd = jnp.einsum('...kce,kj->...kcje', w, eye)
    return bd.reshape(*w.shape[:-3], k * c, k * c)


def kernel(x, c, ctx, c_ctx, w_ada, b_ada, w_in, ret_decay_logit, attn_q_gain, attn_k_gain,
           lru_conv_w, lru_conv_b, lru_w_a, lru_b_a, lru_w_x, lru_b_x, lru_lambda,
           w_out, ln1_g, ln1_b, w_ff1, w_ff2, ln2_g, ln2_b):
    b, n_lat, d = x.shape
    n_ctx = ctx.shape[1]
    depth = w_in.shape[0]
    s = n_lat + n_ctx
    rw, aw, kw = d // 4, d // 2, d // 8
    alpha = (2.0 * depth) ** 0.25
    assert n_lat % CHUNK == 0 and n_ctx % CHUNK == 0 and d == 16 * HEAD_DIM

    pad = (-(b + 1)) % SUBLANES
    s_in = jnp.concatenate([c, c_ctx[None, :], jnp.zeros((pad, d), F32)], axis=0)
    mods = _ada_call(s_in, w_ada, b_ada)
    mod_lat = mods[:, :b].reshape(depth, b, 1, 6, d)
    mod_ctx = jnp.broadcast_to(mods[:, b].reshape(depth, 1, 1, 6, d), (depth, b, 1, 6, d))
    mod_all = jnp.concatenate([mod_lat, mod_ctx], axis=2)

    o_aq = 4 * rw
    o_lx = o_aq + aw + 2 * kw
    wn = jnp.concatenate([w_in[:, :, :o_aq], w_in[:, :, o_lx:]], axis=2).astype(BF16)
    wt = jnp.swapaxes(w_in[:, :, o_aq:o_lx], 1, 2).astype(BF16)
    wg = (0.5 * jnp.concatenate([_block_diag(lru_w_a[:, 0]), _block_diag(lru_w_x[:, 0]),
                                 _block_diag(lru_w_a[:, 1]), _block_diag(lru_w_x[:, 1])],
                                axis=-1)).astype(BF16)
    bg = 0.5 * jnp.concatenate([lru_b_a[:, 0], lru_b_x[:, 0], lru_b_a[:, 1], lru_b_x[:, 1]],
                               axis=-1)[:, None, :]
    w_out_b = w_out.astype(BF16)
    w1_b = w_ff1.astype(BF16)
    w2_b = w_ff2.astype(BF16)
    dl_lane = jnp.repeat(ret_decay_logit, HEAD_DIM, axis=-1)
    cos_t, sin_t = _rope_tables(n_lat, n_ctx)

    def tail_tile(rows):
        return rows // TAIL_TILES if rows % (TAIL_TILES * BF16_ROWS) == 0 else CHUNK

    xa = (x, ctx)
    att = jnp.zeros((b, s, aw), BF16)
    for l in range(depth):
        need_ctx = l < depth - 1
        mod = mod_all[l]
        outs = _inproj_call(xa, mod, wn[l], wt[l], cos_t, sin_t,
                            attn_q_gain[l][:, None], attn_k_gain[l][:, None], n_lat)
        ret_in, g_in, lru_in, qt, k, vt, kn = outs[:7]
        if l == 0:
            xa = outs[7]
        ret = _ret_call(ret_in, g_in, dl_lane[l], n_lat)
        att = _attn_call(qt, k, vt, kn, n_lat, att if need_ctx else None)
        af, df, ab, db = _lru_pre_call(lru_in, lru_conv_w[l], lru_conv_b[l][None, :], wg[l], bg[l],
                                       lru_lambda[l], n_lat)
        hf, hb = _scan_call(af, df, ab, db, n_lat)
        n_rows = s if need_ctx else n_lat
        xa = _tail_call(xa, ret, att, hf, hb, lru_in, mod,
                        w_out_b[l], ln1_g[l][None, :], ln1_b[l][None, :], w1_b[l], w2_b[l],
                        ln2_g[l][None, :], ln2_b[l][None, :], n_lat, n_rows, alpha, tail_tile(n_rows))
    return xa
```
